```python
import math, functools
import jax, jax.numpy as jnp
from jax import lax
import numpy as np


D_MODEL = 1024
BATCH = 4
SEQ = 4096
DEPTH = 1
DEC_BATCH = 32
DEC_SEQ = 1
PAST_LEN = 16384
PAGE_SIZE = 128

HEAD_DIM = 64
RWKV_HEADS = 6
SB_HEADS = 6
X_HEADS = 4
N_MEM = 256
RWKV_W = RWKV_HEADS * HEAD_DIM
SB_W = SB_HEADS * HEAD_DIM
X_W = X_HEADS * HEAD_DIM
D_MIX = RWKV_W + SB_W + X_W
LORA_W = 64
LORA_A = 64
RWKV_COLS = 3 * RWKV_W + LORA_W + LORA_A
SB_COLS = 3 * SB_W
GATE_COLS = D_MIX
D_IN = RWKV_COLS + SB_COLS + X_W + GATE_COLS
Q_BLOCK = 128
NORM_EPS = 1e-6
GN_EPS = 64e-5
DECAY_SCALE = math.exp(-0.5)
ATTN_SCALE = HEAD_DIM ** -0.5
SB_BIAS_INIT = -8.0

kernel_name = "hymba_rwkv7_stickbreaking_memxattn_step"


def rmsnorm(x, g):
    xf = x.astype(jnp.float32)
    y = xf * lax.rsqrt(jnp.mean(xf * xf, axis=-1, keepdims=True) + NORM_EPS)
    return (y * g.astype(jnp.float32)).astype(x.dtype)


def split_heads(t, h):
    return t.reshape(t.shape[:-1] + (h, HEAD_DIM))


def rwkv7_branch(p_rw, prev_row, s0, mu_shift, w0, w_lora_b, a0, a_lora_b, k_k, k_a, r_k, lnx_g, lnx_b):
    f32 = jnp.float32
    pf = p_rw.astype(f32)
    prev = jnp.concatenate([prev_row.astype(f32)[:, None, :], pf[:, :-1]], axis=1)
    xs = pf + (prev - pf) * mu_shift.astype(f32)
    r, k, v, xw, xa = jnp.split(xs, [RWKV_W, 2 * RWKV_W, 3 * RWKV_W, 3 * RWKV_W + LORA_W], axis=-1)
    log_w = -DECAY_SCALE * jax.nn.sigmoid(w0.astype(f32) + jnp.tanh(xw) @ w_lora_b.astype(f32))
    a = jax.nn.sigmoid(a0.astype(f32) + xa @ a_lora_b.astype(f32))
    kk = split_heads(k * k_k.astype(f32), RWKV_HEADS)
    kk = kk * lax.rsqrt(jnp.maximum(jnp.sum(kk * kk, axis=-1, keepdims=True), 1e-12))
    k = k * (1.0 + (a - 1.0) * k_a.astype(f32))
    r, k, v, w, a = [split_heads(t, RWKV_HEADS) for t in (r, k, v, jnp.exp(log_w), a)]

    def step(S, inp):
        r_t, w_t, k_t, v_t, kk_t, a_t = inp
        s_kk = jnp.einsum('bhvk,bhk->bhv', S, kk_t)
        S = (S * w_t[:, :, None, :]
             - s_kk[..., None] * (kk_t * a_t)[:, :, None, :]
             + v_t[..., None] * k_t[:, :, None, :])
        return S, jnp.einsum('bhvk,bhk->bhv', S, r_t)

    seq = tuple(jnp.moveaxis(t, 1, 0) for t in (r, w, k, v, kk, a))
    s_fin, o = lax.scan(step, s0.astype(f32), seq)
    o = jnp.moveaxis(o, 0, 1)
    mean = jnp.mean(o, axis=-1, keepdims=True)
    var = jnp.mean(jnp.square(o - mean), axis=-1, keepdims=True)
    o = (o - mean) * lax.rsqrt(var + GN_EPS)
    o = o.reshape(o.shape[:2] + (RWKV_W,)) * lnx_g.astype(f32) + lnx_b.astype(f32)
    bonus = jnp.sum(r * k * r_k.astype(f32), axis=-1, keepdims=True) * v
    o = o + bonus.reshape(o.shape)
    return o.astype(p_rw.dtype), s_fin.astype(s0.dtype), p_rw[:, -1]


def sb_attend(q, k, v, bias, q_pos, k_pos):
    z = jnp.einsum('bqhd,bkhd->bhqk', q, k, preferred_element_type=jnp.float32) * ATTN_SCALE
    z = z + bias.astype(jnp.float32)[None, :, None, None]
    mask = k_pos[None, :] < q_pos[:, None]
    log_not = jnp.where(mask, jax.nn.log_sigmoid(-z), 0.0)
    after = lax.cumsum(log_not, axis=3, reverse=True) - log_not
    A = jnp.where(mask, jnp.exp(jax.nn.log_sigmoid(z) + after), 0.0)
    o = jnp.einsum('bhqk,bkhd->bqhd', A, v.astype(jnp.float32))
    return o.astype(q.dtype)


def sb_prompt(q, k, v, bias):
    B, T = q.shape[0], q.shape[1]
    nb = T // Q_BLOCK
    qb = jnp.moveaxis(q.reshape(B, nb, Q_BLOCK, SB_HEADS, HEAD_DIM), 1, 0)
    k_pos = jnp.arange(T)

    def blk(args):
        q_i, i = args
        q_pos = i * Q_BLOCK + jnp.arange(Q_BLOCK)
        return sb_attend(q_i, k, v, bias, q_pos, k_pos)

    o = lax.map(blk, (qb, jnp.arange(nb)))
    return jnp.moveaxis(o, 0, 1).reshape(B, T, SB_W)


def sb_sample(q, k, v, bias, past_k, past_v):
    B, T = q.shape[0], q.shape[1]
    P = past_k.shape[1]
    k_all = jnp.concatenate([past_k.astype(k.dtype), k], axis=1)
    v_all = jnp.concatenate([past_v.astype(v.dtype), v], axis=1)
    q_pos = P + jnp.arange(T)
    k_pos = jnp.arange(P + T)
    return sb_attend(q, k_all, v_all, bias, q_pos, k_pos).reshape(B, T, SB_W)


def memory_kv(mem, g, wk, wv):
    h = rmsnorm(mem, g)
    return split_heads(h @ wk, X_HEADS), split_heads(h @ wv, X_HEADS)


def cross_attend(q, mk, mv):
    s = jnp.einsum('bqhd,bmhd->bhqm', q, mk, preferred_element_type=jnp.float32) * ATTN_SCALE
    p = jax.nn.softmax(s, axis=-1)
    o = jnp.einsum('bhqm,bmhd->bqhd', p, mv.astype(jnp.float32))
    return o.reshape(o.shape[:2] + (X_W,)).astype(q.dtype)


def mixer_layer(x, prev_row, s0, mem_k, mem_v, sb_fn, norm_g, w_in, sb_bias, rw, w_out):
    h = rmsnorm(x, norm_g)
    p = h @ w_in
    p_rw, p_sb, p_xq, p_gate = jnp.split(
        p, [RWKV_COLS, RWKV_COLS + SB_COLS, RWKV_COLS + SB_COLS + X_W], axis=-1)
    o_rw, s_fin, last_row = rwkv7_branch(p_rw, prev_row, s0, *rw)
    q, k, v = [split_heads(t, SB_HEADS) for t in jnp.split(p_sb, 3, axis=-1)]
    o_sb = sb_fn(q, k, v, sb_bias)
    o_x = cross_attend(split_heads(p_xq, X_HEADS), mem_k, mem_v)
    o = jnp.concatenate([o_rw, o_sb, o_x], axis=-1) * jax.nn.silu(p_gate)
    return x + o @ w_out, s_fin, last_row, k, v


def setup_inputs(seed: int = 0) -> dict:
    key = jax.random.key(seed)
    ks = jax.random.split(key, 32)
    f32 = jnp.float32
    n_pages = PAST_LEN // PAGE_SIZE
    n_pool = (DEC_BATCH * n_pages * 5) // 4

    def nrm(k, shape, scale):
        return jax.random.normal(k, shape, f32) * scale

    return {
        "x_prompt": nrm(ks[0], (BATCH, SEQ, D_MODEL), 1.0),
        "mem_prompt": nrm(ks[1], (BATCH, N_MEM, D_MODEL), 1.0),
        "x_sample": nrm(ks[2], (DEC_BATCH, DEC_SEQ, D_MODEL), 1.0),
        "cache_sb_k": nrm(ks[3], (DEPTH, n_pool, PAGE_SIZE, SB_HEADS, HEAD_DIM), 1.0),
        "cache_sb_v": nrm(ks[4], (DEPTH, n_pool, PAGE_SIZE, SB_HEADS, HEAD_DIM), 1.0),
        "page_table": jax.random.permutation(ks[5], n_pool)[:DEC_BATCH * n_pages]
                      .reshape(DEC_BATCH, n_pages).astype(jnp.int32),
        "state_wkv": nrm(ks[6], (DEPTH, DEC_BATCH, RWKV_HEADS, HEAD_DIM, HEAD_DIM), 0.3),
        "state_shift": nrm(ks[7], (DEPTH, DEC_BATCH, RWKV_COLS), 1.0),
        "cache_mem_k": nrm(ks[8], (DEPTH, DEC_BATCH, N_MEM, X_HEADS, HEAD_DIM), 1.0),
        "cache_mem_v": nrm(ks[9], (DEPTH, DEC_BATCH, N_MEM, X_HEADS, HEAD_DIM), 1.0),
        "norm_g": 1.0 + nrm(ks[10], (DEPTH, D_MODEL), 0.02),
        "w_in": nrm(ks[11], (DEPTH, D_MODEL, D_IN), D_MODEL ** -0.5),
        "sb_bias": SB_BIAS_INIT + nrm(ks[27], (DEPTH, SB_HEADS), 0.3),
        "mu_shift": jax.random.uniform(ks[12], (DEPTH, RWKV_COLS), f32, 0.1, 0.9),
        "w0": nrm(ks[13], (DEPTH, RWKV_W), 0.5),
        "w_lora_b": nrm(ks[14], (DEPTH, LORA_W, RWKV_W), LORA_W ** -0.5),
        "a0": nrm(ks[15], (DEPTH, RWKV_W), 0.3),
        "a_lora_b": nrm(ks[16], (DEPTH, LORA_A, RWKV_W), LORA_A ** -0.5),
        "k_k": 0.85 + nrm(ks[17], (DEPTH, RWKV_W), 0.05),
        "k_a": 1.0 + nrm(ks[18], (DEPTH, RWKV_W), 0.05),
        "r_k": nrm(ks[19], (DEPTH, RWKV_HEADS, HEAD_DIM), 0.1),
        "lnx_g": 1.0 + nrm(ks[20], (DEPTH, RWKV_W), 0.02),
        "lnx_b": nrm(ks[21], (DEPTH, RWKV_W), 0.02),
        "mem_norm_g": 1.0 + nrm(ks[22], (DEPTH, D_MODEL), 0.02),
        "w_mem_k": nrm(ks[23], (DEPTH, D_MODEL, X_W), D_MODEL ** -0.5),
        "w_mem_v": nrm(ks[24], (DEPTH, D_MODEL, X_W), D_MODEL ** -0.5),
        "w_out": nrm(ks[25], (DEPTH, D_MIX, D_MODEL), D_MIX ** -0.5),
        "final_norm_g": 1.0 + nrm(ks[26], (D_MODEL,), 0.02),
    }


def reference(x_prompt, mem_prompt, x_sample, cache_sb_k, cache_sb_v, page_table,
              state_wkv, state_shift, cache_mem_k, cache_mem_v,
              norm_g, w_in, sb_bias, mu_shift, w0, w_lora_b, a0, a_lora_b, k_k, k_a, r_k,
              lnx_g, lnx_b, mem_norm_g, w_mem_k, w_mem_v, w_out, final_norm_g):
    n_pages = page_table.shape[1]
    b_p = x_prompt.shape[0]
    b_s = x_sample.shape[0]
    xp, xs = x_prompt, x_sample
    sbk_p, sbv_p, wkv_p, sh_p, mk_p_l, mv_p_l = [], [], [], [], [], []
    sbk_s, sbv_s, wkv_s, sh_s = [], [], [], []
    for l in range(DEPTH):
        rw = (mu_shift[l], w0[l], w_lora_b[l], a0[l], a_lora_b[l], k_k[l], k_a[l], r_k[l],
              lnx_g[l], lnx_b[l])
        mk_p, mv_p = memory_kv(mem_prompt, mem_norm_g[l], w_mem_k[l], w_mem_v[l])
        xp, s_p, last_p, k_p, v_p = mixer_layer(
            xp, jnp.zeros((b_p, RWKV_COLS), xp.dtype),
            jnp.zeros((b_p, RWKV_HEADS, HEAD_DIM, HEAD_DIM), state_wkv.dtype),
            mk_p, mv_p, sb_prompt, norm_g[l], w_in[l], sb_bias[l], rw, w_out[l])
        sbk_p.append(k_p); sbv_p.append(v_p); wkv_p.append(s_p); sh_p.append(last_p)
        mk_p_l.append(mk_p); mv_p_l.append(mv_p)
        past_k = cache_sb_k[l][page_table].reshape(b_s, n_pages * PAGE_SIZE, SB_HEADS, HEAD_DIM)
        past_v = cache_sb_v[l][page_table].reshape(b_s, n_pages * PAGE_SIZE, SB_HEADS, HEAD_DIM)
        sb_fn = functools.partial(sb_sample, past_k=past_k, past_v=past_v)
        xs, s_s, last_s, k_s, v_s = mixer_layer(
            xs, state_shift[l], state_wkv[l], cache_mem_k[l], cache_mem_v[l], sb_fn,
            norm_g[l], w_in[l], sb_bias[l], rw, w_out[l])
        sbk_s.append(k_s); sbv_s.append(v_s); wkv_s.append(s_s); sh_s.append(last_s)
    y_prompt = rmsnorm(xp, final_norm_g)
    y_sample = rmsnorm(xs, final_norm_g)
    return (y_prompt, y_sample,
            jnp.stack(sbk_p), jnp.stack(sbv_p), jnp.stack(wkv_p), jnp.stack(sh_p),
            jnp.stack(mk_p_l), jnp.stack(mv_p_l),
            jnp.stack(sbk_s), jnp.stack(sbv_s), jnp.stack(wkv_s), jnp.stack(sh_s))
```

```python
import functools
import math

import jax
import jax.numpy as jnp
from jax import lax
from jax.experimental import pallas as pl
from jax.experimental.pallas import tpu as pltpu

F32 = jnp.float32
BF16 = jnp.bfloat16

HEAD_DIM = 64
RWKV_HEADS = 6
SB_HEADS = 6
X_HEADS = 4
RWKV_W = RWKV_HEADS * HEAD_DIM
SB_W = SB_HEADS * HEAD_DIM
X_W = X_HEADS * HEAD_DIM
LORA = 64
RWKV_COLS = 3 * RWKV_W + 2 * LORA
NORM_EPS = 1e-6
GN_EPS = 64e-5
DECAY_SCALE = math.exp(-0.5)
ATTN_SCALE = HEAD_DIM ** -0.5

LANES = 128
PAIR_W = 2 * HEAD_DIM
RWKV_CHUNK = 128
SB_BLOCK = 256
PAGES_PER_STEP = 8
VMEM_LIMIT = 48 * 1024 * 1024

NN = (((1,), (0,)), ((), ()))
NT = (((1,), (1,)), ((), ()))
TN = (((0,), (0,)), ((), ()))


def _dot(a, b, dims=NN):
    return lax.dot_general(a, b, dims, preferred_element_type=F32)


def _split2(x):
    hi = x.astype(BF16)
    lo = (x - hi.astype(F32)).astype(BF16)
    return hi, lo


def _split3(x):
    hi = x.astype(BF16)
    r1 = x - hi.astype(F32)
    mid = r1.astype(BF16)
    lo = (r1 - mid.astype(F32)).astype(BF16)
    return hi, mid, lo


def _dot_f32(a, b, dims=NN):
    ah, al = _split2(a)
    bh, bl = _split2(b)
    return _dot(ah, bh, dims) + (_dot(ah, bl, dims) + _dot(al, bh, dims))


def _dot_exact_rhs(a, b_bf16, dims=NN):
    hi, mid, lo = _split3(a)
    return _dot(hi, b_bf16, dims) + (_dot(mid, b_bf16, dims) + _dot(lo, b_bf16, dims))


def _dot_exact_lhs(a_bf16, b, dims=NN):
    hi, mid, lo = _split3(b)
    return _dot(a_bf16, hi, dims) + (_dot(a_bf16, mid, dims) + _dot(a_bf16, lo, dims))


def _sigmoid(x):
    return 1.0 / (1.0 + jnp.exp(-x))


def _softplus(z):
    return jnp.maximum(z, 0.0) + jnp.log1p(jnp.exp(-jnp.abs(z)))


def _rmsnorm_rows(x, g):
    ms = jnp.mean(x * x, axis=-1, keepdims=True)
    return x * lax.rsqrt(ms + NORM_EPS) * g


def _params(*sem):
    return pltpu.CompilerParams(dimension_semantics=sem, vmem_limit_bytes=VMEM_LIMIT)


N_A = RWKV_COLS + SB_W
N_B = X_W + RWKV_W + SB_W + X_W


def _in_proj_kernel(x_ref, g_ref, wa_ref, wb_ref, wkvT_ref,
                    prw_ref, q_ref, xq_ref, gate_ref, kT_ref, vT_ref):
    h = _rmsnorm_rows(x_ref[...], g_ref[...]).astype(BF16)
    pa = _dot(h, wa_ref[...])
    prw_ref[...] = pa[:, :RWKV_COLS]
    q_ref[...] = pa[:, RWKV_COLS:]
    pb = _dot(h, wb_ref[...])
    xq_ref[...] = pb[:, :X_W]
    gate_ref[...] = pb[:, X_W:]
    kv = _dot(wkvT_ref[...], h, NT)
    kT_ref[...] = kv[:SB_W]
    vT_ref[...] = kv[SB_W:]


def _in_proj(x, norm_g, wa, wb, wkvT, rows):
    b, t, d = x.shape
    n_gate = N_B - X_W
    row = lambda bi, i: (bi, i, 0)
    col = lambda bi, i: (bi, 0, i)
    const = lambda bi, i: (0, 0)
    out_shape = (
        jax.ShapeDtypeStruct((b, t, RWKV_COLS), F32),
        jax.ShapeDtypeStruct((b, t, SB_W), F32),
        jax.ShapeDtypeStruct((b, t, X_W), F32),
        jax.ShapeDtypeStruct((b, t, n_gate), F32),
        jax.ShapeDtypeStruct((b, SB_W, t), F32),
        jax.ShapeDtypeStruct((b, SB_W, t), F32),
    )
    return pl.pallas_call(
        _in_proj_kernel,
        grid=(b, t // rows),
        in_specs=[
            pl.BlockSpec((None, rows, d), row),
            pl.BlockSpec((1, d), const),
            pl.BlockSpec((d, N_A), const),
            pl.BlockSpec((d, N_B), const),
            pl.BlockSpec((2 * SB_W, d), const),
        ],
        out_specs=(
            pl.BlockSpec((None, rows, RWKV_COLS), row),
            pl.BlockSpec((None, rows, SB_W), row),
            pl.BlockSpec((None, rows, X_W), row),
            pl.BlockSpec((None, rows, n_gate), row),
            pl.BlockSpec((None, SB_W, rows), col),
            pl.BlockSpec((None, SB_W, rows), col),
        ),
        out_shape=out_shape,
        compiler_params=_params("parallel", "parallel"),
        name="in_proj",
    )(x, norm_g, wa, wb, wkvT)


def _mem_kv_kernel(mem_ref, g_ref, wkT_ref, wvT_ref, kT_ref, vT_ref):
    h = _rmsnorm_rows(mem_ref[...], g_ref[...]).astype(BF16)
    kT_ref[...] = _dot(wkT_ref[...], h, NT)
    vT_ref[...] = _dot(wvT_ref[...], h, NT)


def _mem_kv(mem, g, wkT, wvT):
    b, m, d = mem.shape
    const = lambda bi: (0, 0)
    blk = lambda bi: (bi, 0, 0)
    return pl.pallas_call(
        _mem_kv_kernel,
        grid=(b,),
        in_specs=[
            pl.BlockSpec((None, m, d), blk),
            pl.BlockSpec((1, d), const),
            pl.BlockSpec((X_W, d), const),
            pl.BlockSpec((X_W, d), const),
        ],
        out_specs=(pl.BlockSpec((None, X_W, m), blk), pl.BlockSpec((None, X_W, m), blk)),
        out_shape=(jax.ShapeDtypeStruct((b, X_W, m), F32),) * 2,
        compiler_params=_params("parallel"),
        name="mem_kv",
    )(mem, g, wkT, wvT)


def _xattn_kernel(xq_ref, kT_ref, vT_ref, o_ref):
    for h in range(X_HEADS):
        sl = slice(h * HEAD_DIM, (h + 1) * HEAD_DIM)
        q = (xq_ref[:, sl] * ATTN_SCALE).astype(BF16)
        s = _dot(q, kT_ref[sl, :].astype(BF16))
        e = jnp.exp(s - jnp.max(s, axis=-1, keepdims=True))
        l = jnp.sum(e, axis=-1, keepdims=True)
        o_ref[:, sl] = _dot(e.astype(BF16), vT_ref[sl, :].astype(BF16), NT) / l


def _xattn(xq, kT, vT, rows):
    b, t, _ = xq.shape
    m = kT.shape[-1]
    row = lambda bi, i: (bi, i, 0)
    full = lambda bi, i: (bi, 0, 0)
    return pl.pallas_call(
        _xattn_kernel,
        grid=(b, t // rows),
        in_specs=[
            pl.BlockSpec((None, rows, X_W), row),
            pl.BlockSpec((None, X_W, m), full),
            pl.BlockSpec((None, X_W, m), full),
        ],
        out_specs=pl.BlockSpec((None, rows, X_W), row),
        out_shape=jax.ShapeDtypeStruct((b, t, X_W), F32),
        compiler_params=_params("parallel", "parallel"),
        name="xattn",
    )(xq, kT, vT)


def _out_proj_kernel(orw_ref, osb_ref, ox_ref, gate_ref, x_ref, w_ref, fg_ref, y_ref):
    g = gate_ref[...]
    sg = g * _sigmoid(g)
    a0, a1 = RWKV_W, RWKV_W + SB_W
    acc = _dot((orw_ref[...] * sg[:, :a0]).astype(BF16), w_ref[:a0, :])
    acc += _dot((osb_ref[...] * sg[:, a0:a1]).astype(BF16), w_ref[a0:a1, :])
    acc += _dot((ox_ref[...] * sg[:, a1:]).astype(BF16), w_ref[a1:, :])
    y_ref[...] = _rmsnorm_rows(x_ref[...] + acc, fg_ref[...])


def _out_proj(o_rw, o_sb, o_x, gate, x, w_out, fg, rows):
    b, t, d = x.shape
    dm = w_out.shape[0]
    row = lambda bi, i: (bi, i, 0)
    const = lambda bi, i: (0, 0)
    return pl.pallas_call(
        _out_proj_kernel,
        grid=(b, t // rows),
        in_specs=[
            pl.BlockSpec((None, rows, RWKV_W), row),
            pl.BlockSpec((None, rows, SB_W), row),
            pl.BlockSpec((None, rows, X_W), row),
            pl.BlockSpec((None, rows, dm), row),
            pl.BlockSpec((None, rows, d), row),
            pl.BlockSpec((dm, d), const),
            pl.BlockSpec((1, d), const),
        ],
        out_specs=pl.BlockSpec((None, rows, d), row),
        out_shape=jax.ShapeDtypeStruct((b, t, d), F32),
        compiler_params=_params("parallel", "parallel"),
        name="out_proj",
    )(o_rw, o_sb, o_x, gate, x, w_out, fg)


def _rwkv_token_math(r, k, v, xw, xa, w0, a0, k_k, k_a, r_k, wlb, alb, head_ones):
    log_w = -DECAY_SCALE * _sigmoid(w0 + _dot(jnp.tanh(xw).astype(BF16), wlb.astype(BF16)))
    a = _sigmoid(a0 + _dot(xa.astype(BF16), alb.astype(BF16)))
    kk = k * k_k
    ss = _dot_exact_rhs(kk * kk, head_ones)
    kk = kk * lax.rsqrt(jnp.maximum(ss, 1e-12))
    k2 = k * (1.0 + (a - 1.0) * k_a)
    bonus = _dot_exact_rhs(r * k2 * r_k, head_ones) * v
    return log_w, a, kk, k2, bonus


def _group_norm(o, g, b, head_ones):
    inv = 1.0 / HEAD_DIM
    mean = _dot_exact_rhs(o, head_ones) * inv
    d = o - mean
    var = _dot_exact_rhs(d * d, head_ones) * inv
    return d * lax.rsqrt(var + GN_EPS) * g + b


def _shift_mix(cur, last_row, mu, first_row_mask):
    prev = jnp.where(first_row_mask, last_row, pltpu.roll(cur, 1, axis=0))
    return cur + (prev - cur) * mu


def _rwkv_prompt_kernel(
        r_ref, k_ref, v_ref, x_ref, rp_ref, kp_ref, vp_ref, xp_ref,
        r0_ref, k0_ref, v0_ref, x0_ref, mur_ref, muk_ref, muv_ref, mux_ref,
        w0_ref, a0_ref, kk_ref, ka_ref, rk_ref, lg_ref, lb_ref, wlb_ref, alb_ref,
        h0_ref, tri_ref, ones_ref,
        o_ref, hfin_ref, h_scr):
    i = pl.program_id(2)
    c = RWKV_CHUNK

    @pl.when(i == 0)
    def _():
        h_scr[...] = h0_ref[...]

    row = lax.broadcasted_iota(jnp.int32, (c, PAIR_W), 0)
    colid = lax.broadcasted_iota(jnp.int32, (c, c), 1)
    rowid = lax.broadcasted_iota(jnp.int32, (c, c), 0)
    first = row == 0
    is_first_chunk = i == 0

    def mixed(cur_ref, prev_ref, first_ref, mu_ref):
        last = jnp.where(is_first_chunk, first_ref[...], prev_ref[7:8, :])
        return _shift_mix(cur_ref[...], last, mu_ref[...], first)

    r = mixed(r_ref, rp_ref, r0_ref, mur_ref)
    k = mixed(k_ref, kp_ref, k0_ref, muk_ref)
    v = mixed(v_ref, vp_ref, v0_ref, muv_ref)
    x = mixed(x_ref, xp_ref, x0_ref, mux_ref)
    head_ones = ones_ref[...]
    log_w, a, kk, k2, bonus = _rwkv_token_math(
        r, k, v, x[:, :LORA], x[:, LORA:], w0_ref[...], a0_ref[...], kk_ref[...],
        ka_ref[...], rk_ref[...], wlb_ref[...], alb_ref[...], head_ones)

    lam = _dot_exact_lhs(tri_ref[...], log_w)
    lam_c = lam[c - 1:c, :]
    w_in = jnp.exp(lam)
    w_ex = jnp.exp(lam - log_w)
    w_inv = jnp.exp(-lam)
    w_end = jnp.exp(lam_c - lam)
    w_c = jnp.exp(lam_c)
    bvec = kk * a
    at = -kk * w_ex
    bt = bvec * w_inv
    kt = k2 * w_inv
    rt = r * w_in
    bh = bvec * w_end
    kh = k2 * w_end

    strict = colid < rowid
    incl = colid <= rowid
    eye64 = (lax.broadcasted_iota(jnp.int32, (HEAD_DIM, HEAD_DIM), 0)
             == lax.broadcasted_iota(jnp.int32, (HEAD_DIM, HEAD_DIM), 1))

    outs = []
    for hh in range(2):
        sl = slice(hh * HEAD_DIM, (hh + 1) * HEAD_DIM)
        at_h, bt_h, kt_h, rt_h, v_h = at[:, sl], bt[:, sl], kt[:, sl], rt[:, sl], v[:, sl]
        n = jnp.where(strict, _dot_f32(at_h, bt_h, NT), 0.0)
        aak = jnp.where(strict, _dot_f32(at_h, kt_h, NT), 0.0)
        mrb = jnp.where(incl, _dot_f32(rt_h, bt_h, NT), 0.0)
        mrk = jnp.where(incl, _dot_f32(rt_h, kt_h, NT), 0.0)
        tm = jnp.where(colid == rowid, 1.0, n)
        npow = n
        span = 1
        while 2 * span < c:
            npow = _dot_f32(npow, npow)
            tm = tm + _dot_f32(tm, npow)
            span *= 2
        xcat = jnp.concatenate([at_h, _dot_f32(aak, v_h)], axis=1)
        pq = _dot_f32(tm, xcat)
        mpq = _dot_f32(mrb, pq)
        p2 = rt_h + mpq[:, :HEAD_DIM]
        q2 = mpq[:, HEAD_DIM:] + _dot_f32(mrk, v_h)
        gj = _dot_f32(bh[:, sl], pq, TN)
        g = gj[:, :HEAD_DIM] + jnp.where(eye64, w_c[:, sl], 0.0)
        j = gj[:, HEAD_DIM:] + _dot_f32(kh[:, sl], v_h, TN)
        h_prev = h_scr[hh]
        outs.append(_dot_f32(p2, h_prev) + q2)
        h_scr[hh] = _dot_f32(g, h_prev) + j

    o = jnp.concatenate(outs, axis=1)
    o_ref[...] = _group_norm(o, lg_ref[...], lb_ref[...], head_ones) + bonus

    @pl.when(i == pl.num_programs(2) - 1)
    def _():
        hfin_ref[...] = h_scr[...]


def _rwkv_prompt(p_rw, prev0, h0, vecs, wlb, alb, tri, ones):
    b, t, _ = p_rw.shape
    c = RWKV_CHUNK
    npair = RWKV_HEADS // 2
    sub = 8
    mu, w0, a0, k_k, k_a, r_k, lnx_g, lnx_b = vecs

    def cur(off):
        return pl.BlockSpec((None, c, PAIR_W), lambda bi, p, i: (bi, i, off + p))

    def cur_x():
        return pl.BlockSpec((None, c, PAIR_W), lambda bi, p, i: (bi, i, 3 * npair))

    def prev(off):
        return pl.BlockSpec(
            (None, sub, PAIR_W),
            lambda bi, p, i: (bi, jnp.maximum(i * (c // sub) - 1, 0), off + p))

    def prev_x():
        return pl.BlockSpec(
            (None, sub, PAIR_W),
            lambda bi, p, i: (bi, jnp.maximum(i * (c // sub) - 1, 0), 3 * npair))

    def first(off):
        return pl.BlockSpec((None, 1, PAIR_W), lambda bi, p, i: (bi, 0, off + p))

    def first_x():
        return pl.BlockSpec((None, 1, PAIR_W), lambda bi, p, i: (bi, 0, 3 * npair))

    def vec(off):
        return pl.BlockSpec((1, PAIR_W), lambda bi, p, i: (0, off + p))

    vec_x = pl.BlockSpec((1, PAIR_W), lambda bi, p, i: (0, 3 * npair))
    pvec = pl.BlockSpec((1, PAIR_W), lambda bi, p, i: (0, p))
    lora = pl.BlockSpec((LORA, PAIR_W), lambda bi, p, i: (0, p))
    state = pl.BlockSpec((None, 2, HEAD_DIM, HEAD_DIM), lambda bi, p, i: (bi, p, 0, 0))
    const = lambda bi, p, i: (0, 0)

    return pl.pallas_call(
        _rwkv_prompt_kernel,
        grid=(b, npair, t // c),
        in_specs=[
            cur(0), cur(npair), cur(2 * npair), cur_x(),
            prev(0), prev(npair), prev(2 * npair), prev_x(),
            first(0), first(npair), first(2 * npair), first_x(),
            vec(0), vec(npair), vec(2 * npair), vec_x,
            pvec, pvec, pvec, pvec, pvec, pvec, pvec, lora, lora,
            state,
            pl.BlockSpec((c, c), const),
            pl.BlockSpec((PAIR_W, PAIR_W), const),
        ],
        out_specs=(
            pl.BlockSpec((None, c, PAIR_W), lambda bi, p, i: (bi, i, p)),
            state,
        ),
        out_shape=(
            jax.ShapeDtypeStruct((b, t, RWKV_W), F32),
            jax.ShapeDtypeStruct((b, RWKV_HEADS, HEAD_DIM, HEAD_DIM), F32),
        ),
        scratch_shapes=[pltpu.VMEM((2, HEAD_DIM, HEAD_DIM), F32)],
        compiler_params=_params("parallel", "parallel", "arbitrary"),
        name="rwkv_prompt",
    )(p_rw, p_rw, p_rw, p_rw, p_rw, p_rw, p_rw, p_rw, prev0, prev0, prev0, prev0,
      mu, mu, mu, mu, w0, a0, k_k, k_a, r_k, lnx_g, lnx_b, wlb, alb, h0, tri, ones)


def _rwkv_sample_prep_kernel(p_ref, prev_ref, mu_ref, w0_ref, a0_ref, kk_ref, ka_ref, rk_ref,
                             wlb_ref, alb_ref, ones_ref,
                             r_ref, w_ref, k2_ref, v_ref, kkn_ref, b_ref, bonus_ref):
    pf = p_ref[...]
    xs = pf + (prev_ref[...] - pf) * mu_ref[...]
    w3 = 3 * RWKV_W
    r, k, v = xs[:, :RWKV_W], xs[:, RWKV_W:2 * RWKV_W], xs[:, 2 * RWKV_W:w3]
    log_w, a, kk, k2, bonus = _rwkv_token_math(
        r, k, v, xs[:, w3:w3 + LORA], xs[:, w3 + LORA:], w0_ref[...], a0_ref[...], kk_ref[...],
        ka_ref[...], rk_ref[...], wlb_ref[...], alb_ref[...], ones_ref[...])
    r_ref[...] = r
    w_ref[...] = jnp.exp(log_w)
    k2_ref[...] = k2
    v_ref[...] = v
    kkn_ref[...] = kk
    b_ref[...] = kk * a
    bonus_ref[...] = bonus


def _rwkv_sample_prep(p, prev, vecs, wlb, alb, ones):
    n = p.shape[0]
    mu, w0, a0, k_k, k_a, r_k = vecs
    return pl.pallas_call(
        _rwkv_sample_prep_kernel,
        out_shape=(jax.ShapeDtypeStruct((n, RWKV_W), F32),) * 7,
        name="rwkv_sample_prep",
    )(p, prev, mu, w0, a0, k_k, k_a, r_k, wlb, alb, ones)


def _rwkv_sample_step_kernel(s_ref, r_ref, w_ref, k_ref, kk_ref, b_ref,
                             vcol_ref, bonus_ref, g_ref, beta_ref, o_ref, snew_ref):
    s = s_ref[...]
    s_kk = jnp.sum(s * kk_ref[...], axis=2, keepdims=True)
    s = s * w_ref[...] - s_kk * b_ref[...] + vcol_ref[...] * k_ref[...]
    snew_ref[...] = s
    o = jnp.sum(s * r_ref[...], axis=2, keepdims=True)
    mean = jnp.mean(o, axis=1, keepdims=True)
    d = o - mean
    var = jnp.mean(d * d, axis=1, keepdims=True)
    o_ref[...] = d * lax.rsqrt(var + GN_EPS) * g_ref[...] + beta_ref[...] + bonus_ref[...]


def _rwkv_sample_step(s0, rows, cols, g_col, beta_col):
    n = s0.shape[0]
    h, d = RWKV_HEADS, HEAD_DIM
    st = pl.BlockSpec((None, h, d, d), lambda bi: (bi, 0, 0, 0))
    rowspec = pl.BlockSpec((None, h, 1, d), lambda bi: (bi, 0, 0, 0))
    colspec = pl.BlockSpec((None, h, d, 1), lambda bi: (bi, 0, 0, 0))
    cconst = pl.BlockSpec((h, d, 1), lambda bi: (0, 0, 0))
    return pl.pallas_call(
        _rwkv_sample_step_kernel,
        grid=(n,),
        in_specs=[st] + [rowspec] * 5 + [colspec] * 2 + [cconst] * 2,
        out_specs=(colspec, st),
        out_shape=(jax.ShapeDtypeStruct((n, h, d, 1), F32),
                   jax.ShapeDtypeStruct((n, h, d, d), F32)),
        compiler_params=_params("parallel"),
        name="rwkv_sample_step",
    )(s0, *rows, *cols, g_col, beta_col)


def _sb_prompt_kernel(bias_ref, q_ref, kT_ref, vT_ref, cs_ref, o_ref):
    pair = pl.program_id(1)
    qi = pl.program_id(2)
    blk = SB_BLOCK
    cs_mat = cs_ref[...]
    q_all = q_ref[...] * ATTN_SCALE
    valid = (lax.broadcasted_iota(jnp.int32, (blk, blk), 1)
             < lax.broadcasted_iota(jnp.int32, (blk, blk), 0))

    outs = []
    for hh in range(2):
        rows = slice(hh * HEAD_DIM, (hh + 1) * HEAD_DIM)
        q = q_all[:, rows].astype(BF16)
        bias = bias_ref[2 * pair + hh]

        def block(j, carry, o_acc, mask):
            start = pl.multiple_of(j * blk, blk)
            k_blk = kT_ref[rows, pl.ds(start, blk)].astype(BF16)
            v_blk = vT_ref[rows, pl.ds(start, blk)].astype(BF16)
            z = _dot(q, k_blk) + bias
            sp = _softplus(z)
            log_not = -sp if mask is None else jnp.where(mask, -sp, 0.0)
            hi, lo = _split2(log_not)
            cs = _dot(hi, cs_mat) + _dot(lo, cs_mat)
            log_a = (z - sp) + cs[:, :blk] + carry
            a = jnp.exp(log_a)
            if mask is not None:
                a = jnp.where(mask, a, 0.0)
            o_acc = o_acc + _dot(a.astype(BF16), v_blk, NT)
            return carry + cs[:, blk:], o_acc

        carry0 = jnp.zeros((blk, blk), F32)
        o0 = jnp.zeros((blk, HEAD_DIM), F32)
        carry, o_acc = block(qi, carry0, o0, valid)

        def body(jj, st):
            return block(qi - 1 - jj, st[0], st[1], None)

        carry, o_acc = lax.fori_loop(0, qi, body, (carry, o_acc))
        outs.append(o_acc)
    o_ref[...] = jnp.concatenate(outs, axis=1)


def _sb_prompt(q, kT, vT, bias, cs_mat):
    b, t, _ = q.shape
    blk = SB_BLOCK
    npair = SB_HEADS // 2
    kv = pl.BlockSpec((None, PAIR_W, t), lambda bi, p, i: (bi, p, 0))
    qo = pl.BlockSpec((None, blk, PAIR_W), lambda bi, p, i: (bi, i, p))
    return pl.pallas_call(
        _sb_prompt_kernel,
        grid=(b, npair, t // blk),
        in_specs=[
            pl.BlockSpec(memory_space=pltpu.SMEM),
            qo, kv, kv,
            pl.BlockSpec((blk, 2 * blk), lambda bi, p, i: (0, 0)),
        ],
        out_specs=qo,
        out_shape=jax.ShapeDtypeStruct((b, t, SB_W), F32),
        compiler_params=_params("parallel", "parallel", "arbitrary"),
        name="sb_prompt",
    )(bias, q, kT, vT, cs_mat)


def _suffix_sum_lanes(x):
    lane = lax.broadcasted_iota(jnp.int32, x.shape, 1)
    d = 1
    while d < LANES:
        shifted = pltpu.roll(x, LANES - d, axis=1)
        x = x + jnp.where(lane < LANES - d, shifted, 0.0)
        d *= 2
    return x


def _sb_sample_kernel(pt_ref, *refs):
    npg = PAGES_PER_STEP
    k_refs = refs[:npg]
    v_refs = refs[npg:2 * npg]
    q_ref, bias_ref, o_ref, acc_ref, carry_ref = refs[2 * npg:]
    j = pl.program_id(1)

    @pl.when(j == 0)
    def _():
        acc_ref[...] = jnp.zeros_like(acc_ref)
        carry_ref[...] = jnp.zeros_like(carry_ref)

    for u in range(npg):
        for h in range(SB_HEADS):
            z = jnp.sum(k_refs[u][h] * q_ref[h], axis=0, keepdims=True) + bias_ref[h]
            sp = _softplus(z)
            suffix = _suffix_sum_lanes(-sp)
            carry = carry_ref[h]
            a = jnp.exp(z + suffix + carry)
            acc_ref[h] += v_refs[u][h] * a
            carry_ref[h] = carry + jnp.sum(-sp, axis=1, keepdims=True)

    @pl.when(j == pl.num_programs(1) - 1)
    def _():
        o_ref[...] = jnp.sum(acc_ref[...], axis=2)


def _sb_sample(page_table, cache_k, cache_v, q_bcast, bias_bcast):
    n, n_pages = page_table.shape
    npg = PAGES_PER_STEP
    h, d, pg = cache_k.shape[1:]
    last = n_pages - 1

    def page_spec(u):
        return pl.BlockSpec(
            (None, h, d, pg),
            lambda bi, j, pt: (pt[bi, last - (j * npg + u)], 0, 0, 0))

    grid_spec = pltpu.PrefetchScalarGridSpec(
        num_scalar_prefetch=1,
        grid=(n, n_pages // npg),
        in_specs=[page_spec(u) for u in range(npg)] * 2 + [
            pl.BlockSpec((None, h, d, pg), lambda bi, j, pt: (bi, 0, 0, 0)),
            pl.BlockSpec((h, 1, pg), lambda bi, j, pt: (0, 0, 0)),
        ],
        out_specs=pl.BlockSpec((None, h, d), lambda bi, j, pt: (bi, 0, 0)),
        scratch_shapes=[pltpu.VMEM((h, d, pg), F32), pltpu.VMEM((h, 1, pg), F32)],
    )
    return pl.pallas_call(
        _sb_sample_kernel,
        grid_spec=grid_spec,
        out_shape=jax.ShapeDtypeStruct((n, h, d), F32),
        compiler_params=_params("parallel", "arbitrary"),
        name="sb_sample",
    )(page_table, *([cache_k] * npg), *([cache_v] * npg), q_bcast, bias_bcast)


def _head_ones(width):
    hid = jnp.arange(width) // HEAD_DIM
    return (hid[:, None] == hid[None, :]).astype(BF16)


def _rows_for(t, target):
    return target if t % target == 0 else t


def kernel(x_prompt, mem_prompt, x_sample, cache_sb_k, cache_sb_v, page_table, state_wkv, state_shift, cache_mem_k, cache_mem_v, norm_g, w_in, sb_bias, mu_shift, w0, w_lora_b, a0, a_lora_b, k_k, k_a, r_k, lnx_g, lnx_b, mem_norm_g, w_mem_k, w_mem_v, w_out, final_norm_g):
    depth = w_in.shape[0]
    assert depth == 1, "single-layer trunk"
    l = 0
    b_p, t_p, d = x_prompt.shape
    b_s = x_sample.shape[0]
    assert x_sample.shape[1] == 1
    n_mem = mem_prompt.shape[1]
    page = cache_sb_k.shape[2]

    kv0 = RWKV_COLS + SB_W
    kv1 = kv0 + 2 * SB_W
    w = w_in[l]
    wa = w[:, :kv0].astype(BF16)
    wb = w[:, kv1:].astype(BF16)
    wkvT = w[:, kv0:kv1].T.astype(BF16)
    wkT = w_mem_k[l].T.astype(BF16)
    wvT = w_mem_v[l].T.astype(BF16)
    wo = w_out[l].astype(BF16)
    fg = final_norm_g.reshape(1, d)
    r_k_flat = r_k[l].reshape(1, RWKV_W)
    ones_pair = _head_ones(PAIR_W)
    ones_all = _head_ones(RWKV_W)
    c = RWKV_CHUNK
    tri = (jnp.arange(c)[None, :] <= jnp.arange(c)[:, None]).astype(BF16)
    blk = SB_BLOCK
    ar = jnp.arange(blk)
    cs_mat = jnp.concatenate(
        [(ar[:, None] > ar[None, :]).astype(BF16), jnp.ones((blk, blk), BF16)], axis=1)

    mkT, mvT = _mem_kv(mem_prompt, mem_norm_g[l:l + 1], wkT, wvT)
    p_rw, q_sb, xq, gate, kT, vT = _in_proj(
        x_prompt, norm_g[l:l + 1], wa, wb, wkvT, _rows_for(t_p, 256))
    vecs = (mu_shift[l:l + 1], w0[l:l + 1], a0[l:l + 1], k_k[l:l + 1], k_a[l:l + 1],
            r_k_flat, lnx_g[l:l + 1], lnx_b[l:l + 1])
    o_rw, h_fin = _rwkv_prompt(
        p_rw, jnp.zeros((b_p, 1, RWKV_COLS), F32),
        jnp.zeros((b_p, RWKV_HEADS, HEAD_DIM, HEAD_DIM), F32),
        vecs, w_lora_b[l], a_lora_b[l], tri, ones_pair)
    o_sb = _sb_prompt(q_sb, kT, vT, sb_bias[l], cs_mat)
    o_x = _xattn(xq, mkT, mvT, _rows_for(t_p, 512))
    y_prompt = _out_proj(o_rw, o_sb, o_x, gate, x_prompt, wo, fg, _rows_for(t_p, 512))

    def tokens_major(xT, heads):
        return xT.reshape(1, xT.shape[0], heads, HEAD_DIM, xT.shape[2]).transpose(0, 1, 4, 2, 3)

    new_sb_k_p = tokens_major(kT, SB_HEADS)
    new_sb_v_p = tokens_major(vT, SB_HEADS)
    new_wkv_p = jnp.swapaxes(h_fin, -1, -2)[None]
    new_shift_p = p_rw[:, -1][None]
    new_mem_k_p = tokens_major(mkT, X_HEADS)
    new_mem_v_p = tokens_major(mvT, X_HEADS)

    xs2 = x_sample.reshape(1, b_s, d)
    p_rw_s, q_s, xq_s, gate_s, kT_s, vT_s = _in_proj(xs2, norm_g[l:l + 1], wa, wb, wkvT, b_s)
    p_rw_s = p_rw_s[0]
    svecs = (mu_shift[l:l + 1], w0[l:l + 1], a0[l:l + 1], k_k[l:l + 1], k_a[l:l + 1], r_k_flat)
    r_s, w_s, k2_s, v_s, kk_s, bv_s, bonus_s = _rwkv_sample_prep(
        p_rw_s, state_shift[l], svecs, w_lora_b[l], a_lora_b[l], ones_all)
    as_row = lambda a_: a_.reshape(b_s, RWKV_HEADS, 1, HEAD_DIM)
    as_col = lambda a_: a_.reshape(b_s, RWKV_HEADS, HEAD_DIM, 1)
    o_rw_s, new_wkv_s = _rwkv_sample_step(
        state_wkv[l],
        [as_row(r_s), as_row(w_s), as_row(k2_s), as_row(kk_s), as_row(bv_s)],
        [as_col(v_s), as_col(bonus_s)],
        lnx_g[l].reshape(RWKV_HEADS, HEAD_DIM, 1), lnx_b[l].reshape(RWKV_HEADS, HEAD_DIM, 1))
    o_rw_s = o_rw_s.reshape(1, b_s, RWKV_W)

    ck = cache_sb_k[l].transpose(0, 2, 3, 1)
    cv = cache_sb_v[l].transpose(0, 2, 3, 1)
    q_b = jnp.broadcast_to(
        (q_s[0] * ATTN_SCALE).reshape(b_s, SB_HEADS, HEAD_DIM, 1), (b_s, SB_HEADS, HEAD_DIM, page))
    bias_b = jnp.broadcast_to(sb_bias[l].reshape(SB_HEADS, 1, 1), (SB_HEADS, 1, page))
    o_sb_s = _sb_sample(page_table, ck, cv, q_b, bias_b).reshape(1, b_s, SB_W)

    mk_s = cache_mem_k[l].transpose(0, 2, 3, 1).reshape(b_s, X_W, n_mem)
    mv_s = cache_mem_v[l].transpose(0, 2, 3, 1).reshape(b_s, X_W, n_mem)
    xq_rows = jnp.broadcast_to(xq_s[0][:, None, :], (b_s, 8, X_W))
    o_x_s = _xattn(xq_rows, mk_s, mv_s, 8)[:, 0][None]
    y_sample = _out_proj(o_rw_s, o_sb_s, o_x_s, gate_s, xs2, wo, fg, b_s).reshape(b_s, 1, d)

    new_sb_k_s = kT_s[0].T.reshape(1, b_s, 1, SB_HEADS, HEAD_DIM)
    new_sb_v_s = vT_s[0].T.reshape(1, b_s, 1, SB_HEADS, HEAD_DIM)
    new_shift_s = p_rw_s[None]

    return (y_prompt, y_sample, new_sb_k_p, new_sb_v_p, new_wkv_p, new_shift_p,
            new_mem_k_p, new_mem_v_p, new_sb_k_s, new_sb_v_s, new_wkv_s[None], new_shift_s)
```

```python
import functools
import math

import jax
import jax.numpy as jnp
from jax import lax
from jax.experimental import pallas as pl
from jax.experimental.pallas import tpu as pltpu

F32 = jnp.float32
BF16 = jnp.bfloat16

HEAD_DIM = 64
RWKV_HEADS = 6
SB_HEADS = 6
X_HEADS = 4
RWKV_W = RWKV_HEADS * HEAD_DIM
SB_W = SB_HEADS * HEAD_DIM
X_W = X_HEADS * HEAD_DIM
LORA = 64
RWKV_COLS = 3 * RWKV_W + 2 * LORA
NORM_EPS = 1e-6
GN_EPS = 64e-5
DECAY_SCALE = math.exp(-0.5)
ATTN_SCALE = HEAD_DIM ** -0.5
LOG2E = math.log2(math.e)

LANES = 128
PAIR_W = 2 * HEAD_DIM
RWKV_CHUNK = 128
INV_BASE = 8
SB_BLOCK = 256
SB_Q_SUB = 2
PAGES_PER_STEP = 16
VMEM_LIMIT = 48 * 1024 * 1024

NN = (((1,), (0,)), ((), ()))
NT = (((1,), (1,)), ((), ()))
TN = (((0,), (0,)), ((), ()))


def _dot(a, b, dims=NN):
    return lax.dot_general(a, b, dims, preferred_element_type=F32)


def _split2(x):
    hi = x.astype(BF16)
    lo = (x - hi.astype(F32)).astype(BF16)
    return hi, lo


def _split3(x):
    hi = x.astype(BF16)
    r1 = x - hi.astype(F32)
    mid = r1.astype(BF16)
    lo = (r1 - mid.astype(F32)).astype(BF16)
    return hi, mid, lo


def _dot_f32(a, b, dims=NN):
    ah, al = _split2(a)
    bh, bl = _split2(b)
    return _dot(ah, bh, dims) + (_dot(ah, bl, dims) + _dot(al, bh, dims))


def _dot_bf16(a, b, dims=NN):
    return _dot(a.astype(BF16), b.astype(BF16), dims)


def _dot_exact_rhs(a, b_bf16, dims=NN):
    hi, mid, lo = _split3(a)
    return _dot(hi, b_bf16, dims) + (_dot(mid, b_bf16, dims) + _dot(lo, b_bf16, dims))


def _dot_exact_lhs(a_bf16, b, dims=NN):
    hi, mid, lo = _split3(b)
    return _dot(a_bf16, hi, dims) + (_dot(a_bf16, mid, dims) + _dot(a_bf16, lo, dims))


def _sigmoid(x):
    return 1.0 / (1.0 + jnp.exp(-x))


def _softplus_pair(z):
    sp = jnp.maximum(z, 0.0) + jnp.log(1.0 + jnp.exp2(jnp.abs(z) * -LOG2E))
    return sp, z - sp


def _rmsnorm_rows(x, g):
    ms = jnp.mean(x * x, axis=-1, keepdims=True)
    return x * lax.rsqrt(ms + NORM_EPS) * g


def _params(*sem):
    return pltpu.CompilerParams(dimension_semantics=sem, vmem_limit_bytes=VMEM_LIMIT)


N_A = RWKV_COLS + SB_W
N_B = X_W + RWKV_W + SB_W + X_W


def _in_proj_kernel(x_ref, g_ref, wa_ref, wb_ref, wkvT_ref,
                    prw_ref, q_ref, xq_ref, gate_ref, kT_ref, vT_ref):
    h = _rmsnorm_rows(x_ref[...], g_ref[...]).astype(BF16)
    pa = _dot(h, wa_ref[...])
    prw_ref[...] = pa[:, :RWKV_COLS]
    q_ref[...] = pa[:, RWKV_COLS:]
    pb = _dot(h, wb_ref[...])
    xq_ref[...] = pb[:, :X_W]
    gate_ref[...] = pb[:, X_W:]
    kv = _dot(wkvT_ref[...], h, NT)
    kT_ref[...] = kv[:SB_W]
    vT_ref[...] = kv[SB_W:]


def _in_proj(x, norm_g, wa, wb, wkvT, rows):
    b, t, d = x.shape
    n_gate = N_B - X_W
    row = lambda bi, i: (bi, i, 0)
    col = lambda bi, i: (bi, 0, i)
    const = lambda bi, i: (0, 0)
    out_shape = (
        jax.ShapeDtypeStruct((b, t, RWKV_COLS), F32),
        jax.ShapeDtypeStruct((b, t, SB_W), F32),
        jax.ShapeDtypeStruct((b, t, X_W), F32),
        jax.ShapeDtypeStruct((b, t, n_gate), F32),
        jax.ShapeDtypeStruct((b, SB_W, t), F32),
        jax.ShapeDtypeStruct((b, SB_W, t), F32),
    )
    return pl.pallas_call(
        _in_proj_kernel,
        grid=(b, t // rows),
        in_specs=[
            pl.BlockSpec((None, rows, d), row),
            pl.BlockSpec((1, d), const),
            pl.BlockSpec((d, N_A), const),
            pl.BlockSpec((d, N_B), const),
            pl.BlockSpec((2 * SB_W, d), const),
        ],
        out_specs=(
            pl.BlockSpec((None, rows, RWKV_COLS), row),
            pl.BlockSpec((None, rows, SB_W), row),
            pl.BlockSpec((None, rows, X_W), row),
            pl.BlockSpec((None, rows, n_gate), row),
            pl.BlockSpec((None, SB_W, rows), col),
            pl.BlockSpec((None, SB_W, rows), col),
        ),
        out_shape=out_shape,
        compiler_params=_params("parallel", "parallel"),
        name="in_proj",
    )(x, norm_g, wa, wb, wkvT)


def _mem_kv_kernel(mem_ref, g_ref, wkT_ref, wvT_ref, kT_ref, vT_ref):
    h = _rmsnorm_rows(mem_ref[...], g_ref[...]).astype(BF16)
    kT_ref[...] = _dot(wkT_ref[...], h, NT)
    vT_ref[...] = _dot(wvT_ref[...], h, NT)


def _mem_kv(mem, g, wkT, wvT):
    b, m, d = mem.shape
    const = lambda bi: (0, 0)
    blk = lambda bi: (bi, 0, 0)
    return pl.pallas_call(
        _mem_kv_kernel,
        grid=(b,),
        in_specs=[
            pl.BlockSpec((None, m, d), blk),
            pl.BlockSpec((1, d), const),
            pl.BlockSpec((X_W, d), const),
            pl.BlockSpec((X_W, d), const),
        ],
        out_specs=(pl.BlockSpec((None, X_W, m), blk), pl.BlockSpec((None, X_W, m), blk)),
        out_shape=(jax.ShapeDtypeStruct((b, X_W, m), F32),) * 2,
        compiler_params=_params("parallel"),
        name="mem_kv",
    )(mem, g, wkT, wvT)


def _xattn_kernel(xq_ref, kT_ref, vT_ref, o_ref):
    for h in range(X_HEADS):
        sl = slice(h * HEAD_DIM, (h + 1) * HEAD_DIM)
        q = (xq_ref[:, sl] * ATTN_SCALE).astype(BF16)
        s = _dot(q, kT_ref[sl, :].astype(BF16))
        e = jnp.exp(s - jnp.max(s, axis=-1, keepdims=True))
        l = jnp.sum(e, axis=-1, keepdims=True)
        o_ref[:, sl] = _dot(e.astype(BF16), vT_ref[sl, :].astype(BF16), NT) / l


def _xattn(xq, kT, vT, rows):
    b, t, _ = xq.shape
    m = kT.shape[-1]
    row = lambda bi, i: (bi, i, 0)
    full = lambda bi, i: (bi, 0, 0)
    return pl.pallas_call(
        _xattn_kernel,
        grid=(b, t // rows),
        in_specs=[
            pl.BlockSpec((None, rows, X_W), row),
            pl.BlockSpec((None, X_W, m), full),
            pl.BlockSpec((None, X_W, m), full),
        ],
        out_specs=pl.BlockSpec((None, rows, X_W), row),
        out_shape=jax.ShapeDtypeStruct((b, t, X_W), F32),
        compiler_params=_params("parallel", "parallel"),
        name="xattn",
    )(xq, kT, vT)


def _out_proj_kernel(orw_ref, osb_ref, ox_ref, gate_ref, x_ref, w_ref, fg_ref, y_ref):
    g = gate_ref[...]
    sg = g * _sigmoid(g)
    a0, a1 = RWKV_W, RWKV_W + SB_W
    acc = _dot((orw_ref[...] * sg[:, :a0]).astype(BF16), w_ref[:a0, :])
    acc += _dot((osb_ref[...] * sg[:, a0:a1]).astype(BF16), w_ref[a0:a1, :])
    acc += _dot((ox_ref[...] * sg[:, a1:]).astype(BF16), w_ref[a1:, :])
    y_ref[...] = _rmsnorm_rows(x_ref[...] + acc, fg_ref[...])


def _out_proj(o_rw, o_sb, o_x, gate, x, w_out, fg, rows):
    b, t, d = x.shape
    dm = w_out.shape[0]
    row = lambda bi, i: (bi, i, 0)
    const = lambda bi, i: (0, 0)
    return pl.pallas_call(
        _out_proj_kernel,
        grid=(b, t // rows),
        in_specs=[
            pl.BlockSpec((None, rows, RWKV_W), row),
            pl.BlockSpec((None, rows, SB_W), row),
            pl.BlockSpec((None, rows, X_W), row),
            pl.BlockSpec((None, rows, dm), row),
            pl.BlockSpec((None, rows, d), row),
            pl.BlockSpec((dm, d), const),
            pl.BlockSpec((1, d), const),
        ],
        out_specs=pl.BlockSpec((None, rows, d), row),
        out_shape=jax.ShapeDtypeStruct((b, t, d), F32),
        compiler_params=_params("parallel", "parallel"),
        name="out_proj",
    )(o_rw, o_sb, o_x, gate, x, w_out, fg)


def _rwkv_token_math(r, k, v, xw, xa, w0, a0, k_k, k_a, r_k, wlb, alb, head_ones):
    log_w = -DECAY_SCALE * _sigmoid(w0 + _dot(jnp.tanh(xw).astype(BF16), wlb.astype(BF16)))
    a = _sigmoid(a0 + _dot(xa.astype(BF16), alb.astype(BF16)))
    kk = k * k_k
    ss = _dot_exact_rhs(kk * kk, head_ones)
    kk = kk * lax.rsqrt(jnp.maximum(ss, 1e-12))
    k2 = k * (1.0 + (a - 1.0) * k_a)
    bonus = _dot_exact_rhs(r * k2 * r_k, head_ones) * v
    return log_w, a, kk, k2, bonus


def _group_norm(o, g, b, head_ones):
    inv = 1.0 / HEAD_DIM
    mean = _dot_exact_rhs(o, head_ones) * inv
    d = o - mean
    var = _dot_exact_rhs(d * d, head_ones) * inv
    return d * lax.rsqrt(var + GN_EPS) * g + b


def _shift_mix(cur, last_row, mu, first_row_mask):
    prev = jnp.where(first_row_mask, last_row, pltpu.roll(cur, 1, axis=0))
    return cur + (prev - cur) * mu


def _rwkv_prompt_kernel(
        p_ref, prev_ref, first_ref, mu_ref,
        w0_ref, a0_ref, kk_ref, ka_ref, rk_ref, lg_ref, lb_ref, wlb_ref, alb_ref,
        h0_ref, tri_ref, ones_ref,
        o_ref, hfin_ref, h_scr):
    i = pl.program_id(1)
    c = RWKV_CHUNK
    w3 = 3 * RWKV_W

    @pl.when(i == 0)
    def _():
        h_scr[...] = h0_ref[...]

    first = lax.broadcasted_iota(jnp.int32, (c, RWKV_COLS), 0) == 0
    last = jnp.where(i == 0, first_ref[...], prev_ref[7:8, :])
    xs = _shift_mix(p_ref[...], last, mu_ref[...], first)
    r, k, v = xs[:, :RWKV_W], xs[:, RWKV_W:2 * RWKV_W], xs[:, 2 * RWKV_W:w3]
    head_ones = ones_ref[...]
    log_w, a, kk, k2, bonus = _rwkv_token_math(
        r, k, v, xs[:, w3:w3 + LORA], xs[:, w3 + LORA:], w0_ref[...], a0_ref[...], kk_ref[...],
        ka_ref[...], rk_ref[...], wlb_ref[...], alb_ref[...], head_ones)

    lam = _dot_exact_lhs(tri_ref[...], log_w)
    lam_c = lam[c - 1:c, :]
    w_in = jnp.exp(lam)
    w_ex = jnp.exp(lam - log_w)
    w_inv = jnp.exp(-lam)
    w_end = jnp.exp(lam_c - lam)
    w_c = jnp.exp(lam_c)
    bvec = kk * a
    at = -kk * w_ex
    bt = bvec * w_inv
    kt = k2 * w_inv
    rt = r * w_in
    bh = bvec * w_end
    kh = k2 * w_end

    colid = lax.broadcasted_iota(jnp.int32, (c, c), 1)
    rowid = lax.broadcasted_iota(jnp.int32, (c, c), 0)
    strict = colid < rowid
    incl = colid <= rowid
    eye64 = (lax.broadcasted_iota(jnp.int32, (HEAD_DIM, HEAD_DIM), 0)
             == lax.broadcasted_iota(jnp.int32, (HEAD_DIM, HEAD_DIM), 1))

    heads = range(RWKV_HEADS)
    hs = lambda x: [x[:, hh * HEAD_DIM:(hh + 1) * HEAD_DIM] for hh in heads]
    at_h, bt_h, kt_h, rt_h, v_h, bh_h, kh_h, wc_h = map(hs, (at, bt, kt, rt, v, bh, kh, w_c))
    n = [jnp.where(strict, _dot_f32(at_h[hh], bt_h[hh], NT), 0.0) for hh in heads]
    aak = [jnp.where(strict, _dot_f32(at_h[hh], kt_h[hh], NT), 0.0) for hh in heads]
    mrb = [jnp.where(incl, _dot_f32(rt_h[hh], bt_h[hh], NT), 0.0) for hh in heads]
    mrk = [jnp.where(incl, _dot_f32(rt_h[hh], kt_h[hh], NT), 0.0) for hh in heads]
    rb, cb = rowid // INV_BASE, colid // INV_BASE
    dg = [jnp.where(rb == cb, n[hh], 0.0) for hh in heads]
    tm = [jnp.where(colid == rowid, 1.0, dg[hh]) for hh in heads]
    span = 1
    while 2 * span < INV_BASE:
        dg = [_dot_bf16(dg[hh], dg[hh]) for hh in heads]
        tm = [tm[hh] + _dot_bf16(tm[hh], dg[hh]) for hh in heads]
        span *= 2
    m = INV_BASE
    while m < c:
        off = (rowid // (2 * m) == colid // (2 * m)) & (rowid // m != colid // m)
        tn = [_dot_bf16(tm[hh], jnp.where(off, n[hh], 0.0)) for hh in heads]
        tm = [tm[hh] + _dot_bf16(tn[hh], tm[hh]) for hh in heads]
        m *= 2
    akv = [_dot_f32(aak[hh], v_h[hh]) for hh in heads]
    mkv = [_dot_f32(mrk[hh], v_h[hh]) for hh in heads]
    khv = [_dot_f32(kh_h[hh], v_h[hh], TN) for hh in heads]
    pq = [_dot_f32(tm[hh], jnp.concatenate([at_h[hh], akv[hh]], axis=1)) for hh in heads]
    mpq = [_dot_f32(mrb[hh], pq[hh]) for hh in heads]
    gj = [_dot_f32(bh_h[hh], pq[hh], TN) for hh in heads]
    h_prev = [h_scr[hh] for hh in heads]
    outs = [
        _dot_f32(rt_h[hh] + mpq[hh][:, :HEAD_DIM], h_prev[hh]) + (mpq[hh][:, HEAD_DIM:] + mkv[hh])
        for hh in heads]
    for hh in heads:
        g = gj[hh][:, :HEAD_DIM] + jnp.where(eye64, wc_h[hh], 0.0)
        h_scr[hh] = _dot_f32(g, h_prev[hh]) + (gj[hh][:, HEAD_DIM:] + khv[hh])

    o = jnp.concatenate(outs, axis=1)
    o_ref[...] = _group_norm(o, lg_ref[...], lb_ref[...], head_ones) + bonus

    @pl.when(i == pl.num_programs(1) - 1)
    def _():
        hfin_ref[...] = h_scr[...]


def _rwkv_prompt(p_rw, prev0, h0, vecs, wlb, alb, tri, ones):
    b, t, _ = p_rw.shape
    c = RWKV_CHUNK
    sub = 8
    mu, w0, a0, k_k, k_a, r_k, lnx_g, lnx_b = vecs
    const = lambda bi, i: (0, 0)
    hvec = pl.BlockSpec((1, RWKV_W), const)
    lora = pl.BlockSpec((LORA, RWKV_W), const)
    state = pl.BlockSpec((None, RWKV_HEADS, HEAD_DIM, HEAD_DIM), lambda bi, i: (bi, 0, 0, 0))

    return pl.pallas_call(
        _rwkv_prompt_kernel,
        grid=(b, t // c),
        in_specs=[
            pl.BlockSpec((None, c, RWKV_COLS), lambda bi, i: (bi, i, 0)),
            pl.BlockSpec((None, sub, RWKV_COLS),
                         lambda bi, i: (bi, jnp.maximum(i * (c // sub) - 1, 0), 0)),
            pl.BlockSpec((None, 1, RWKV_COLS), lambda bi, i: (bi, 0, 0)),
            pl.BlockSpec((1, RWKV_COLS), const),
            hvec, hvec, hvec, hvec, hvec, hvec, hvec, lora, lora,
            state,
            pl.BlockSpec((c, c), const),
            pl.BlockSpec((RWKV_W, RWKV_W), const),
        ],
        out_specs=(
            pl.BlockSpec((None, c, RWKV_W), lambda bi, i: (bi, i, 0)),
            state,
        ),
        out_shape=(
            jax.ShapeDtypeStruct((b, t, RWKV_W), F32),
            jax.ShapeDtypeStruct((b, RWKV_HEADS, HEAD_DIM, HEAD_DIM), F32),
        ),
        scratch_shapes=[pltpu.VMEM((RWKV_HEADS, HEAD_DIM, HEAD_DIM), F32)],
        compiler_params=_params("parallel", "arbitrary"),
        name="rwkv_prompt",
    )(p_rw, p_rw, prev0, mu, w0, a0, k_k, k_a, r_k, lnx_g, lnx_b, wlb, alb, h0, tri, ones)


def _rwkv_sample_prep_kernel(p_ref, prev_ref, mu_ref, w0_ref, a0_ref, kk_ref, ka_ref, rk_ref,
                             wlb_ref, alb_ref, ones_ref,
                             r_ref, w_ref, k2_ref, v_ref, kkn_ref, b_ref, bonus_ref):
    pf = p_ref[...]
    xs = pf + (prev_ref[...] - pf) * mu_ref[...]
    w3 = 3 * RWKV_W
    r, k, v = xs[:, :RWKV_W], xs[:, RWKV_W:2 * RWKV_W], xs[:, 2 * RWKV_W:w3]
    log_w, a, kk, k2, bonus = _rwkv_token_math(
        r, k, v, xs[:, w3:w3 + LORA], xs[:, w3 + LORA:], w0_ref[...], a0_ref[...], kk_ref[...],
        ka_ref[...], rk_ref[...], wlb_ref[...], alb_ref[...], ones_ref[...])
    r_ref[...] = r
    w_ref[...] = jnp.exp(log_w)
    k2_ref[...] = k2
    v_ref[...] = v
    kkn_ref[...] = kk
    b_ref[...] = kk * a
    bonus_ref[...] = bonus


def _rwkv_sample_prep(p, prev, vecs, wlb, alb, ones):
    n = p.shape[0]
    mu, w0, a0, k_k, k_a, r_k = vecs
    return pl.pallas_call(
        _rwkv_sample_prep_kernel,
        out_shape=(jax.ShapeDtypeStruct((n, RWKV_W), F32),) * 7,
        name="rwkv_sample_prep",
    )(p, prev, mu, w0, a0, k_k, k_a, r_k, wlb, alb, ones)


def _rwkv_sample_step_kernel(s_ref, r_ref, w_ref, k_ref, kk_ref, b_ref,
                             vcol_ref, bonus_ref, g_ref, beta_ref, o_ref, snew_ref):
    s = s_ref[...]
    s_kk = jnp.sum(s * kk_ref[...], axis=2, keepdims=True)
    s = s * w_ref[...] - s_kk * b_ref[...] + vcol_ref[...] * k_ref[...]
    snew_ref[...] = s
    o = jnp.sum(s * r_ref[...], axis=2, keepdims=True)
    mean = jnp.mean(o, axis=1, keepdims=True)
    d = o - mean
    var = jnp.mean(d * d, axis=1, keepdims=True)
    o_ref[...] = d * lax.rsqrt(var + GN_EPS) * g_ref[...] + beta_ref[...] + bonus_ref[...]


def _rwkv_sample_step(s0, rows, cols, g_col, beta_col):
    n = s0.shape[0]
    h, d = RWKV_HEADS, HEAD_DIM
    st = pl.BlockSpec((None, h, d, d), lambda bi: (bi, 0, 0, 0))
    rowspec = pl.BlockSpec((None, h, 1, d), lambda bi: (bi, 0, 0, 0))
    colspec = pl.BlockSpec((None, h, d, 1), lambda bi: (bi, 0, 0, 0))
    cconst = pl.BlockSpec((h, d, 1), lambda bi: (0, 0, 0))
    return pl.pallas_call(
        _rwkv_sample_step_kernel,
        grid=(n,),
        in_specs=[st] + [rowspec] * 5 + [colspec] * 2 + [cconst] * 2,
        out_specs=(colspec, st),
        out_shape=(jax.ShapeDtypeStruct((n, h, d, 1), F32),
                   jax.ShapeDtypeStruct((n, h, d, d), F32)),
        compiler_params=_params("parallel"),
        name="rwkv_sample_step",
    )(s0, *rows, *cols, g_col, beta_col)


def _sb_prompt_kernel(bias_ref, q_ref, kT_ref, vT_ref, cs_ref, o_ref):
    pair = pl.program_id(1)
    qi = pl.program_id(2)
    blk = SB_BLOCK
    cs_mat = cs_ref[...]
    q_all = q_ref[...] * ATTN_SCALE
    valid = (lax.broadcasted_iota(jnp.int32, (blk, blk), 1)
             < lax.broadcasted_iota(jnp.int32, (blk, blk), 0))

    heads = range(2)
    rows = [slice(hh * HEAD_DIM, (hh + 1) * HEAD_DIM) for hh in heads]
    biases = [bias_ref[2 * pair + hh] for hh in heads]
    chains = [(hh, u) for u in range(SB_Q_SUB) for hh in heads]
    qs = {(hh, u): q_all[u * blk:(u + 1) * blk, rows[hh]].astype(BF16) for hh, u in chains}

    def block(j, state, active, masked):
        start = pl.multiple_of(j * blk, blk)
        k16 = [kT_ref[rows[hh], pl.ds(start, blk)].astype(BF16) for hh in heads]
        v16 = [vT_ref[rows[hh], pl.ds(start, blk)].astype(BF16) for hh in heads]
        carries, accs = dict(state[0]), dict(state[1])
        tmp = {}

        def stage(s, ch):
            hh = ch[0]
            if s == 0:
                tmp[ch] = {"z": _dot(qs[ch], k16[hh]) + biases[hh]}
            elif s == 1:
                t = tmp[ch]
                sp, t["zs"] = _softplus_pair(t.pop("z"))
                if ch in masked:
                    sp = jnp.where(valid, sp, 0.0)
                t["first"] = sp[:, 0:1]
                t["parts"] = jnp.concatenate(_split2(sp), axis=1)
            elif s == 2:
                t = tmp[ch]
                t["cs"] = _dot(t.pop("parts"), cs_mat)
            elif s == 3:
                t = tmp[ch]
                cs = t.pop("cs")
                a = jnp.exp(t.pop("zs") - cs - carries[ch])
                if ch in masked:
                    a = jnp.where(valid, a, 0.0)
                t["a"] = a.astype(BF16)
                carries[ch] = carries[ch] + (cs[:, 0:1] + t.pop("first"))
            else:
                accs[ch] = accs[ch] + _dot(tmp.pop(ch)["a"], v16[hh], NT)

        n_stage = 5
        for wave in range(len(active) + n_stage - 1):
            for s in reversed(range(n_stage)):
                idx = wave - s
                if 0 <= idx < len(active):
                    stage(s, active[idx])
        return (tuple(carries[ch] for ch in chains), tuple(accs[ch] for ch in chains))

    def as_dicts(state):
        return dict(zip(chains, state[0])), dict(zip(chains, state[1]))

    zeros_c = jnp.zeros((blk, 1), F32)
    zeros_o = jnp.zeros((blk, HEAD_DIM), F32)
    state = ((zeros_c,) * len(chains), (zeros_o,) * len(chains))
    top = SB_Q_SUB * qi
    for d in reversed(range(SB_Q_SUB)):
        active = [ch for ch in chains if ch[1] >= d]
        masked = {ch for ch in chains if ch[1] == d}
        state = block(top + d, as_dicts(state), active, masked)

    def body(jj, st):
        return block(top - 1 - jj, as_dicts(st), chains, set())

    state = lax.fori_loop(0, top, body, state)
    accs = dict(zip(chains, state[1]))
    o_ref[...] = jnp.concatenate(
        [jnp.concatenate([accs[(hh, u)] for hh in heads], axis=1) for u in range(SB_Q_SUB)], axis=0)


def _sb_prompt(q, kT, vT, bias, cs_mat):
    b, t, _ = q.shape
    blk = SB_BLOCK * SB_Q_SUB
    npair = SB_HEADS // 2
    kv = pl.BlockSpec((None, PAIR_W, t), lambda bi, p, i: (bi, p, 0))
    qo = pl.BlockSpec((None, blk, PAIR_W), lambda bi, p, i: (bi, i, p))
    return pl.pallas_call(
        _sb_prompt_kernel,
        grid=(b, npair, t // blk),
        in_specs=[
            pl.BlockSpec(memory_space=pltpu.SMEM),
            qo, kv, kv,
            pl.BlockSpec((2 * SB_BLOCK, SB_BLOCK), lambda bi, p, i: (0, 0)),
        ],
        out_specs=qo,
        out_shape=jax.ShapeDtypeStruct((b, t, SB_W), F32),
        compiler_params=_params("parallel", "parallel", "arbitrary"),
        name="sb_prompt",
    )(bias, q, kT, vT, cs_mat)


def _suffix_sum_lanes(x):
    lane = lax.broadcasted_iota(jnp.int32, x.shape, 1)
    d = 1
    while d < LANES:
        shifted = pltpu.roll(x, LANES - d, axis=1)
        x = x + jnp.where(lane < LANES - d, shifted, 0.0)
        d *= 2
    return x


def _sb_sample_kernel(pt_ref, *refs):
    npg = PAGES_PER_STEP
    k_refs = refs[:npg]
    v_refs = refs[npg:2 * npg]
    q_ref, bias_ref, o_ref, acc_ref, carry_ref = refs[2 * npg:]
    j = pl.program_id(1)

    @pl.when(j == 0)
    def _():
        acc_ref[...] = jnp.zeros_like(acc_ref)
        carry_ref[...] = jnp.zeros_like(carry_ref)

    row = lax.broadcasted_iota(jnp.int32, (npg, LANES), 0)
    for h in range(SB_HEADS):
        qb = q_ref[h]
        z = jnp.concatenate(
            [jnp.sum(k_refs[u][h] * qb, axis=0, keepdims=True) for u in range(npg)], axis=0)
        z = z + bias_ref[h]
        sp, _ = _softplus_pair(z)
        suffix = _suffix_sum_lanes(sp)
        tot = jnp.broadcast_to(suffix[:, 0:1], (npg, LANES))
        later = tot
        d = 1
        while d < npg:
            later = later + jnp.where(row >= d, pltpu.roll(later, d, axis=0), 0.0)
            d *= 2
        carry = carry_ref[h]
        a = jnp.exp(z - suffix - (later - tot) - carry)
        acc = acc_ref[h]
        for u in range(npg):
            acc = acc + v_refs[u][h] * a[u:u + 1, :]
        acc_ref[h] = acc
        carry_ref[h] = carry + later[npg - 1:npg, :]

    @pl.when(j == pl.num_programs(1) - 1)
    def _():
        o_ref[...] = jnp.sum(acc_ref[...], axis=2)


def _sb_sample(page_table, cache_k, cache_v, q_bcast, bias_bcast):
    n, n_pages = page_table.shape
    npg = PAGES_PER_STEP
    h, d, pg = cache_k.shape[1:]
    last = n_pages - 1

    def page_spec(u):
        return pl.BlockSpec(
            (None, h, d, pg),
            lambda bi, j, pt: (pt[bi, last - (j * npg + u)], 0, 0, 0))

    grid_spec = pltpu.PrefetchScalarGridSpec(
        num_scalar_prefetch=1,
        grid=(n, n_pages // npg),
        in_specs=[page_spec(u) for u in range(npg)] * 2 + [
            pl.BlockSpec((None, h, d, pg), lambda bi, j, pt: (bi, 0, 0, 0)),
            pl.BlockSpec((h, 1, pg), lambda bi, j, pt: (0, 0, 0)),
        ],
        out_specs=pl.BlockSpec((None, h, d), lambda bi, j, pt: (bi, 0, 0)),
        scratch_shapes=[pltpu.VMEM((h, d, pg), F32), pltpu.VMEM((h, 1, pg), F32)],
    )
    return pl.pallas_call(
        _sb_sample_kernel,
        grid_spec=grid_spec,
        out_shape=jax.ShapeDtypeStruct((n, h, d), F32),
        compiler_params=_params("parallel", "arbitrary"),
        name="sb_sample",
    )(page_table, *([cache_k] * npg), *([cache_v] * npg), q_bcast, bias_bcast)


def _head_ones(width):
    hid = jnp.arange(width) // HEAD_DIM
    return (hid[:, None] == hid[None, :]).astype(BF16)


def _rows_for(t, target):
    return target if t % target == 0 else t


def kernel(x_prompt, mem_prompt, x_sample, cache_sb_k, cache_sb_v, page_table, state_wkv, state_shift, cache_mem_k, cache_mem_v, norm_g, w_in, sb_bias, mu_shift, w0, w_lora_b, a0, a_lora_b, k_k, k_a, r_k, lnx_g, lnx_b, mem_norm_g, w_mem_k, w_mem_v, w_out, final_norm_g):
    depth = w_in.shape[0]
    assert depth == 1, "single-layer trunk"
    l = 0
    b_p, t_p, d = x_prompt.shape
    b_s = x_sample.shape[0]
    assert x_sample.shape[1] == 1
    n_mem = mem_prompt.shape[1]
    page = cache_sb_k.shape[2]

    kv0 = RWKV_COLS + SB_W
    kv1 = kv0 + 2 * SB_W
    w = w_in[l]
    wa = w[:, :kv0].astype(BF16)
    wb = w[:, kv1:].astype(BF16)
    wkvT = w[:, kv0:kv1].T.astype(BF16)
    wkT = w_mem_k[l].T.astype(BF16)
    wvT = w_mem_v[l].T.astype(BF16)
    wo = w_out[l].astype(BF16)
    fg = final_norm_g.reshape(1, d)
    r_k_flat = r_k[l].reshape(1, RWKV_W)
    ones_all = _head_ones(RWKV_W)
    c = RWKV_CHUNK
    tri = (jnp.arange(c)[None, :] <= jnp.arange(c)[:, None]).astype(BF16)
    blk = SB_BLOCK
    ar = jnp.arange(blk)
    suffix_mat = (ar[:, None] > ar[None, :]).astype(BF16)
    cs_mat = jnp.concatenate([suffix_mat, suffix_mat], axis=0)

    mkT, mvT = _mem_kv(mem_prompt, mem_norm_g[l:l + 1], wkT, wvT)
    p_rw, q_sb, xq, gate, kT, vT = _in_proj(
        x_prompt, norm_g[l:l + 1], wa, wb, wkvT, _rows_for(t_p, 256))
    vecs = (mu_shift[l:l + 1], w0[l:l + 1], a0[l:l + 1], k_k[l:l + 1], k_a[l:l + 1],
            r_k_flat, lnx_g[l:l + 1], lnx_b[l:l + 1])
    o_rw, h_fin = _rwkv_prompt(
        p_rw, jnp.zeros((b_p, 1, RWKV_COLS), F32),
        jnp.zeros((b_p, RWKV_HEADS, HEAD_DIM, HEAD_DIM), F32),
        vecs, w_lora_b[l], a_lora_b[l], tri, ones_all)
    o_sb = _sb_prompt(q_sb, kT, vT, sb_bias[l], cs_mat)
    o_x = _xattn(xq, mkT, mvT, _rows_for(t_p, 512))
    y_prompt = _out_proj(o_rw, o_sb, o_x, gate, x_prompt, wo, fg, _rows_for(t_p, 512))

    def tokens_major(xT, heads):
        return xT.reshape(1, xT.shape[0], heads, HEAD_DIM, xT.shape[2]).transpose(0, 1, 4, 2, 3)

    new_sb_k_p = tokens_major(kT, SB_HEADS)
    new_sb_v_p = tokens_major(vT, SB_HEADS)
    new_wkv_p = jnp.swapaxes(h_fin, -1, -2)[None]
    new_shift_p = p_rw[:, -1][None]
    new_mem_k_p = tokens_major(mkT, X_HEADS)
    new_mem_v_p = tokens_major(mvT, X_HEADS)

    xs2 = x_sample.reshape(1, b_s, d)
    p_rw_s, q_s, xq_s, gate_s, kT_s, vT_s = _in_proj(xs2, norm_g[l:l + 1], wa, wb, wkvT, b_s)
    p_rw_s = p_rw_s[0]
    svecs = (mu_shift[l:l + 1], w0[l:l + 1], a0[l:l + 1], k_k[l:l + 1], k_a[l:l + 1], r_k_flat)
    r_s, w_s, k2_s, v_s, kk_s, bv_s, bonus_s = _rwkv_sample_prep(
        p_rw_s, state_shift[l], svecs, w_lora_b[l], a_lora_b[l], ones_all)
    as_row = lambda a_: a_.reshape(b_s, RWKV_HEADS, 1, HEAD_DIM)
    as_col = lambda a_: a_.reshape(b_s, RWKV_HEADS, HEAD_DIM, 1)
    o_rw_s, new_wkv_s = _rwkv_sample_step(
        state_wkv[l],
        [as_row(r_s), as_row(w_s), as_row(k2_s), as_row(kk_s), as_row(bv_s)],
        [as_col(v_s), as_col(bonus_s)],
        lnx_g[l].reshape(RWKV_HEADS, HEAD_DIM, 1), lnx_b[l].reshape(RWKV_HEADS, HEAD_DIM, 1))
    o_rw_s = o_rw_s.reshape(1, b_s, RWKV_W)

    ck = cache_sb_k[l].transpose(0, 2, 3, 1)
    cv = cache_sb_v[l].transpose(0, 2, 3, 1)
    q_b = jnp.broadcast_to(
        (q_s[0] * ATTN_SCALE).reshape(b_s, SB_HEADS, HEAD_DIM, 1), (b_s, SB_HEADS, HEAD_DIM, page))
    bias_b = jnp.broadcast_to(sb_bias[l].reshape(SB_HEADS, 1, 1), (SB_HEADS, 1, page))
    o_sb_s = _sb_sample(page_table, ck, cv, q_b, bias_b).reshape(1, b_s, SB_W)

    mk_s = cache_mem_k[l].transpose(0, 2, 3, 1).reshape(b_s, X_W, n_mem)
    mv_s = cache_mem_v[l].transpose(0, 2, 3, 1).reshape(b_s, X_W, n_mem)
    xq_rows = jnp.broadcast_to(xq_s[0][:, None, :], (b_s, 8, X_W))
    o_x_s = _xattn(xq_rows, mk_s, mv_s, 8)[:, 0][None]
    y_sample = _out_proj(o_rw_s, o_sb_s, o_x_s, gate_s, xs2, wo, fg, b_s).reshape(b_s, 1, d)

    new_sb_k_s = kT_s[0].T.reshape(1, b_s, 1, SB_HEADS, HEAD_DIM)
    new_sb_v_s = vT_s[0].T.reshape(1, b_s, 1, SB_HEADS, HEAD_DIM)
    new_shift_s = p_rw_s[None]

    return (y_prompt, y_sample, new_sb_k_p, new_sb_v_p, new_wkv_p, new_shift_p,
            new_mem_k_p, new_mem_v_p, new_sb_k_s, new_sb_v_s, new_wkv_s[None], new_shift_s)
```

```python
import functools
import math

import jax
import jax.numpy as jnp
from jax import lax
from jax.experimental import pallas as pl
from jax.experimental.pallas import tpu as pltpu

F32 = jnp.float32
BF16 = jnp.bfloat16

HEAD_DIM = 64
RWKV_HEADS = 6
SB_HEADS = 6
X_HEADS = 4
RWKV_W = RWKV_HEADS * HEAD_DIM
SB_W = SB_HEADS * HEAD_DIM
X_W = X_HEADS * HEAD_DIM
LORA = 64
RWKV_COLS = 3 * RWKV_W + 2 * LORA
NORM_EPS = 1e-6
GN_EPS = 64e-5
DECAY_SCALE = math.exp(-0.5)
ATTN_SCALE = HEAD_DIM ** -0.5
LOG2E = math.log2(math.e)

LANES = 128
PAIR_W = 2 * HEAD_DIM
RWKV_CHUNK = 128
RWKV_SUB = 2
INV_BASE = 8
SB_BLOCK = 256
SB_Q_SUB = 2
PAGES_PER_STEP = 16
SB_SAMPLE_ROWS = 128
VMEM_LIMIT = 48 * 1024 * 1024

NN = (((1,), (0,)), ((), ()))
NT = (((1,), (1,)), ((), ()))
TN = (((0,), (0,)), ((), ()))


def _dot(a, b, dims=NN):
    return lax.dot_general(a, b, dims, preferred_element_type=F32)


def _split2(x):
    hi = x.astype(BF16)
    lo = (x - hi.astype(F32)).astype(BF16)
    return hi, lo


def _split3(x):
    hi = x.astype(BF16)
    r1 = x - hi.astype(F32)
    mid = r1.astype(BF16)
    lo = (r1 - mid.astype(F32)).astype(BF16)
    return hi, mid, lo


def _dot_f32(a, b, dims=NN):
    (ka,), (kb,) = dims[0]
    ah, al = _split2(a)
    bh, bl = _split2(b)
    return _dot(jnp.concatenate([ah, ah, al], axis=ka), jnp.concatenate([bh, bl, bh], axis=kb), dims)


def _dot_bf16(a, b, dims=NN):
    return _dot(a.astype(BF16), b.astype(BF16), dims)


def _dot_exact_rhs(a, b_bf16, dims=NN):
    hi, mid, lo = _split3(a)
    return _dot(hi, b_bf16, dims) + (_dot(mid, b_bf16, dims) + _dot(lo, b_bf16, dims))


def _dot_exact_lhs(a_bf16, b, dims=NN):
    hi, mid, lo = _split3(b)
    return _dot(a_bf16, hi, dims) + (_dot(a_bf16, mid, dims) + _dot(a_bf16, lo, dims))


def _sigmoid(x):
    return 1.0 / (1.0 + jnp.exp(-x))


def _softplus_pair(z):
    sp = jnp.maximum(z, 0.0) + jnp.log(1.0 + jnp.exp2(jnp.abs(z) * -LOG2E))
    return sp, z - sp


def _rmsnorm_rows(x, g):
    ms = jnp.mean(x * x, axis=-1, keepdims=True)
    return x * lax.rsqrt(ms + NORM_EPS) * g


def _params(*sem):
    return pltpu.CompilerParams(dimension_semantics=sem, vmem_limit_bytes=VMEM_LIMIT)


N_A = RWKV_COLS + SB_W
N_B = X_W + RWKV_W + SB_W + X_W


def _in_proj_kernel(x_ref, g_ref, wa_ref, wb_ref, wkvT_ref,
                    prw_ref, q_ref, xq_ref, gate_ref, kT_ref, vT_ref):
    h = _rmsnorm_rows(x_ref[...], g_ref[...]).astype(BF16)
    pa = _dot(h, wa_ref[...])
    prw_ref[...] = pa[:, :RWKV_COLS]
    q_ref[...] = pa[:, RWKV_COLS:]
    pb = _dot(h, wb_ref[...])
    xq_ref[...] = pb[:, :X_W]
    gate_ref[...] = pb[:, X_W:]
    kv = _dot(wkvT_ref[...], h, NT)
    kT_ref[...] = kv[:SB_W]
    vT_ref[...] = kv[SB_W:]


def _in_proj(x, norm_g, wa, wb, wkvT, rows):
    b, t, d = x.shape
    n_gate = N_B - X_W
    row = lambda bi, i: (bi, i, 0)
    col = lambda bi, i: (bi, 0, i)
    const = lambda bi, i: (0, 0)
    out_shape = (
        jax.ShapeDtypeStruct((b, t, RWKV_COLS), F32),
        jax.ShapeDtypeStruct((b, t, SB_W), F32),
        jax.ShapeDtypeStruct((b, t, X_W), F32),
        jax.ShapeDtypeStruct((b, t, n_gate), F32),
        jax.ShapeDtypeStruct((b, SB_W, t), F32),
        jax.ShapeDtypeStruct((b, SB_W, t), F32),
    )
    return pl.pallas_call(
        _in_proj_kernel,
        grid=(b, t // rows),
        in_specs=[
            pl.BlockSpec((None, rows, d), row),
            pl.BlockSpec((1, d), const),
            pl.BlockSpec((d, N_A), const),
            pl.BlockSpec((d, N_B), const),
            pl.BlockSpec((2 * SB_W, d), const),
        ],
        out_specs=(
            pl.BlockSpec((None, rows, RWKV_COLS), row),
            pl.BlockSpec((None, rows, SB_W), row),
            pl.BlockSpec((None, rows, X_W), row),
            pl.BlockSpec((None, rows, n_gate), row),
            pl.BlockSpec((None, SB_W, rows), col),
            pl.BlockSpec((None, SB_W, rows), col),
        ),
        out_shape=out_shape,
        compiler_params=_params("parallel", "parallel"),
        name="in_proj",
    )(x, norm_g, wa, wb, wkvT)


def _mem_kv_kernel(mem_ref, g_ref, wkT_ref, wvT_ref, kT_ref, vT_ref):
    h = _rmsnorm_rows(mem_ref[...], g_ref[...]).astype(BF16)
    kT_ref[...] = _dot(wkT_ref[...], h, NT)
    vT_ref[...] = _dot(wvT_ref[...], h, NT)


def _mem_kv(mem, g, wkT, wvT):
    b, m, d = mem.shape
    const = lambda bi: (0, 0)
    blk = lambda bi: (bi, 0, 0)
    return pl.pallas_call(
        _mem_kv_kernel,
        grid=(b,),
        in_specs=[
            pl.BlockSpec((None, m, d), blk),
            pl.BlockSpec((1, d), const),
            pl.BlockSpec((X_W, d), const),
            pl.BlockSpec((X_W, d), const),
        ],
        out_specs=(pl.BlockSpec((None, X_W, m), blk), pl.BlockSpec((None, X_W, m), blk)),
        out_shape=(jax.ShapeDtypeStruct((b, X_W, m), F32),) * 2,
        compiler_params=_params("parallel"),
        name="mem_kv",
    )(mem, g, wkT, wvT)


def _xattn_kernel(xq_ref, kT_ref, vT_ref, o_ref):
    for h in range(X_HEADS):
        sl = slice(h * HEAD_DIM, (h + 1) * HEAD_DIM)
        q = (xq_ref[:, sl] * ATTN_SCALE).astype(BF16)
        s = _dot(q, kT_ref[sl, :].astype(BF16))
        e = jnp.exp(s - jnp.max(s, axis=-1, keepdims=True))
        l = jnp.sum(e, axis=-1, keepdims=True)
        o_ref[:, sl] = _dot(e.astype(BF16), vT_ref[sl, :].astype(BF16), NT) / l


def _xattn(xq, kT, vT, rows):
    b, t, _ = xq.shape
    m = kT.shape[-1]
    row = lambda bi, i: (bi, i, 0)
    full = lambda bi, i: (bi, 0, 0)
    return pl.pallas_call(
        _xattn_kernel,
        grid=(b, t // rows),
        in_specs=[
            pl.BlockSpec((None, rows, X_W), row),
            pl.BlockSpec((None, X_W, m), full),
            pl.BlockSpec((None, X_W, m), full),
        ],
        out_specs=pl.BlockSpec((None, rows, X_W), row),
        out_shape=jax.ShapeDtypeStruct((b, t, X_W), F32),
        compiler_params=_params("parallel", "parallel"),
        name="xattn",
    )(xq, kT, vT)


def _out_proj_kernel(orw_ref, osb_ref, ox_ref, gate_ref, x_ref, w_ref, fg_ref, y_ref):
    g = gate_ref[...]
    sg = g * _sigmoid(g)
    a0, a1 = RWKV_W, RWKV_W + SB_W
    acc = _dot((orw_ref[...] * sg[:, :a0]).astype(BF16), w_ref[:a0, :])
    acc += _dot((osb_ref[...] * sg[:, a0:a1]).astype(BF16), w_ref[a0:a1, :])
    acc += _dot((ox_ref[...] * sg[:, a1:]).astype(BF16), w_ref[a1:, :])
    y_ref[...] = _rmsnorm_rows(x_ref[...] + acc, fg_ref[...])


def _out_proj(o_rw, o_sb, o_x, gate, x, w_out, fg, rows):
    b, t, d = x.shape
    dm = w_out.shape[0]
    row = lambda bi, i: (bi, i, 0)
    const = lambda bi, i: (0, 0)
    return pl.pallas_call(
        _out_proj_kernel,
        grid=(b, t // rows),
        in_specs=[
            pl.BlockSpec((None, rows, RWKV_W), row),
            pl.BlockSpec((None, rows, SB_W), row),
            pl.BlockSpec((None, rows, X_W), row),
            pl.BlockSpec((None, rows, dm), row),
            pl.BlockSpec((None, rows, d), row),
            pl.BlockSpec((dm, d), const),
            pl.BlockSpec((1, d), const),
        ],
        out_specs=pl.BlockSpec((None, rows, d), row),
        out_shape=jax.ShapeDtypeStruct((b, t, d), F32),
        compiler_params=_params("parallel", "parallel"),
        name="out_proj",
    )(o_rw, o_sb, o_x, gate, x, w_out, fg)


def _rwkv_token_math(r, k, v, xw, xa, w0, a0, k_k, k_a, r_k, wlb, alb, head_ones):
    log_w = -DECAY_SCALE * _sigmoid(w0 + _dot(jnp.tanh(xw).astype(BF16), wlb.astype(BF16)))
    a = _sigmoid(a0 + _dot(xa.astype(BF16), alb.astype(BF16)))
    kk = k * k_k
    ss = _dot_exact_rhs(kk * kk, head_ones)
    kk = kk * lax.rsqrt(jnp.maximum(ss, 1e-12))
    k2 = k * (1.0 + (a - 1.0) * k_a)
    bonus = _dot_exact_rhs(r * k2 * r_k, head_ones) * v
    return log_w, a, kk, k2, bonus


def _group_norm(o, g, b, head_ones):
    inv = 1.0 / HEAD_DIM
    mean = _dot_exact_rhs(o, head_ones) * inv
    d = o - mean
    var = _dot_exact_rhs(d * d, head_ones) * inv
    return d * lax.rsqrt(var + GN_EPS) * g + b


def _shift_mix(cur, last_row, mu, first_row_mask):
    prev = jnp.where(first_row_mask, last_row, pltpu.roll(cur, 1, axis=0))
    return cur + (prev - cur) * mu


def _rwkv_prompt_kernel(
        p_ref, prev_ref, first_ref, mu_ref,
        w0_ref, a0_ref, kk_ref, ka_ref, rk_ref, lg_ref, lb_ref, wlb_ref, alb_ref,
        h0_ref, tri_ref, ones_ref,
        o_ref, hfin_ref, h_scr):
    i = pl.program_id(1)
    c = RWKV_CHUNK
    rows = c * RWKV_SUB
    w3 = 3 * RWKV_W

    @pl.when(i == 0)
    def _():
        h_scr[...] = h0_ref[...]

    first = lax.broadcasted_iota(jnp.int32, (rows, RWKV_COLS), 0) == 0
    last = jnp.where(i == 0, first_ref[...], prev_ref[7:8, :])
    xs = _shift_mix(p_ref[...], last, mu_ref[...], first)
    r, k, v = xs[:, :RWKV_W], xs[:, RWKV_W:2 * RWKV_W], xs[:, 2 * RWKV_W:w3]
    head_ones = ones_ref[...]
    log_w, a, kk, k2, bonus = _rwkv_token_math(
        r, k, v, xs[:, w3:w3 + LORA], xs[:, w3 + LORA:], w0_ref[...], a0_ref[...], kk_ref[...],
        ka_ref[...], rk_ref[...], wlb_ref[...], alb_ref[...], head_ones)

    subs = range(RWKV_SUB)
    lam = _dot_exact_lhs(tri_ref[...], log_w)
    lam_ends = [lam[(sc + 1) * c - 1:(sc + 1) * c, :] for sc in subs]
    lam_c = jnp.concatenate([jnp.broadcast_to(le, (c, RWKV_W)) for le in lam_ends], axis=0)
    w_in = jnp.exp(lam)
    w_ex = jnp.exp(lam - log_w)
    w_inv = jnp.exp(-lam)
    w_end = jnp.exp(lam_c - lam)
    w_c = [jnp.exp(le) for le in lam_ends]
    bvec = kk * a
    at = -kk * w_ex
    bt = bvec * w_inv
    kt = k2 * w_inv
    rt = r * w_in
    bh = bvec * w_end
    kh = k2 * w_end

    colid = lax.broadcasted_iota(jnp.int32, (c, c), 1)
    rowid = lax.broadcasted_iota(jnp.int32, (c, c), 0)
    strict = colid < rowid
    incl = colid <= rowid
    eye64 = (lax.broadcasted_iota(jnp.int32, (HEAD_DIM, HEAD_DIM), 0)
             == lax.broadcasted_iota(jnp.int32, (HEAD_DIM, HEAD_DIM), 1))

    heads = range(RWKV_HEADS)
    chains = [(sc, hh) for sc in subs for hh in heads]
    cut = lambda x: {(sc, hh): x[sc * c:(sc + 1) * c, hh * HEAD_DIM:(hh + 1) * HEAD_DIM]
                     for sc, hh in chains}
    at_h, bt_h, kt_h, rt_h, v_h, bh_h, kh_h = map(cut, (at, bt, kt, rt, v, bh, kh))
    each = lambda fn: {ch: fn(ch) for ch in chains}
    scores = each(lambda ch: _dot_f32(
        jnp.concatenate([at_h[ch], rt_h[ch]], axis=0),
        jnp.concatenate([bt_h[ch], kt_h[ch]], axis=0), NT))
    n = each(lambda ch: jnp.where(strict, scores[ch][:c, :c], 0.0))
    aak = each(lambda ch: jnp.where(strict, scores[ch][:c, c:], 0.0))
    mrb = each(lambda ch: jnp.where(incl, scores[ch][c:, :c], 0.0))
    mrk = each(lambda ch: jnp.where(incl, scores[ch][c:, c:], 0.0))
    rb, cb = rowid // INV_BASE, colid // INV_BASE
    dg = each(lambda ch: jnp.where(rb == cb, n[ch], 0.0))
    tm = each(lambda ch: jnp.where(colid == rowid, 1.0, dg[ch]))
    span = 1
    while 2 * span < INV_BASE:
        dg = each(lambda ch: _dot_bf16(dg[ch], dg[ch]))
        tm = each(lambda ch: tm[ch] + _dot_bf16(tm[ch], dg[ch]))
        span *= 2
    m = INV_BASE
    while m < c:
        off = (rowid // (2 * m) == colid // (2 * m)) & (rowid // m != colid // m)
        tn = each(lambda ch: _dot_bf16(tm[ch], jnp.where(off, n[ch], 0.0)))
        tm = each(lambda ch: tm[ch] + _dot_bf16(tn[ch], tm[ch]))
        m *= 2
    akv = each(lambda ch: _dot_f32(aak[ch], v_h[ch]))
    mkv = each(lambda ch: _dot_f32(mrk[ch], v_h[ch]))
    khv = each(lambda ch: _dot_f32(kh_h[ch], v_h[ch], TN))
    pq = each(lambda ch: _dot_f32(tm[ch], jnp.concatenate([at_h[ch], akv[ch]], axis=1)))
    mpq = each(lambda ch: _dot_f32(mrb[ch], pq[ch]))
    gj = each(lambda ch: _dot_f32(bh_h[ch], pq[ch], TN))
    p2 = each(lambda ch: rt_h[ch] + mpq[ch][:, :HEAD_DIM])
    q2 = each(lambda ch: mpq[ch][:, HEAD_DIM:] + mkv[ch])
    g = each(lambda ch: gj[ch][:, :HEAD_DIM] + jnp.where(
        eye64, w_c[ch[0]][:, ch[1] * HEAD_DIM:(ch[1] + 1) * HEAD_DIM], 0.0))
    jm = each(lambda ch: gj[ch][:, HEAD_DIM:] + khv[ch])

    state = [h_scr[hh] for hh in heads]
    outs = []
    for sc in subs:
        outs.append(jnp.concatenate(
            [_dot_f32(p2[(sc, hh)], state[hh]) + q2[(sc, hh)] for hh in heads], axis=1))
        state = [_dot_f32(g[(sc, hh)], state[hh]) + jm[(sc, hh)] for hh in heads]
    for hh in heads:
        h_scr[hh] = state[hh]

    o = jnp.concatenate(outs, axis=0)
    o_ref[...] = _group_norm(o, lg_ref[...], lb_ref[...], head_ones) + bonus

    @pl.when(i == pl.num_programs(1) - 1)
    def _():
        hfin_ref[...] = h_scr[...]


def _rwkv_prompt(p_rw, prev0, h0, vecs, wlb, alb, tri, ones):
    b, t, _ = p_rw.shape
    c = RWKV_CHUNK * RWKV_SUB
    sub = 8
    mu, w0, a0, k_k, k_a, r_k, lnx_g, lnx_b = vecs
    const = lambda bi, i: (0, 0)
    hvec = pl.BlockSpec((1, RWKV_W), const)
    lora = pl.BlockSpec((LORA, RWKV_W), const)
    state = pl.BlockSpec((None, RWKV_HEADS, HEAD_DIM, HEAD_DIM), lambda bi, i: (bi, 0, 0, 0))

    return pl.pallas_call(
        _rwkv_prompt_kernel,
        grid=(b, t // c),
        in_specs=[
            pl.BlockSpec((None, c, RWKV_COLS), lambda bi, i: (bi, i, 0)),
            pl.BlockSpec((None, sub, RWKV_COLS),
                         lambda bi, i: (bi, jnp.maximum(i * (c // sub) - 1, 0), 0)),
            pl.BlockSpec((None, 1, RWKV_COLS), lambda bi, i: (bi, 0, 0)),
            pl.BlockSpec((1, RWKV_COLS), const),
            hvec, hvec, hvec, hvec, hvec, hvec, hvec, lora, lora,
            state,
            pl.BlockSpec((c, c), const),
            pl.BlockSpec((RWKV_W, RWKV_W), const),
        ],
        out_specs=(
            pl.BlockSpec((None, c, RWKV_W), lambda bi, i: (bi, i, 0)),
            state,
        ),
        out_shape=(
            jax.ShapeDtypeStruct((b, t, RWKV_W), F32),
            jax.ShapeDtypeStruct((b, RWKV_HEADS, HEAD_DIM, HEAD_DIM), F32),
        ),
        scratch_shapes=[pltpu.VMEM((RWKV_HEADS, HEAD_DIM, HEAD_DIM), F32)],
        compiler_params=_params("parallel", "arbitrary"),
        name="rwkv_prompt",
    )(p_rw, p_rw, prev0, mu, w0, a0, k_k, k_a, r_k, lnx_g, lnx_b, wlb, alb, h0, tri, ones)


def _rwkv_sample_prep_kernel(p_ref, prev_ref, mu_ref, w0_ref, a0_ref, kk_ref, ka_ref, rk_ref,
                             wlb_ref, alb_ref, ones_ref,
                             r_ref, w_ref, k2_ref, v_ref, kkn_ref, b_ref, bonus_ref):
    pf = p_ref[...]
    xs = pf + (prev_ref[...] - pf) * mu_ref[...]
    w3 = 3 * RWKV_W
    r, k, v = xs[:, :RWKV_W], xs[:, RWKV_W:2 * RWKV_W], xs[:, 2 * RWKV_W:w3]
    log_w, a, kk, k2, bonus = _rwkv_token_math(
        r, k, v, xs[:, w3:w3 + LORA], xs[:, w3 + LORA:], w0_ref[...], a0_ref[...], kk_ref[...],
        ka_ref[...], rk_ref[...], wlb_ref[...], alb_ref[...], ones_ref[...])
    r_ref[...] = r
    w_ref[...] = jnp.exp(log_w)
    k2_ref[...] = k2
    v_ref[...] = v
    kkn_ref[...] = kk
    b_ref[...] = kk * a
    bonus_ref[...] = bonus


def _rwkv_sample_prep(p, prev, vecs, wlb, alb, ones):
    n = p.shape[0]
    mu, w0, a0, k_k, k_a, r_k = vecs
    return pl.pallas_call(
        _rwkv_sample_prep_kernel,
        out_shape=(jax.ShapeDtypeStruct((n, RWKV_W), F32),) * 7,
        name="rwkv_sample_prep",
    )(p, prev, mu, w0, a0, k_k, k_a, r_k, wlb, alb, ones)


def _rwkv_sample_step_kernel(s_ref, r_ref, w_ref, k_ref, kk_ref, b_ref,
                             vcol_ref, bonus_ref, g_ref, beta_ref, o_ref, snew_ref):
    s = s_ref[...]
    s_kk = jnp.sum(s * kk_ref[...], axis=2, keepdims=True)
    s = s * w_ref[...] - s_kk * b_ref[...] + vcol_ref[...] * k_ref[...]
    snew_ref[...] = s
    o = jnp.sum(s * r_ref[...], axis=2, keepdims=True)
    mean = jnp.mean(o, axis=1, keepdims=True)
    d = o - mean
    var = jnp.mean(d * d, axis=1, keepdims=True)
    o_ref[...] = d * lax.rsqrt(var + GN_EPS) * g_ref[...] + beta_ref[...] + bonus_ref[...]


def _rwkv_sample_step(s0, rows, cols, g_col, beta_col):
    n = s0.shape[0]
    h, d = RWKV_HEADS, HEAD_DIM
    st = pl.BlockSpec((None, h, d, d), lambda bi: (bi, 0, 0, 0))
    rowspec = pl.BlockSpec((None, h, 1, d), lambda bi: (bi, 0, 0, 0))
    colspec = pl.BlockSpec((None, h, d, 1), lambda bi: (bi, 0, 0, 0))
    cconst = pl.BlockSpec((h, d, 1), lambda bi: (0, 0, 0))
    return pl.pallas_call(
        _rwkv_sample_step_kernel,
        grid=(n,),
        in_specs=[st] + [rowspec] * 5 + [colspec] * 2 + [cconst] * 2,
        out_specs=(colspec, st),
        out_shape=(jax.ShapeDtypeStruct((n, h, d, 1), F32),
                   jax.ShapeDtypeStruct((n, h, d, d), F32)),
        compiler_params=_params("parallel"),
        name="rwkv_sample_step",
    )(s0, *rows, *cols, g_col, beta_col)


def _sb_prompt_kernel(bias_ref, q_ref, kT_ref, vT_ref, cs_ref, o_ref):
    pair = pl.program_id(1)
    qi = pl.program_id(2)
    blk = SB_BLOCK
    cs_mat = cs_ref[...]
    q_all = q_ref[...] * ATTN_SCALE
    valid = (lax.broadcasted_iota(jnp.int32, (blk, blk), 1)
             < lax.broadcasted_iota(jnp.int32, (blk, blk), 0))

    heads = range(2)
    rows = [slice(hh * HEAD_DIM, (hh + 1) * HEAD_DIM) for hh in heads]
    biases = [bias_ref[2 * pair + hh] for hh in heads]
    chains = [(hh, u) for u in range(SB_Q_SUB) for hh in heads]
    qs = {(hh, u): q_all[u * blk:(u + 1) * blk, rows[hh]].astype(BF16) for hh, u in chains}

    def block(j, state, active, masked):
        start = pl.multiple_of(j * blk, blk)
        k16 = [kT_ref[rows[hh], pl.ds(start, blk)].astype(BF16) for hh in heads]
        v16 = [vT_ref[rows[hh], pl.ds(start, blk)].astype(BF16) for hh in heads]
        carries, accs = dict(state[0]), dict(state[1])
        tmp = {}

        def stage(s, ch):
            hh = ch[0]
            if s == 0:
                tmp[ch] = {"z": _dot(qs[ch], k16[hh]) + biases[hh]}
            elif s == 1:
                t = tmp[ch]
                sp, t["zs"] = _softplus_pair(t.pop("z"))
                if ch in masked:
                    sp = jnp.where(valid, sp, 0.0)
                t["first"] = sp[:, 0:1]
                t["parts"] = sp.astype(BF16)
            elif s == 2:
                t = tmp[ch]
                t["cs"] = _dot(t.pop("parts"), cs_mat)
            elif s == 3:
                t = tmp[ch]
                cs = t.pop("cs")
                a = jnp.exp(t.pop("zs") - cs - carries[ch])
                if ch in masked:
                    a = jnp.where(valid, a, 0.0)
                t["a"] = a.astype(BF16)
                carries[ch] = carries[ch] + (cs[:, 0:1] + t.pop("first"))
            else:
                accs[ch] = accs[ch] + _dot(tmp.pop(ch)["a"], v16[hh], NT)

        n_stage = 5
        for wave in range(len(active) + n_stage - 1):
            for s in reversed(range(n_stage)):
                idx = wave - s
                if 0 <= idx < len(active):
                    stage(s, active[idx])
        return (tuple(carries[ch] for ch in chains), tuple(accs[ch] for ch in chains))

    def as_dicts(state):
        return dict(zip(chains, state[0])), dict(zip(chains, state[1]))

    zeros_c = jnp.zeros((blk, 1), F32)
    zeros_o = jnp.zeros((blk, HEAD_DIM), F32)
    state = ((zeros_c,) * len(chains), (zeros_o,) * len(chains))
    top = SB_Q_SUB * qi
    for d in reversed(range(SB_Q_SUB)):
        active = [ch for ch in chains if ch[1] >= d]
        masked = {ch for ch in chains if ch[1] == d}
        state = block(top + d, as_dicts(state), active, masked)

    def body(jj, st):
        return block(top - 1 - jj, as_dicts(st), chains, set())

    state = lax.fori_loop(0, top, body, state)
    accs = dict(zip(chains, state[1]))
    o_ref[...] = jnp.concatenate(
        [jnp.concatenate([accs[(hh, u)] for hh in heads], axis=1) for u in range(SB_Q_SUB)], axis=0)


def _sb_prompt(q, kT, vT, bias, cs_mat):
    b, t, _ = q.shape
    blk = SB_BLOCK * SB_Q_SUB
    npair = SB_HEADS // 2
    kv = pl.BlockSpec((None, PAIR_W, t), lambda bi, p, i: (bi, p, 0))
    qo = pl.BlockSpec((None, blk, PAIR_W), lambda bi, p, i: (bi, i, p))
    return pl.pallas_call(
        _sb_prompt_kernel,
        grid=(b, npair, t // blk),
        in_specs=[
            pl.BlockSpec(memory_space=pltpu.SMEM),
            qo, kv, kv,
            pl.BlockSpec((SB_BLOCK, SB_BLOCK), lambda bi, p, i: (0, 0)),
        ],
        out_specs=qo,
        out_shape=jax.ShapeDtypeStruct((b, t, SB_W), F32),
        compiler_params=_params("parallel", "parallel", "arbitrary"),
        name="sb_prompt",
    )(bias, q, kT, vT, cs_mat)


def _sb_sample_kernel(pt_ref, *refs):
    npg = PAGES_PER_STEP
    k_refs = refs[:npg]
    v_refs = refs[npg:2 * npg]
    q_ref, bias_ref, sfx_ref, pages_ref, o_ref, acc_ref, carry_ref = refs[2 * npg:]
    j = pl.program_id(1)

    @pl.when(j == 0)
    def _():
        acc_ref[...] = jnp.zeros_like(acc_ref)
        carry_ref[...] = jnp.zeros_like(carry_ref)

    heads = range(SB_HEADS)
    nrow = SB_HEADS * npg
    pad = SB_SAMPLE_ROWS - nrow
    z = jnp.concatenate(
        [jnp.sum(k_refs[u][h] * q_ref[h], axis=0, keepdims=True) + bias_ref[h]
         for h in heads for u in range(npg)] + [jnp.zeros((pad, LANES), F32)], axis=0)
    sp, _ = _softplus_pair(z)
    sums = _dot_exact_rhs(sp, sfx_ref[...])
    suffix, tot = sums[:, :LANES], sums[:, LANES:]
    later = _dot_exact_lhs(pages_ref[...], tot)
    carry = jnp.concatenate(
        [jnp.broadcast_to(carry_ref[h], (npg, LANES)) for h in heads]
        + [jnp.zeros((pad, LANES), F32)], axis=0)
    a = jnp.exp(z - suffix - later - carry)
    for h in heads:
        acc = acc_ref[h]
        for u in range(npg):
            r = h * npg + u
            acc = acc + v_refs[u][h] * a[r:r + 1, :]
        acc_ref[h] = acc
        last = (h + 1) * npg - 1
        carry_ref[h] = carry_ref[h] + (later[last:last + 1, :] + tot[last:last + 1, :])

    @pl.when(j == pl.num_programs(1) - 1)
    def _():
        o_ref[...] = jnp.sum(acc_ref[...], axis=2)


def _sb_sample(page_table, cache_k, cache_v, q_bcast, bias_bcast):
    n, n_pages = page_table.shape
    npg = PAGES_PER_STEP
    h, d, pg = cache_k.shape[1:]
    last = n_pages - 1

    def page_spec(u):
        return pl.BlockSpec(
            (None, h, d, pg),
            lambda bi, j, pt: (pt[bi, last - (j * npg + u)], 0, 0, 0))

    nrow = SB_SAMPLE_ROWS
    assert h * npg <= nrow and pg == LANES
    tok = jnp.arange(pg)
    sfx_mat = jnp.concatenate(
        [(tok[:, None] >= tok[None, :]).astype(BF16), jnp.ones((pg, pg), BF16)], axis=1)
    r = jnp.arange(nrow)
    pages_mat = ((r[:, None] // npg == r[None, :] // npg) & (r[None, :] < r[:, None])
                 & (r[:, None] < h * npg)).astype(BF16)
    const2 = lambda bi, j, pt: (0, 0)

    grid_spec = pltpu.PrefetchScalarGridSpec(
        num_scalar_prefetch=1,
        grid=(n, n_pages // npg),
        in_specs=[page_spec(u) for u in range(npg)] * 2 + [
            pl.BlockSpec((None, h, d, pg), lambda bi, j, pt: (bi, 0, 0, 0)),
            pl.BlockSpec((h, 1, pg), lambda bi, j, pt: (0, 0, 0)),
            pl.BlockSpec((pg, 2 * pg), const2),
            pl.BlockSpec((nrow, nrow), const2),
        ],
        out_specs=pl.BlockSpec((None, h, d), lambda bi, j, pt: (bi, 0, 0)),
        scratch_shapes=[pltpu.VMEM((h, d, pg), F32), pltpu.VMEM((h, 1, pg), F32)],
    )
    return pl.pallas_call(
        _sb_sample_kernel,
        grid_spec=grid_spec,
        out_shape=jax.ShapeDtypeStruct((n, h, d), F32),
        compiler_params=_params("parallel", "arbitrary"),
        name="sb_sample",
    )(page_table, *([cache_k] * npg), *([cache_v] * npg), q_bcast, bias_bcast, sfx_mat, pages_mat)


def _head_ones(width):
    hid = jnp.arange(width) // HEAD_DIM
    return (hid[:, None] == hid[None, :]).astype(BF16)


def _rows_for(t, target):
    return target if t % target == 0 else t


def kernel(x_prompt, mem_prompt, x_sample, cache_sb_k, cache_sb_v, page_table, state_wkv, state_shift, cache_mem_k, cache_mem_v, norm_g, w_in, sb_bias, mu_shift, w0, w_lora_b, a0, a_lora_b, k_k, k_a, r_k, lnx_g, lnx_b, mem_norm_g, w_mem_k, w_mem_v, w_out, final_norm_g):
    depth = w_in.shape[0]
    assert depth == 1, "single-layer trunk"
    l = 0
    b_p, t_p, d = x_prompt.shape
    b_s = x_sample.shape[0]
    assert x_sample.shape[1] == 1
    n_mem = mem_prompt.shape[1]
    page = cache_sb_k.shape[2]

    kv0 = RWKV_COLS + SB_W
    kv1 = kv0 + 2 * SB_W
    w = w_in[l]
    wa = w[:, :kv0].astype(BF16)
    wb = w[:, kv1:].astype(BF16)
    wkvT = w[:, kv0:kv1].T.astype(BF16)
    wkT = w_mem_k[l].T.astype(BF16)
    wvT = w_mem_v[l].T.astype(BF16)
    wo = w_out[l].astype(BF16)
    fg = final_norm_g.reshape(1, d)
    r_k_flat = r_k[l].reshape(1, RWKV_W)
    ones_all = _head_ones(RWKV_W)
    c = RWKV_CHUNK
    tok = jnp.arange(c * RWKV_SUB)
    tri = ((tok[None, :] <= tok[:, None]) & (tok[None, :] // c == tok[:, None] // c)).astype(BF16)
    blk = SB_BLOCK
    ar = jnp.arange(blk)
    cs_mat = (ar[:, None] > ar[None, :]).astype(BF16)

    mkT, mvT = _mem_kv(mem_prompt, mem_norm_g[l:l + 1], wkT, wvT)
    p_rw, q_sb, xq, gate, kT, vT = _in_proj(
        x_prompt, norm_g[l:l + 1], wa, wb, wkvT, _rows_for(t_p, 512))
    vecs = (mu_shift[l:l + 1], w0[l:l + 1], a0[l:l + 1], k_k[l:l + 1], k_a[l:l + 1],
            r_k_flat, lnx_g[l:l + 1], lnx_b[l:l + 1])
    o_rw, h_fin = _rwkv_prompt(
        p_rw, jnp.zeros((b_p, 1, RWKV_COLS), F32),
        jnp.zeros((b_p, RWKV_HEADS, HEAD_DIM, HEAD_DIM), F32),
        vecs, w_lora_b[l], a_lora_b[l], tri, ones_all)
    o_sb = _sb_prompt(q_sb, kT, vT, sb_bias[l], cs_mat)
    o_x = _xattn(xq, mkT, mvT, _rows_for(t_p, 512))
    y_prompt = _out_proj(o_rw, o_sb, o_x, gate, x_prompt, wo, fg, _rows_for(t_p, 512))

    def tokens_major(xT, heads):
        return xT.reshape(1, xT.shape[0], heads, HEAD_DIM, xT.shape[2]).transpose(0, 1, 4, 2, 3)

    new_sb_k_p = tokens_major(kT, SB_HEADS)
    new_sb_v_p = tokens_major(vT, SB_HEADS)
    new_wkv_p = jnp.swapaxes(h_fin, -1, -2)[None]
    new_shift_p = p_rw[:, -1][None]
    new_mem_k_p = tokens_major(mkT, X_HEADS)
    new_mem_v_p = tokens_major(mvT, X_HEADS)

    xs2 = x_sample.reshape(1, b_s, d)
    p_rw_s, q_s, xq_s, gate_s, kT_s, vT_s = _in_proj(xs2, norm_g[l:l + 1], wa, wb, wkvT, b_s)
    p_rw_s = p_rw_s[0]
    svecs = (mu_shift[l:l + 1], w0[l:l + 1], a0[l:l + 1], k_k[l:l + 1], k_a[l:l + 1], r_k_flat)
    r_s, w_s, k2_s, v_s, kk_s, bv_s, bonus_s = _rwkv_sample_prep(
        p_rw_s, state_shift[l], svecs, w_lora_b[l], a_lora_b[l], ones_all)
    as_row = lambda a_: a_.reshape(b_s, RWKV_HEADS, 1, HEAD_DIM)
    as_col = lambda a_: a_.reshape(b_s, RWKV_HEADS, HEAD_DIM, 1)
    o_rw_s, new_wkv_s = _rwkv_sample_step(
        state_wkv[l],
        [as_row(r_s), as_row(w_s), as_row(k2_s), as_row(kk_s), as_row(bv_s)],
        [as_col(v_s), as_col(bonus_s)],
        lnx_g[l].reshape(RWKV_HEADS, HEAD_DIM, 1), lnx_b[l].reshape(RWKV_HEADS, HEAD_DIM, 1))
    o_rw_s = o_rw_s.reshape(1, b_s, RWKV_W)

    ck = cache_sb_k[l].transpose(0, 2, 3, 1)
    cv = cache_sb_v[l].transpose(0, 2, 3, 1)
    q_b = jnp.broadcast_to(
        (q_s[0] * ATTN_SCALE).reshape(b_s, SB_HEADS, HEAD_DIM, 1), (b_s, SB_HEADS, HEAD_DIM, page))
    bias_b = jnp.broadcast_to(sb_bias[l].reshape(SB_HEADS, 1, 1), (SB_HEADS, 1, page))
    o_sb_s = _sb_sample(page_table, ck, cv, q_b, bias_b).reshape(1, b_s, SB_W)

    mk_s = cache_mem_k[l].transpose(0, 2, 3, 1).reshape(b_s, X_W, n_mem)
    mv_s = cache_mem_v[l].transpose(0, 2, 3, 1).reshape(b_s, X_W, n_mem)
    xq_rows = jnp.broadcast_to(xq_s[0][:, None, :], (b_s, 8, X_W))
    o_x_s = _xattn(xq_rows, mk_s, mv_s, 8)[:, 0][None]
    y_sample = _out_proj(o_rw_s, o_sb_s, o_x_s, gate_s, xs2, wo, fg, b_s).reshape(b_s, 1, d)

    new_sb_k_s = kT_s[0].T.reshape(1, b_s, 1, SB_HEADS, HEAD_DIM)
    new_sb_v_s = vT_s[0].T.reshape(1, b_s, 1, SB_HEADS, HEAD_DIM)
    new_shift_s = p_rw_s[None]

    return (y_prompt, y_sample, new_sb_k_p, new_sb_v_p, new_wkv_p, new_shift_p,
            new_mem_k_p, new_mem_v_p, new_sb_k_s, new_sb_v_s, new_wkv_s[None], new_shift_s)
```

```python
import functools
import math

import jax
import jax.numpy as jnp
from jax import lax
from jax.experimental import pallas as pl
from jax.experimental.pallas import tpu as pltpu

F32 = jnp.float32
BF16 = jnp.bfloat16

HEAD_DIM = 64
RWKV_HEADS = 6
SB_HEADS = 6
X_HEADS = 4
RWKV_W = RWKV_HEADS * HEAD_DIM
SB_W = SB_HEADS * HEAD_DIM
X_W = X_HEADS * HEAD_DIM
LORA = 64
RWKV_COLS = 3 * RWKV_W + 2 * LORA
NORM_EPS = 1e-6
GN_EPS = 64e-5
DECAY_SCALE = math.exp(-0.5)
ATTN_SCALE = HEAD_DIM ** -0.5
LOG2E = math.log2(math.e)

LANES = 128
PAIR_W = 2 * HEAD_DIM
RWKV_CHUNK = 128
RWKV_SUB = 2
INV_BASE = 8
SB_BLOCK = 256
SB_Q_SUB = 4
PAGES_PER_STEP = 16
SAMPLE_GROUP = 8
SB_SAMPLE_ROWS = 128
VMEM_LIMIT = 48 * 1024 * 1024

NN = (((1,), (0,)), ((), ()))
NT = (((1,), (1,)), ((), ()))
TN = (((0,), (0,)), ((), ()))


def _dot(a, b, dims=NN):
    return lax.dot_general(a, b, dims, preferred_element_type=F32)


def _split2(x):
    hi = x.astype(BF16)
    lo = (x - hi.astype(F32)).astype(BF16)
    return hi, lo


def _split3(x):
    hi = x.astype(BF16)
    r1 = x - hi.astype(F32)
    mid = r1.astype(BF16)
    lo = (r1 - mid.astype(F32)).astype(BF16)
    return hi, mid, lo


def _dot_f32(a, b, dims=NN):
    (ka,), (kb,) = dims[0]
    ah, al = _split2(a)
    bh, bl = _split2(b)
    return _dot(jnp.concatenate([ah, ah, al], axis=ka), jnp.concatenate([bh, bl, bh], axis=kb), dims)


def _dot_bf16(a, b, dims=NN):
    return _dot(a.astype(BF16), b.astype(BF16), dims)


def _dot_exact_rhs(a, b_bf16, dims=NN):
    hi, mid, lo = _split3(a)
    return _dot(hi, b_bf16, dims) + (_dot(mid, b_bf16, dims) + _dot(lo, b_bf16, dims))


def _dot_exact_lhs(a_bf16, b, dims=NN):
    hi, mid, lo = _split3(b)
    return _dot(a_bf16, hi, dims) + (_dot(a_bf16, mid, dims) + _dot(a_bf16, lo, dims))


def _sigmoid(x):
    return 1.0 / (1.0 + jnp.exp(-x))


def _softplus_pair(z):
    sp = jnp.maximum(z, 0.0) + jnp.log(1.0 + jnp.exp2(jnp.abs(z) * -LOG2E))
    return sp, z - sp


def _rmsnorm_rows(x, g):
    ms = jnp.mean(x * x, axis=-1, keepdims=True)
    return x * lax.rsqrt(ms + NORM_EPS) * g


def _params(*sem):
    return pltpu.CompilerParams(dimension_semantics=sem, vmem_limit_bytes=VMEM_LIMIT)


KV_COL0 = RWKV_COLS + SB_W
KV_COL1 = KV_COL0 + 2 * SB_W


def _in_proj_kernel(x_ref, g_ref, w_ref, prw_ref, q_ref, xq_ref, gate_ref, k_ref, v_ref,
                    *, kv_feature_major):
    h = _rmsnorm_rows(x_ref[...], g_ref[...]).astype(BF16)
    pa = _dot(h, w_ref[:, :KV_COL0])
    prw_ref[...] = pa[:, :RWKV_COLS]
    q_ref[...] = pa[:, RWKV_COLS:]
    kv = _dot(h, w_ref[:, KV_COL0:KV_COL1])
    if kv_feature_major:
        k_ref[...] = kv[:, :SB_W].T
        v_ref[...] = kv[:, SB_W:].T
    else:
        k_ref[...] = kv[:, :SB_W]
        v_ref[...] = kv[:, SB_W:]
    pb = _dot(h, w_ref[:, KV_COL1:])
    xq_ref[...] = pb[:, :X_W]
    gate_ref[...] = pb[:, X_W:]


def _in_proj(x, norm_g, w16, rows, kv_feature_major):
    b, t, d = x.shape
    n_in = w16.shape[1]
    n_gate = n_in - KV_COL1 - X_W
    row = lambda bi, i: (bi, i, 0)
    col = lambda bi, i: (bi, 0, i)
    const = lambda bi, i: (0, 0)
    if kv_feature_major:
        kv_shape = jax.ShapeDtypeStruct((b, SB_W, t), F32)
        kv_spec = pl.BlockSpec((None, SB_W, rows), col)
    else:
        kv_shape = jax.ShapeDtypeStruct((b, t, SB_W), F32)
        kv_spec = pl.BlockSpec((None, rows, SB_W), row)
    out_shape = (
        jax.ShapeDtypeStruct((b, t, RWKV_COLS), F32),
        jax.ShapeDtypeStruct((b, t, SB_W), F32),
        jax.ShapeDtypeStruct((b, t, X_W), F32),
        jax.ShapeDtypeStruct((b, t, n_gate), F32),
        kv_shape, kv_shape,
    )
    return pl.pallas_call(
        functools.partial(_in_proj_kernel, kv_feature_major=kv_feature_major),
        grid=(b, t // rows),
        in_specs=[
            pl.BlockSpec((None, rows, d), row),
            pl.BlockSpec((1, d), const),
            pl.BlockSpec((d, n_in), const),
        ],
        out_specs=(
            pl.BlockSpec((None, rows, RWKV_COLS), row),
            pl.BlockSpec((None, rows, SB_W), row),
            pl.BlockSpec((None, rows, X_W), row),
            pl.BlockSpec((None, rows, n_gate), row),
            kv_spec, kv_spec,
        ),
        out_shape=out_shape,
        compiler_params=_params("parallel", "parallel"),
        name="in_proj",
    )(x, norm_g, w16)


def _mem_kv_kernel(mem_ref, g_ref, wkT_ref, wvT_ref, kT_ref, vT_ref):
    h = _rmsnorm_rows(mem_ref[...], g_ref[...]).astype(BF16)
    kT_ref[...] = _dot(wkT_ref[...], h, NT)
    vT_ref[...] = _dot(wvT_ref[...], h, NT)


def _mem_kv(mem, g, wkT, wvT):
    b, m, d = mem.shape
    const = lambda bi: (0, 0)
    blk = lambda bi: (bi, 0, 0)
    return pl.pallas_call(
        _mem_kv_kernel,
        grid=(b,),
        in_specs=[
            pl.BlockSpec((None, m, d), blk),
            pl.BlockSpec((1, d), const),
            pl.BlockSpec((X_W, d), const),
            pl.BlockSpec((X_W, d), const),
        ],
        out_specs=(pl.BlockSpec((None, X_W, m), blk), pl.BlockSpec((None, X_W, m), blk)),
        out_shape=(jax.ShapeDtypeStruct((b, X_W, m), F32),) * 2,
        compiler_params=_params("parallel"),
        name="mem_kv",
    )(mem, g, wkT, wvT)


def _xattn_kernel(xq_ref, kT_ref, vT_ref, o_ref):
    for g in range(xq_ref.shape[0]):
        for h in range(X_HEADS):
            sl = slice(h * HEAD_DIM, (h + 1) * HEAD_DIM)
            q = (xq_ref[g, :, sl] * ATTN_SCALE).astype(BF16)
            s = _dot(q, kT_ref[g, sl, :].astype(BF16))
            e = jnp.exp(s - jnp.max(s, axis=-1, keepdims=True))
            l = jnp.sum(e, axis=-1, keepdims=True)
            o_ref[g, :, sl] = _dot(e.astype(BF16), vT_ref[g, sl, :].astype(BF16), NT) / l


def _xattn(xq, kT, vT, rows, group):
    b, t, _ = xq.shape
    m = kT.shape[-1]
    row = lambda bi, i: (bi, i, 0)
    full = lambda bi, i: (bi, 0, 0)
    return pl.pallas_call(
        _xattn_kernel,
        grid=(b // group, t // rows),
        in_specs=[
            pl.BlockSpec((group, rows, X_W), row),
            pl.BlockSpec((group, X_W, m), full),
            pl.BlockSpec((group, X_W, m), full),
        ],
        out_specs=pl.BlockSpec((group, rows, X_W), row),
        out_shape=jax.ShapeDtypeStruct((b, t, X_W), F32),
        compiler_params=_params("parallel", "parallel"),
        name="xattn",
    )(xq, kT, vT)


def _out_proj_kernel(orw_ref, osb_ref, ox_ref, gate_ref, x_ref, w_ref, fg_ref, y_ref):
    g = gate_ref[...]
    sg = g * _sigmoid(g)
    a0, a1 = RWKV_W, RWKV_W + SB_W
    acc = _dot((orw_ref[...] * sg[:, :a0]).astype(BF16), w_ref[:a0, :])
    acc += _dot((osb_ref[...] * sg[:, a0:a1]).astype(BF16), w_ref[a0:a1, :])
    acc += _dot((ox_ref[...] * sg[:, a1:]).astype(BF16), w_ref[a1:, :])
    y_ref[...] = _rmsnorm_rows(x_ref[...] + acc, fg_ref[...])


def _out_proj(o_rw, o_sb, o_x, gate, x, w_out, fg, rows):
    b, t, d = x.shape
    dm = w_out.shape[0]
    row = lambda bi, i: (bi, i, 0)
    const = lambda bi, i: (0, 0)
    return pl.pallas_call(
        _out_proj_kernel,
        grid=(b, t // rows),
        in_specs=[
            pl.BlockSpec((None, rows, RWKV_W), row),
            pl.BlockSpec((None, rows, SB_W), row),
            pl.BlockSpec((None, rows, X_W), row),
            pl.BlockSpec((None, rows, dm), row),
            pl.BlockSpec((None, rows, d), row),
            pl.BlockSpec((dm, d), const),
            pl.BlockSpec((1, d), const),
        ],
        out_specs=pl.BlockSpec((None, rows, d), row),
        out_shape=jax.ShapeDtypeStruct((b, t, d), F32),
        compiler_params=_params("parallel", "parallel"),
        name="out_proj",
    )(o_rw, o_sb, o_x, gate, x, w_out, fg)


def _rwkv_token_math(r, k, v, xw, xa, w0, a0, k_k, k_a, r_k, wlb, alb, head_ones):
    log_w = -DECAY_SCALE * _sigmoid(w0 + _dot(jnp.tanh(xw).astype(BF16), wlb.astype(BF16)))
    a = _sigmoid(a0 + _dot(xa.astype(BF16), alb.astype(BF16)))
    kk = k * k_k
    ss = _dot_exact_rhs(kk * kk, head_ones)
    kk = kk * lax.rsqrt(jnp.maximum(ss, 1e-12))
    k2 = k * (1.0 + (a - 1.0) * k_a)
    bonus = _dot_exact_rhs(r * k2 * r_k, head_ones) * v
    return log_w, a, kk, k2, bonus


def _group_norm(o, g, b, head_ones):
    inv = 1.0 / HEAD_DIM
    mean = _dot_exact_rhs(o, head_ones) * inv
    d = o - mean
    var = _dot_exact_rhs(d * d, head_ones) * inv
    return d * lax.rsqrt(var + GN_EPS) * g + b


def _shift_mix(cur, last_row, mu, first_row_mask):
    prev = jnp.where(first_row_mask, last_row, pltpu.roll(cur, 1, axis=0))
    return cur + (prev - cur) * mu


def _rwkv_prompt_kernel(
        p_ref, prev_ref, first_ref, mu_ref,
        w0_ref, a0_ref, kk_ref, ka_ref, rk_ref, lg_ref, lb_ref, wlb_ref, alb_ref,
        h0_ref, tri_ref, ones_ref,
        o_ref, hfin_ref, h_scr):
    i = pl.program_id(1)
    c = RWKV_CHUNK
    rows = c * RWKV_SUB
    w3 = 3 * RWKV_W

    @pl.when(i == 0)
    def _():
        h_scr[...] = h0_ref[...]

    first = lax.broadcasted_iota(jnp.int32, (rows, RWKV_COLS), 0) == 0
    last = jnp.where(i == 0, first_ref[...], prev_ref[7:8, :])
    xs = _shift_mix(p_ref[...], last, mu_ref[...], first)
    r, k, v = xs[:, :RWKV_W], xs[:, RWKV_W:2 * RWKV_W], xs[:, 2 * RWKV_W:w3]
    head_ones = ones_ref[...]
    log_w, a, kk, k2, bonus = _rwkv_token_math(
        r, k, v, xs[:, w3:w3 + LORA], xs[:, w3 + LORA:], w0_ref[...], a0_ref[...], kk_ref[...],
        ka_ref[...], rk_ref[...], wlb_ref[...], alb_ref[...], head_ones)

    subs = range(RWKV_SUB)
    lam = _dot_exact_lhs(tri_ref[...], log_w)
    lam_ends = [lam[(sc + 1) * c - 1:(sc + 1) * c, :] for sc in subs]
    lam_c = jnp.concatenate([jnp.broadcast_to(le, (c, RWKV_W)) for le in lam_ends], axis=0)
    w_in = jnp.exp(lam)
    w_ex = jnp.exp(lam - log_w)
    w_inv = jnp.exp(-lam)
    w_end = jnp.exp(lam_c - lam)
    w_c = [jnp.exp(le) for le in lam_ends]
    bvec = kk * a
    at = -kk * w_ex
    bt = bvec * w_inv
    kt = k2 * w_inv
    rt = r * w_in
    bh = bvec * w_end
    kh = k2 * w_end

    colid = lax.broadcasted_iota(jnp.int32, (c, c), 1)
    rowid = lax.broadcasted_iota(jnp.int32, (c, c), 0)
    strict = colid < rowid
    incl = colid <= rowid
    eye64 = (lax.broadcasted_iota(jnp.int32, (HEAD_DIM, HEAD_DIM), 0)
             == lax.broadcasted_iota(jnp.int32, (HEAD_DIM, HEAD_DIM), 1))

    heads = range(RWKV_HEADS)
    chains = [(sc, hh) for sc in subs for hh in heads]
    cut = lambda x: {(sc, hh): x[sc * c:(sc + 1) * c, hh * HEAD_DIM:(hh + 1) * HEAD_DIM]
                     for sc, hh in chains}
    at_h, bt_h, kt_h, rt_h, v_h, bh_h, kh_h = map(cut, (at, bt, kt, rt, v, bh, kh))
    each = lambda fn: {ch: fn(ch) for ch in chains}
    scores = each(lambda ch: _dot_f32(
        jnp.concatenate([at_h[ch], rt_h[ch]], axis=0),
        jnp.concatenate([bt_h[ch], kt_h[ch]], axis=0), NT))
    n = each(lambda ch: jnp.where(strict, scores[ch][:c, :c], 0.0))
    aak = each(lambda ch: jnp.where(strict, scores[ch][:c, c:], 0.0))
    mrb = each(lambda ch: jnp.where(incl, scores[ch][c:, :c], 0.0))
    mrk = each(lambda ch: jnp.where(incl, scores[ch][c:, c:], 0.0))
    rb, cb = rowid // INV_BASE, colid // INV_BASE
    dg = each(lambda ch: jnp.where(rb == cb, n[ch], 0.0))
    tm = each(lambda ch: jnp.where(colid == rowid, 1.0, dg[ch]))
    span = 1
    while 2 * span < INV_BASE:
        dg = each(lambda ch: _dot_bf16(dg[ch], dg[ch]))
        tm = each(lambda ch: tm[ch] + _dot_bf16(tm[ch], dg[ch]))
        span *= 2
    m = INV_BASE
    while m < c:
        off = (rowid // (2 * m) == colid // (2 * m)) & (rowid // m != colid // m)
        tn = each(lambda ch: _dot_bf16(tm[ch], jnp.where(off, n[ch], 0.0)))
        tm = each(lambda ch: tm[ch] + _dot_bf16(tn[ch], tm[ch]))
        m *= 2
    akv = each(lambda ch: _dot_f32(aak[ch], v_h[ch]))
    mkv = each(lambda ch: _dot_f32(mrk[ch], v_h[ch]))
    khv = each(lambda ch: _dot_f32(kh_h[ch], v_h[ch], TN))
    pq = each(lambda ch: _dot_f32(tm[ch], jnp.concatenate([at_h[ch], akv[ch]], axis=1)))
    mpq = each(lambda ch: _dot_f32(mrb[ch], pq[ch]))
    gj = each(lambda ch: _dot_f32(bh_h[ch], pq[ch], TN))
    p2 = each(lambda ch: rt_h[ch] + mpq[ch][:, :HEAD_DIM])
    q2 = each(lambda ch: mpq[ch][:, HEAD_DIM:] + mkv[ch])
    g = each(lambda ch: gj[ch][:, :HEAD_DIM] + jnp.where(
        eye64, w_c[ch[0]][:, ch[1] * HEAD_DIM:(ch[1] + 1) * HEAD_DIM], 0.0))
    jm = each(lambda ch: gj[ch][:, HEAD_DIM:] + khv[ch])

    state = [h_scr[hh] for hh in heads]
    outs = []
    for sc in subs:
        outs.append(jnp.concatenate(
            [_dot_f32(p2[(sc, hh)], state[hh]) + q2[(sc, hh)] for hh in heads], axis=1))
        state = [_dot_f32(g[(sc, hh)], state[hh]) + jm[(sc, hh)] for hh in heads]
    for hh in heads:
        h_scr[hh] = state[hh]

    o = jnp.concatenate(outs, axis=0)
    o_ref[...] = _group_norm(o, lg_ref[...], lb_ref[...], head_ones) + bonus

    @pl.when(i == pl.num_programs(1) - 1)
    def _():
        hfin_ref[...] = h_scr[...]


def _rwkv_prompt(p_rw, prev0, h0, vecs, wlb, alb, tri, ones):
    b, t, _ = p_rw.shape
    c = RWKV_CHUNK * RWKV_SUB
    sub = 8
    mu, w0, a0, k_k, k_a, r_k, lnx_g, lnx_b = vecs
    const = lambda bi, i: (0, 0)
    hvec = pl.BlockSpec((1, RWKV_W), const)
    lora = pl.BlockSpec((LORA, RWKV_W), const)
    state = pl.BlockSpec((None, RWKV_HEADS, HEAD_DIM, HEAD_DIM), lambda bi, i: (bi, 0, 0, 0))

    return pl.pallas_call(
        _rwkv_prompt_kernel,
        grid=(b, t // c),
        in_specs=[
            pl.BlockSpec((None, c, RWKV_COLS), lambda bi, i: (bi, i, 0)),
            pl.BlockSpec((None, sub, RWKV_COLS),
                         lambda bi, i: (bi, jnp.maximum(i * (c // sub) - 1, 0), 0)),
            pl.BlockSpec((None, 1, RWKV_COLS), lambda bi, i: (bi, 0, 0)),
            pl.BlockSpec((1, RWKV_COLS), const),
            hvec, hvec, hvec, hvec, hvec, hvec, hvec, lora, lora,
            state,
            pl.BlockSpec((c, c), const),
            pl.BlockSpec((RWKV_W, RWKV_W), const),
        ],
        out_specs=(
            pl.BlockSpec((None, c, RWKV_W), lambda bi, i: (bi, i, 0)),
            state,
        ),
        out_shape=(
            jax.ShapeDtypeStruct((b, t, RWKV_W), F32),
            jax.ShapeDtypeStruct((b, RWKV_HEADS, HEAD_DIM, HEAD_DIM), F32),
        ),
        scratch_shapes=[pltpu.VMEM((RWKV_HEADS, HEAD_DIM, HEAD_DIM), F32)],
        compiler_params=_params("parallel", "arbitrary"),
        name="rwkv_prompt",
    )(p_rw, p_rw, prev0, mu, w0, a0, k_k, k_a, r_k, lnx_g, lnx_b, wlb, alb, h0, tri, ones)


def _rwkv_sample_prep_kernel(p_ref, prev_ref, mu_ref, w0_ref, a0_ref, kk_ref, ka_ref, rk_ref,
                             wlb_ref, alb_ref, ones_ref,
                             r_ref, w_ref, k2_ref, v_ref, kkn_ref, b_ref, bonus_ref):
    pf = p_ref[...]
    xs = pf + (prev_ref[...] - pf) * mu_ref[...]
    w3 = 3 * RWKV_W
    r, k, v = xs[:, :RWKV_W], xs[:, RWKV_W:2 * RWKV_W], xs[:, 2 * RWKV_W:w3]
    log_w, a, kk, k2, bonus = _rwkv_token_math(
        r, k, v, xs[:, w3:w3 + LORA], xs[:, w3 + LORA:], w0_ref[...], a0_ref[...], kk_ref[...],
        ka_ref[...], rk_ref[...], wlb_ref[...], alb_ref[...], ones_ref[...])
    r_ref[...] = r
    w_ref[...] = jnp.exp(log_w)
    k2_ref[...] = k2
    v_ref[...] = v
    kkn_ref[...] = kk
    b_ref[...] = kk * a
    bonus_ref[...] = bonus


def _rwkv_sample_prep(p, prev, vecs, wlb, alb, ones):
    n = p.shape[0]
    mu, w0, a0, k_k, k_a, r_k = vecs
    return pl.pallas_call(
        _rwkv_sample_prep_kernel,
        out_shape=(jax.ShapeDtypeStruct((n, RWKV_W), F32),) * 7,
        name="rwkv_sample_prep",
    )(p, prev, mu, w0, a0, k_k, k_a, r_k, wlb, alb, ones)


def _rwkv_sample_step_kernel(s_ref, r_ref, w_ref, k_ref, kk_ref, b_ref, v_ref, bonus_ref,
                             g_ref, beta_ref, o_ref, snew_ref):
    eye = (lax.broadcasted_iota(jnp.int32, (HEAD_DIM, HEAD_DIM), 0)
           == lax.broadcasted_iota(jnp.int32, (HEAD_DIM, HEAD_DIM), 1))
    to_col = lambda row: jnp.sum(jnp.where(eye, row, 0.0), axis=3, keepdims=True)
    to_row = lambda col: jnp.sum(jnp.where(eye, col, 0.0), axis=2, keepdims=True)
    s = s_ref[...]
    s_kk = jnp.sum(s * kk_ref[...], axis=3, keepdims=True)
    s = s * w_ref[...] - s_kk * b_ref[...] + to_col(v_ref[...]) * k_ref[...]
    snew_ref[...] = s
    o = to_row(jnp.sum(s * r_ref[...], axis=3, keepdims=True))
    mean = jnp.mean(o, axis=3, keepdims=True)
    d = o - mean
    var = jnp.mean(d * d, axis=3, keepdims=True)
    o_ref[...] = d * lax.rsqrt(var + GN_EPS) * g_ref[...] + beta_ref[...] + bonus_ref[...]


def _rwkv_sample_step(s0, rows, g_row, beta_row, group):
    n = s0.shape[0]
    h, d = RWKV_HEADS, HEAD_DIM
    st = pl.BlockSpec((group, h, d, d), lambda bi: (bi, 0, 0, 0))
    rowspec = pl.BlockSpec((group, h, 1, d), lambda bi: (bi, 0, 0, 0))
    cconst = pl.BlockSpec((h, 1, d), lambda bi: (0, 0, 0))
    return pl.pallas_call(
        _rwkv_sample_step_kernel,
        grid=(n // group,),
        in_specs=[st] + [rowspec] * 7 + [cconst] * 2,
        out_specs=(rowspec, st),
        out_shape=(jax.ShapeDtypeStruct((n, h, 1, d), F32),
                   jax.ShapeDtypeStruct((n, h, d, d), F32)),
        compiler_params=_params("parallel"),
        name="rwkv_sample_step",
    )(s0, *rows, g_row, beta_row)


def _sb_prompt_kernel(bias_ref, q_ref, kT_ref, vT_ref, cs_ref, o_ref):
    pair = pl.program_id(1)
    qi = pl.program_id(2)
    blk = SB_BLOCK
    cs_mat = cs_ref[...]
    q_all = q_ref[...] * ATTN_SCALE
    valid = (lax.broadcasted_iota(jnp.int32, (blk, blk), 1)
             < lax.broadcasted_iota(jnp.int32, (blk, blk), 0))

    heads = range(2)
    rows = [slice(hh * HEAD_DIM, (hh + 1) * HEAD_DIM) for hh in heads]
    biases = [bias_ref[2 * pair + hh] for hh in heads]
    chains = [(hh, u) for u in range(SB_Q_SUB) for hh in heads]
    qs = {(hh, u): q_all[u * blk:(u + 1) * blk, rows[hh]].astype(BF16) for hh, u in chains}

    def block(j, state, active, masked):
        start = pl.multiple_of(j * blk, blk)
        k16 = [kT_ref[rows[hh], pl.ds(start, blk)].astype(BF16) for hh in heads]
        v16 = [vT_ref[rows[hh], pl.ds(start, blk)].astype(BF16) for hh in heads]
        carries, accs = dict(state[0]), dict(state[1])
        tmp = {}

        def stage(s, ch):
            hh = ch[0]
            if s == 0:
                tmp[ch] = {"z": _dot(qs[ch], k16[hh]) + biases[hh]}
            elif s == 1:
                t = tmp[ch]
                sp, t["zs"] = _softplus_pair(t.pop("z"))
                if ch in masked:
                    sp = jnp.where(valid, sp, 0.0)
                t["first"] = sp[:, 0:1]
                t["parts"] = sp.astype(BF16)
            elif s == 2:
                t = tmp[ch]
                t["cs"] = _dot(t.pop("parts"), cs_mat)
            elif s == 3:
                t = tmp[ch]
                cs = t.pop("cs")
                a = jnp.exp(t.pop("zs") - cs - carries[ch])
                if ch in masked:
                    a = jnp.where(valid, a, 0.0)
                t["a"] = a.astype(BF16)
                carries[ch] = carries[ch] + (cs[:, 0:1] + t.pop("first"))
            else:
                accs[ch] = accs[ch] + _dot(tmp.pop(ch)["a"], v16[hh], NT)

        n_stage = 5
        for wave in range(len(active) + n_stage - 1):
            for s in reversed(range(n_stage)):
                idx = wave - s
                if 0 <= idx < len(active):
                    stage(s, active[idx])
        return (tuple(carries[ch] for ch in chains), tuple(accs[ch] for ch in chains))

    def as_dicts(state):
        return dict(zip(chains, state[0])), dict(zip(chains, state[1]))

    zeros_c = jnp.zeros((blk, 1), F32)
    zeros_o = jnp.zeros((blk, HEAD_DIM), F32)
    state = ((zeros_c,) * len(chains), (zeros_o,) * len(chains))
    top = SB_Q_SUB * qi
    for d in reversed(range(SB_Q_SUB)):
        active = [ch for ch in chains if ch[1] >= d]
        masked = {ch for ch in chains if ch[1] == d}
        state = block(top + d, as_dicts(state), active, masked)

    def body(jj, st):
        return block(top - 1 - jj, as_dicts(st), chains, set())

    state = lax.fori_loop(0, top, body, state)
    accs = dict(zip(chains, state[1]))
    o_ref[...] = jnp.concatenate(
        [jnp.concatenate([accs[(hh, u)] for hh in heads], axis=1) for u in range(SB_Q_SUB)], axis=0)


def _sb_prompt(q, kT, vT, bias, cs_mat):
    b, t, _ = q.shape
    blk = SB_BLOCK * SB_Q_SUB
    npair = SB_HEADS // 2
    kv = pl.BlockSpec((None, PAIR_W, t), lambda bi, p, i: (bi, p, 0))
    qo = pl.BlockSpec((None, blk, PAIR_W), lambda bi, p, i: (bi, i, p))
    return pl.pallas_call(
        _sb_prompt_kernel,
        grid=(b, npair, t // blk),
        in_specs=[
            pl.BlockSpec(memory_space=pltpu.SMEM),
            qo, kv, kv,
            pl.BlockSpec((SB_BLOCK, SB_BLOCK), lambda bi, p, i: (0, 0)),
        ],
        out_specs=qo,
        out_shape=jax.ShapeDtypeStruct((b, t, SB_W), F32),
        compiler_params=_params("parallel", "parallel", "arbitrary"),
        name="sb_prompt",
    )(bias, q, kT, vT, cs_mat)


def _sb_sample_kernel(pt_ref, *refs):
    npg = PAGES_PER_STEP
    k_refs = refs[:npg]
    v_refs = refs[npg:2 * npg]
    q_ref, bias_ref, sfx_ref, pages_ref, o_ref, acc_ref, carry_ref = refs[2 * npg:]
    j = pl.program_id(1)

    @pl.when(j == 0)
    def _():
        acc_ref[...] = jnp.zeros_like(acc_ref)
        carry_ref[...] = jnp.zeros_like(carry_ref)

    heads = range(SB_HEADS)
    nrow = SB_HEADS * npg
    pad = SB_SAMPLE_ROWS - nrow
    z = jnp.concatenate(
        [jnp.sum(k_refs[u][h] * q_ref[h], axis=0, keepdims=True) + bias_ref[h]
         for h in heads for u in range(npg)] + [jnp.zeros((pad, LANES), F32)], axis=0)
    sp, _ = _softplus_pair(z)
    sums = _dot_exact_rhs(sp, sfx_ref[...])
    suffix, tot = sums[:, :LANES], sums[:, LANES:]
    later = _dot_exact_lhs(pages_ref[...], tot)
    carry = jnp.concatenate(
        [jnp.broadcast_to(carry_ref[h], (npg, LANES)) for h in heads]
        + [jnp.zeros((pad, LANES), F32)], axis=0)
    a = jnp.exp(z - suffix - later - carry)
    for h in heads:
        acc = acc_ref[h]
        for u in range(npg):
            r = h * npg + u
            acc = acc + v_refs[u][h] * a[r:r + 1, :]
        acc_ref[h] = acc
        last = (h + 1) * npg - 1
        carry_ref[h] = carry_ref[h] + (later[last:last + 1, :] + tot[last:last + 1, :])

    @pl.when(j == pl.num_programs(1) - 1)
    def _():
        o_ref[...] = jnp.sum(acc_ref[...], axis=2)


def _sb_sample(page_table, cache_k, cache_v, q_bcast, bias_bcast):
    n, n_pages = page_table.shape
    npg = PAGES_PER_STEP
    h, d, pg = cache_k.shape[1:]
    last = n_pages - 1

    def page_spec(u):
        return pl.BlockSpec(
            (None, h, d, pg),
            lambda bi, j, pt: (pt[bi, last - (j * npg + u)], 0, 0, 0))

    nrow = SB_SAMPLE_ROWS
    assert h * npg <= nrow and pg == LANES
    tok = jnp.arange(pg)
    sfx_mat = jnp.concatenate(
        [(tok[:, None] >= tok[None, :]).astype(BF16), jnp.ones((pg, pg), BF16)], axis=1)
    r = jnp.arange(nrow)
    pages_mat = ((r[:, None] // npg == r[None, :] // npg) & (r[None, :] < r[:, None])
                 & (r[:, None] < h * npg)).astype(BF16)
    const2 = lambda bi, j, pt: (0, 0)

    grid_spec = pltpu.PrefetchScalarGridSpec(
        num_scalar_prefetch=1,
        grid=(n, n_pages // npg),
        in_specs=[page_spec(u) for u in range(npg)] * 2 + [
            pl.BlockSpec((None, h, d, pg), lambda bi, j, pt: (bi, 0, 0, 0)),
            pl.BlockSpec((h, 1, pg), lambda bi, j, pt: (0, 0, 0)),
            pl.BlockSpec((pg, 2 * pg), const2),
            pl.BlockSpec((nrow, nrow), const2),
        ],
        out_specs=pl.BlockSpec((None, h, d), lambda bi, j, pt: (bi, 0, 0)),
        scratch_shapes=[pltpu.VMEM((h, d, pg), F32), pltpu.VMEM((h, 1, pg), F32)],
    )
    return pl.pallas_call(
        _sb_sample_kernel,
        grid_spec=grid_spec,
        out_shape=jax.ShapeDtypeStruct((n, h, d), F32),
        compiler_params=_params("parallel", "arbitrary"),
        name="sb_sample",
    )(page_table, *([cache_k] * npg), *([cache_v] * npg), q_bcast, bias_bcast, sfx_mat, pages_mat)


def _head_ones(width):
    hid = jnp.arange(width) // HEAD_DIM
    return (hid[:, None] == hid[None, :]).astype(BF16)


def _rows_for(t, target):
    return target if t % target == 0 else t


def kernel(x_prompt, mem_prompt, x_sample, cache_sb_k, cache_sb_v, page_table, state_wkv, state_shift, cache_mem_k, cache_mem_v, norm_g, w_in, sb_bias, mu_shift, w0, w_lora_b, a0, a_lora_b, k_k, k_a, r_k, lnx_g, lnx_b, mem_norm_g, w_mem_k, w_mem_v, w_out, final_norm_g):
    depth = w_in.shape[0]
    assert depth == 1, "single-layer trunk"
    l = 0
    b_p, t_p, d = x_prompt.shape
    b_s = x_sample.shape[0]
    assert x_sample.shape[1] == 1
    n_mem = mem_prompt.shape[1]
    page = cache_sb_k.shape[2]

    w16 = w_in[l].astype(BF16)
    wkT = w_mem_k[l].T.astype(BF16)
    wvT = w_mem_v[l].T.astype(BF16)
    wo = w_out[l].astype(BF16)
    fg = final_norm_g.reshape(1, d)
    r_k_flat = r_k[l].reshape(1, RWKV_W)
    ones_all = _head_ones(RWKV_W)
    c = RWKV_CHUNK
    tok = jnp.arange(c * RWKV_SUB)
    tri = ((tok[None, :] <= tok[:, None]) & (tok[None, :] // c == tok[:, None] // c)).astype(BF16)
    blk = SB_BLOCK
    ar = jnp.arange(blk)
    cs_mat = (ar[:, None] > ar[None, :]).astype(BF16)

    mkT, mvT = _mem_kv(mem_prompt, mem_norm_g[l:l + 1], wkT, wvT)
    p_rw, q_sb, xq, gate, kT, vT = _in_proj(
        x_prompt, norm_g[l:l + 1], w16, _rows_for(t_p, 512), True)
    vecs = (mu_shift[l:l + 1], w0[l:l + 1], a0[l:l + 1], k_k[l:l + 1], k_a[l:l + 1],
            r_k_flat, lnx_g[l:l + 1], lnx_b[l:l + 1])
    o_rw, h_fin = _rwkv_prompt(
        p_rw, jnp.zeros((b_p, 1, RWKV_COLS), F32),
        jnp.zeros((b_p, RWKV_HEADS, HEAD_DIM, HEAD_DIM), F32),
        vecs, w_lora_b[l], a_lora_b[l], tri, ones_all)
    o_sb = _sb_prompt(q_sb, kT, vT, sb_bias[l], cs_mat)
    o_x = _xattn(xq, mkT, mvT, _rows_for(t_p, 512), 1)
    y_prompt = _out_proj(o_rw, o_sb, o_x, gate, x_prompt, wo, fg, _rows_for(t_p, 512))

    def tokens_major(xT, heads):
        return xT.reshape(1, xT.shape[0], heads, HEAD_DIM, xT.shape[2]).transpose(0, 1, 4, 2, 3)

    new_sb_k_p = tokens_major(kT, SB_HEADS)
    new_sb_v_p = tokens_major(vT, SB_HEADS)
    new_wkv_p = jnp.swapaxes(h_fin, -1, -2)[None]
    new_shift_p = p_rw[:, -1][None]
    new_mem_k_p = tokens_major(mkT, X_HEADS)
    new_mem_v_p = tokens_major(mvT, X_HEADS)

    xs2 = x_sample.reshape(1, b_s, d)
    p_rw_s, q_s, xq_s, gate_s, sbk_s, sbv_s = _in_proj(xs2, norm_g[l:l + 1], w16, b_s, False)
    p_rw_s = p_rw_s[0]
    svecs = (mu_shift[l:l + 1], w0[l:l + 1], a0[l:l + 1], k_k[l:l + 1], k_a[l:l + 1], r_k_flat)
    r_s, w_s, k2_s, v_s, kk_s, bv_s, bonus_s = _rwkv_sample_prep(
        p_rw_s, state_shift[l], svecs, w_lora_b[l], a_lora_b[l], ones_all)
    as_row = lambda a_: a_.reshape(b_s, RWKV_HEADS, 1, HEAD_DIM)
    o_rw_s, new_wkv_s = _rwkv_sample_step(
        state_wkv[l],
        [as_row(t_) for t_ in (r_s, w_s, k2_s, kk_s, bv_s, v_s, bonus_s)],
        lnx_g[l].reshape(RWKV_HEADS, 1, HEAD_DIM), lnx_b[l].reshape(RWKV_HEADS, 1, HEAD_DIM),
        SAMPLE_GROUP if b_s % SAMPLE_GROUP == 0 else 1)
    o_rw_s = o_rw_s.reshape(1, b_s, RWKV_W)

    ck = cache_sb_k[l].transpose(0, 2, 3, 1)
    cv = cache_sb_v[l].transpose(0, 2, 3, 1)
    q_b = jnp.broadcast_to(
        (q_s[0] * ATTN_SCALE).reshape(b_s, SB_HEADS, HEAD_DIM, 1), (b_s, SB_HEADS, HEAD_DIM, page))
    bias_b = jnp.broadcast_to(sb_bias[l].reshape(SB_HEADS, 1, 1), (SB_HEADS, 1, page))
    o_sb_s = _sb_sample(page_table, ck, cv, q_b, bias_b).reshape(1, b_s, SB_W)

    mk_s = cache_mem_k[l].transpose(0, 2, 3, 1).reshape(b_s, X_W, n_mem)
    mv_s = cache_mem_v[l].transpose(0, 2, 3, 1).reshape(b_s, X_W, n_mem)
    xq_rows = jnp.broadcast_to(xq_s[0][:, None, :], (b_s, 8, X_W))
    o_x_s = _xattn(xq_rows, mk_s, mv_s, 8, SAMPLE_GROUP if b_s % SAMPLE_GROUP == 0 else 1)[:, 0][None]
    y_sample = _out_proj(o_rw_s, o_sb_s, o_x_s, gate_s, xs2, wo, fg, b_s).reshape(b_s, 1, d)

    new_sb_k_s = sbk_s.reshape(1, b_s, 1, SB_HEADS, HEAD_DIM)
    new_sb_v_s = sbv_s.reshape(1, b_s, 1, SB_HEADS, HEAD_DIM)
    new_shift_s = p_rw_s[None]

    return (y_prompt, y_sample, new_sb_k_p, new_sb_v_p, new_wkv_p, new_shift_p,
            new_mem_k_p, new_mem_v_p, new_sb_k_s, new_sb_v_s, new_wkv_s[None], new_shift_s)
```

```python
import functools
import math

import jax
import jax.numpy as jnp
from jax import lax
from jax.experimental import pallas as pl
from jax.experimental.pallas import tpu as pltpu

F32 = jnp.float32
BF16 = jnp.bfloat16

HEAD_DIM = 64
RWKV_HEADS = 6
SB_HEADS = 6
X_HEADS = 4
RWKV_W = RWKV_HEADS * HEAD_DIM
SB_W = SB_HEADS * HEAD_DIM
X_W = X_HEADS * HEAD_DIM
LORA = 64
RWKV_COLS = 3 * RWKV_W + 2 * LORA
NORM_EPS = 1e-6
GN_EPS = 64e-5
DECAY_SCALE = math.exp(-0.5)
ATTN_SCALE = HEAD_DIM ** -0.5
LOG2E = math.log2(math.e)

LANES = 128
PAIR_W = 2 * HEAD_DIM
RWKV_CHUNK = 128
RWKV_SUB = 2
INV_BASE = 8
SB_BLOCK = 256
SB_Q_SUB = 4
PAGES_PER_STEP = 16
SAMPLE_GROUP = 8
SB_SAMPLE_ROWS = 128
VMEM_LIMIT = 48 * 1024 * 1024

NN = (((1,), (0,)), ((), ()))
NT = (((1,), (1,)), ((), ()))
TN = (((0,), (0,)), ((), ()))


def _dot(a, b, dims=NN):
    return lax.dot_general(a, b, dims, preferred_element_type=F32)


def _split2(x):
    hi = x.astype(BF16)
    lo = (x - hi.astype(F32)).astype(BF16)
    return hi, lo


def _split3(x):
    hi = x.astype(BF16)
    r1 = x - hi.astype(F32)
    mid = r1.astype(BF16)
    lo = (r1 - mid.astype(F32)).astype(BF16)
    return hi, mid, lo


def _dot_f32(a, b, dims=NN):
    (ka,), (kb,) = dims[0]
    ah, al = _split2(a)
    bh, bl = _split2(b)
    return _dot(jnp.concatenate([ah, ah, al], axis=ka), jnp.concatenate([bh, bl, bh], axis=kb), dims)


def _dot_bf16(a, b, dims=NN):
    return _dot(a.astype(BF16), b.astype(BF16), dims)


def _dot_exact_rhs(a, b_bf16, dims=NN):
    hi, mid, lo = _split3(a)
    return _dot(hi, b_bf16, dims) + (_dot(mid, b_bf16, dims) + _dot(lo, b_bf16, dims))


def _dot_exact_lhs(a_bf16, b, dims=NN):
    hi, mid, lo = _split3(b)
    return _dot(a_bf16, hi, dims) + (_dot(a_bf16, mid, dims) + _dot(a_bf16, lo, dims))


def _sigmoid(x):
    return 1.0 / (1.0 + jnp.exp(-x))


def _softplus_pair(z):
    sp = jnp.maximum(z, 0.0) + jnp.log(1.0 + jnp.exp2(jnp.abs(z) * -LOG2E))
    return sp, z - sp


def _rmsnorm_rows(x, g):
    ms = jnp.mean(x * x, axis=-1, keepdims=True)
    return x * lax.rsqrt(ms + NORM_EPS) * g


def _params(*sem):
    return pltpu.CompilerParams(dimension_semantics=sem, vmem_limit_bytes=VMEM_LIMIT)


KV_COL0 = RWKV_COLS + SB_W
KV_COL1 = KV_COL0 + 2 * SB_W


def _in_proj_kernel(x_ref, g_ref, w_ref, prw_ref, q_ref, xq_ref, gate_ref, k_ref, v_ref,
                    *, kv_feature_major):
    h = _rmsnorm_rows(x_ref[...], g_ref[...]).astype(BF16)
    pa = _dot(h, w_ref[:, :KV_COL0])
    prw_ref[...] = pa[:, :RWKV_COLS]
    q_ref[...] = pa[:, RWKV_COLS:]
    kv = _dot(h, w_ref[:, KV_COL0:KV_COL1])
    if kv_feature_major:
        k_ref[...] = kv[:, :SB_W].T
        v_ref[...] = kv[:, SB_W:].T
    else:
        k_ref[...] = kv[:, :SB_W]
        v_ref[...] = kv[:, SB_W:]
    pb = _dot(h, w_ref[:, KV_COL1:])
    xq_ref[...] = pb[:, :X_W]
    gate_ref[...] = pb[:, X_W:]


def _in_proj(x, norm_g, w16, rows, kv_feature_major):
    b, t, d = x.shape
    n_in = w16.shape[1]
    n_gate = n_in - KV_COL1 - X_W
    row = lambda bi, i: (bi, i, 0)
    col = lambda bi, i: (bi, 0, i)
    const = lambda bi, i: (0, 0)
    if kv_feature_major:
        kv_shape = jax.ShapeDtypeStruct((b, SB_W, t), F32)
        kv_spec = pl.BlockSpec((None, SB_W, rows), col)
    else:
        kv_shape = jax.ShapeDtypeStruct((b, t, SB_W), F32)
        kv_spec = pl.BlockSpec((None, rows, SB_W), row)
    out_shape = (
        jax.ShapeDtypeStruct((b, t, RWKV_COLS), F32),
        jax.ShapeDtypeStruct((b, t, SB_W), F32),
        jax.ShapeDtypeStruct((b, t, X_W), F32),
        jax.ShapeDtypeStruct((b, t, n_gate), F32),
        kv_shape, kv_shape,
    )
    return pl.pallas_call(
        functools.partial(_in_proj_kernel, kv_feature_major=kv_feature_major),
        grid=(b, t // rows),
        in_specs=[
            pl.BlockSpec((None, rows, d), row),
            pl.BlockSpec((1, d), const),
            pl.BlockSpec((d, n_in), const),
        ],
        out_specs=(
            pl.BlockSpec((None, rows, RWKV_COLS), row),
            pl.BlockSpec((None, rows, SB_W), row),
            pl.BlockSpec((None, rows, X_W), row),
            pl.BlockSpec((None, rows, n_gate), row),
            kv_spec, kv_spec,
        ),
        out_shape=out_shape,
        compiler_params=_params("parallel", "parallel"),
        name="in_proj",
    )(x, norm_g, w16)


def _mem_kv_kernel(mem_ref, g_ref, wkT_ref, wvT_ref, kT_ref, vT_ref):
    h = _rmsnorm_rows(mem_ref[...], g_ref[...]).astype(BF16)
    kT_ref[...] = _dot(wkT_ref[...], h, NT)
    vT_ref[...] = _dot(wvT_ref[...], h, NT)


def _mem_kv(mem, g, wkT, wvT):
    b, m, d = mem.shape
    const = lambda bi: (0, 0)
    blk = lambda bi: (bi, 0, 0)
    return pl.pallas_call(
        _mem_kv_kernel,
        grid=(b,),
        in_specs=[
            pl.BlockSpec((None, m, d), blk),
            pl.BlockSpec((1, d), const),
            pl.BlockSpec((X_W, d), const),
            pl.BlockSpec((X_W, d), const),
        ],
        out_specs=(pl.BlockSpec((None, X_W, m), blk), pl.BlockSpec((None, X_W, m), blk)),
        out_shape=(jax.ShapeDtypeStruct((b, X_W, m), F32),) * 2,
        compiler_params=_params("parallel"),
        name="mem_kv",
    )(mem, g, wkT, wvT)


def _xattn_kernel(xq_ref, kT_ref, vT_ref, o_ref):
    heads = range(X_HEADS)
    sl = [slice(h * HEAD_DIM, (h + 1) * HEAD_DIM) for h in heads]
    for g in range(xq_ref.shape[0]):
        s = [_dot((xq_ref[g, :, sl[h]] * ATTN_SCALE).astype(BF16), kT_ref[g, sl[h], :].astype(BF16))
             for h in heads]
        e = [jnp.exp(s[h] - jnp.max(s[h], axis=-1, keepdims=True)) for h in heads]
        l = [jnp.sum(e[h], axis=-1, keepdims=True) for h in heads]
        o = [_dot(e[h].astype(BF16), vT_ref[g, sl[h], :].astype(BF16), NT) / l[h] for h in heads]
        o_ref[g] = jnp.concatenate(o, axis=1)


def _xattn(xq, kT, vT, rows, group):
    b, t, _ = xq.shape
    m = kT.shape[-1]
    row = lambda bi, i: (bi, i, 0)
    full = lambda bi, i: (bi, 0, 0)
    return pl.pallas_call(
        _xattn_kernel,
        grid=(b // group, t // rows),
        in_specs=[
            pl.BlockSpec((group, rows, X_W), row),
            pl.BlockSpec((group, X_W, m), full),
            pl.BlockSpec((group, X_W, m), full),
        ],
        out_specs=pl.BlockSpec((group, rows, X_W), row),
        out_shape=jax.ShapeDtypeStruct((b, t, X_W), F32),
        compiler_params=_params("parallel", "parallel"),
        name="xattn",
    )(xq, kT, vT)


def _out_proj_kernel(orw_ref, osb_ref, ox_ref, gate_ref, x_ref, w_ref, fg_ref, y_ref):
    g = gate_ref[...]
    sg = g * _sigmoid(g)
    a0, a1 = RWKV_W, RWKV_W + SB_W
    acc = _dot((orw_ref[...] * sg[:, :a0]).astype(BF16), w_ref[:a0, :])
    acc += _dot((osb_ref[...] * sg[:, a0:a1]).astype(BF16), w_ref[a0:a1, :])
    acc += _dot((ox_ref[...] * sg[:, a1:]).astype(BF16), w_ref[a1:, :])
    y_ref[...] = _rmsnorm_rows(x_ref[...] + acc, fg_ref[...])


def _out_proj(o_rw, o_sb, o_x, gate, x, w_out, fg, rows):
    b, t, d = x.shape
    dm = w_out.shape[0]
    row = lambda bi, i: (bi, i, 0)
    const = lambda bi, i: (0, 0)
    return pl.pallas_call(
        _out_proj_kernel,
        grid=(b, t // rows),
        in_specs=[
            pl.BlockSpec((None, rows, RWKV_W), row),
            pl.BlockSpec((None, rows, SB_W), row),
            pl.BlockSpec((None, rows, X_W), row),
            pl.BlockSpec((None, rows, dm), row),
            pl.BlockSpec((None, rows, d), row),
            pl.BlockSpec((dm, d), const),
            pl.BlockSpec((1, d), const),
        ],
        out_specs=pl.BlockSpec((None, rows, d), row),
        out_shape=jax.ShapeDtypeStruct((b, t, d), F32),
        compiler_params=_params("parallel", "parallel"),
        name="out_proj",
    )(o_rw, o_sb, o_x, gate, x, w_out, fg)


def _rwkv_token_math(r, k, v, xw, xa, w0, a0, k_k, k_a, r_k, wlb, alb, head_ones):
    log_w = -DECAY_SCALE * _sigmoid(w0 + _dot(jnp.tanh(xw).astype(BF16), wlb.astype(BF16)))
    a = _sigmoid(a0 + _dot(xa.astype(BF16), alb.astype(BF16)))
    kk = k * k_k
    ss = _dot_exact_rhs(kk * kk, head_ones)
    kk = kk * lax.rsqrt(jnp.maximum(ss, 1e-12))
    k2 = k * (1.0 + (a - 1.0) * k_a)
    bonus = _dot_exact_rhs(r * k2 * r_k, head_ones) * v
    return log_w, a, kk, k2, bonus


def _group_norm(o, g, b, head_ones):
    inv = 1.0 / HEAD_DIM
    mean = _dot_exact_rhs(o, head_ones) * inv
    d = o - mean
    var = _dot_exact_rhs(d * d, head_ones) * inv
    return d * lax.rsqrt(var + GN_EPS) * g + b


def _shift_mix(cur, last_row, mu, first_row_mask):
    prev = jnp.where(first_row_mask, last_row, pltpu.roll(cur, 1, axis=0))
    return cur + (prev - cur) * mu


def _rwkv_prompt_kernel(
        p_ref, prev_ref, first_ref, mu_ref,
        w0_ref, a0_ref, kk_ref, ka_ref, rk_ref, lg_ref, lb_ref, wlb_ref, alb_ref,
        h0_ref, tri_ref, ones_ref,
        o_ref, hfin_ref, h_scr):
    i = pl.program_id(1)
    c = RWKV_CHUNK
    rows = c * RWKV_SUB
    w3 = 3 * RWKV_W

    @pl.when(i == 0)
    def _():
        h_scr[...] = h0_ref[...]

    first = lax.broadcasted_iota(jnp.int32, (rows, RWKV_COLS), 0) == 0
    last = jnp.where(i == 0, first_ref[...], prev_ref[7:8, :])
    xs = _shift_mix(p_ref[...], last, mu_ref[...], first)
    r, k, v = xs[:, :RWKV_W], xs[:, RWKV_W:2 * RWKV_W], xs[:, 2 * RWKV_W:w3]
    head_ones = ones_ref[...]
    log_w, a, kk, k2, bonus = _rwkv_token_math(
        r, k, v, xs[:, w3:w3 + LORA], xs[:, w3 + LORA:], w0_ref[...], a0_ref[...], kk_ref[...],
        ka_ref[...], rk_ref[...], wlb_ref[...], alb_ref[...], head_ones)

    subs = range(RWKV_SUB)
    lam = _dot_exact_lhs(tri_ref[...], log_w)
    lam_ends = [lam[(sc + 1) * c - 1:(sc + 1) * c, :] for sc in subs]
    lam_c = jnp.concatenate([jnp.broadcast_to(le, (c, RWKV_W)) for le in lam_ends], axis=0)
    w_in = jnp.exp(lam)
    w_ex = jnp.exp(lam - log_w)
    w_inv = jnp.exp(-lam)
    w_end = jnp.exp(lam_c - lam)
    w_c = [jnp.exp(le) for le in lam_ends]
    bvec = kk * a
    at = -kk * w_ex
    bt = bvec * w_inv
    kt = k2 * w_inv
    rt = r * w_in
    bh = bvec * w_end
    kh = k2 * w_end

    colid = lax.broadcasted_iota(jnp.int32, (c, c), 1)
    rowid = lax.broadcasted_iota(jnp.int32, (c, c), 0)
    strict = colid < rowid
    incl = colid <= rowid
    eye64 = (lax.broadcasted_iota(jnp.int32, (HEAD_DIM, HEAD_DIM), 0)
             == lax.broadcasted_iota(jnp.int32, (HEAD_DIM, HEAD_DIM), 1))

    heads = range(RWKV_HEADS)
    chains = [(sc, hh) for sc in subs for hh in heads]
    cut = lambda x: {(sc, hh): x[sc * c:(sc + 1) * c, hh * HEAD_DIM:(hh + 1) * HEAD_DIM]
                     for sc, hh in chains}
    at_h, bt_h, kt_h, rt_h, v_h, bh_h, kh_h = map(cut, (at, bt, kt, rt, v, bh, kh))
    each = lambda fn: {ch: fn(ch) for ch in chains}
    scores = each(lambda ch: _dot_f32(
        jnp.concatenate([at_h[ch], rt_h[ch]], axis=0),
        jnp.concatenate([bt_h[ch], kt_h[ch]], axis=0), NT))
    n = each(lambda ch: jnp.where(strict, scores[ch][:c, :c], 0.0))
    aak = each(lambda ch: jnp.where(strict, scores[ch][:c, c:], 0.0))
    mrb = each(lambda ch: jnp.where(incl, scores[ch][c:, :c], 0.0))
    mrk = each(lambda ch: jnp.where(incl, scores[ch][c:, c:], 0.0))
    rb, cb = rowid // INV_BASE, colid // INV_BASE
    dg = each(lambda ch: jnp.where(rb == cb, n[ch], 0.0))
    tm = each(lambda ch: jnp.where(colid == rowid, 1.0, dg[ch]))
    span = 1
    while 2 * span < INV_BASE:
        dg = each(lambda ch: _dot_bf16(dg[ch], dg[ch]))
        tm = each(lambda ch: tm[ch] + _dot_bf16(tm[ch], dg[ch]))
        span *= 2
    m = INV_BASE
    while m < c:
        off = (rowid // (2 * m) == colid // (2 * m)) & (rowid // m != colid // m)
        tn = each(lambda ch: _dot_bf16(tm[ch], jnp.where(off, n[ch], 0.0)))
        tm = each(lambda ch: tm[ch] + _dot_bf16(tn[ch], tm[ch]))
        m *= 2
    akv = each(lambda ch: _dot_f32(aak[ch], v_h[ch]))
    mkv = each(lambda ch: _dot_f32(mrk[ch], v_h[ch]))
    khv = each(lambda ch: _dot_f32(kh_h[ch], v_h[ch], TN))
    pq = each(lambda ch: _dot_f32(tm[ch], jnp.concatenate([at_h[ch], akv[ch]], axis=1)))
    mpq = each(lambda ch: _dot_f32(mrb[ch], pq[ch]))
    gj = each(lambda ch: _dot_f32(bh_h[ch], pq[ch], TN))
    p2 = each(lambda ch: rt_h[ch] + mpq[ch][:, :HEAD_DIM])
    q2 = each(lambda ch: mpq[ch][:, HEAD_DIM:] + mkv[ch])
    g = each(lambda ch: gj[ch][:, :HEAD_DIM] + jnp.where(
        eye64, w_c[ch[0]][:, ch[1] * HEAD_DIM:(ch[1] + 1) * HEAD_DIM], 0.0))
    jm = each(lambda ch: gj[ch][:, HEAD_DIM:] + khv[ch])

    state = [h_scr[hh] for hh in heads]
    outs = []
    for sc in subs:
        outs.append(jnp.concatenate(
            [_dot_f32(p2[(sc, hh)], state[hh]) + q2[(sc, hh)] for hh in heads], axis=1))
        state = [_dot_f32(g[(sc, hh)], state[hh]) + jm[(sc, hh)] for hh in heads]
    for hh in heads:
        h_scr[hh] = state[hh]

    o = jnp.concatenate(outs, axis=0)
    o_ref[...] = _group_norm(o, lg_ref[...], lb_ref[...], head_ones) + bonus

    @pl.when(i == pl.num_programs(1) - 1)
    def _():
        hfin_ref[...] = h_scr[...]


def _rwkv_prompt(p_rw, prev0, h0, vecs, wlb, alb, tri, ones):
    b, t, _ = p_rw.shape
    c = RWKV_CHUNK * RWKV_SUB
    assert t % c == 0, "sequence length must be a multiple of the tokens per step"
    sub = 8
    mu, w0, a0, k_k, k_a, r_k, lnx_g, lnx_b = vecs
    const = lambda bi, i: (0, 0)
    hvec = pl.BlockSpec((1, RWKV_W), const)
    lora = pl.BlockSpec((LORA, RWKV_W), const)
    state = pl.BlockSpec((None, RWKV_HEADS, HEAD_DIM, HEAD_DIM), lambda bi, i: (bi, 0, 0, 0))

    return pl.pallas_call(
        _rwkv_prompt_kernel,
        grid=(b, t // c),
        in_specs=[
            pl.BlockSpec((None, c, RWKV_COLS), lambda bi, i: (bi, i, 0)),
            pl.BlockSpec((None, sub, RWKV_COLS),
                         lambda bi, i: (bi, jnp.maximum(i * (c // sub) - 1, 0), 0)),
            pl.BlockSpec((None, 1, RWKV_COLS), lambda bi, i: (bi, 0, 0)),
            pl.BlockSpec((1, RWKV_COLS), const),
            hvec, hvec, hvec, hvec, hvec, hvec, hvec, lora, lora,
            state,
            pl.BlockSpec((c, c), const),
            pl.BlockSpec((RWKV_W, RWKV_W), const),
        ],
        out_specs=(
            pl.BlockSpec((None, c, RWKV_W), lambda bi, i: (bi, i, 0)),
            state,
        ),
        out_shape=(
            jax.ShapeDtypeStruct((b, t, RWKV_W), F32),
            jax.ShapeDtypeStruct((b, RWKV_HEADS, HEAD_DIM, HEAD_DIM), F32),
        ),
        scratch_shapes=[pltpu.VMEM((RWKV_HEADS, HEAD_DIM, HEAD_DIM), F32)],
        compiler_params=_params("parallel", "arbitrary"),
        name="rwkv_prompt",
    )(p_rw, p_rw, prev0, mu, w0, a0, k_k, k_a, r_k, lnx_g, lnx_b, wlb, alb, h0, tri, ones)


def _rwkv_sample_prep_kernel(p_ref, prev_ref, mu_ref, w0_ref, a0_ref, kk_ref, ka_ref, rk_ref,
                             wlb_ref, alb_ref, ones_ref,
                             r_ref, w_ref, k2_ref, v_ref, kkn_ref, b_ref, bonus_ref):
    pf = p_ref[...]
    xs = pf + (prev_ref[...] - pf) * mu_ref[...]
    w3 = 3 * RWKV_W
    r, k, v = xs[:, :RWKV_W], xs[:, RWKV_W:2 * RWKV_W], xs[:, 2 * RWKV_W:w3]
    log_w, a, kk, k2, bonus = _rwkv_token_math(
        r, k, v, xs[:, w3:w3 + LORA], xs[:, w3 + LORA:], w0_ref[...], a0_ref[...], kk_ref[...],
        ka_ref[...], rk_ref[...], wlb_ref[...], alb_ref[...], ones_ref[...])
    r_ref[...] = r
    w_ref[...] = jnp.exp(log_w)
    k2_ref[...] = k2
    v_ref[...] = v
    kkn_ref[...] = kk
    b_ref[...] = kk * a
    bonus_ref[...] = bonus


def _rwkv_sample_prep(p, prev, vecs, wlb, alb, ones):
    n = p.shape[0]
    mu, w0, a0, k_k, k_a, r_k = vecs
    return pl.pallas_call(
        _rwkv_sample_prep_kernel,
        out_shape=(jax.ShapeDtypeStruct((n, RWKV_W), F32),) * 7,
        name="rwkv_sample_prep",
    )(p, prev, mu, w0, a0, k_k, k_a, r_k, wlb, alb, ones)


def _rwkv_sample_step_kernel(s_ref, r_ref, w_ref, k_ref, kk_ref, b_ref, v_ref, bonus_ref,
                             g_ref, beta_ref, o_ref, snew_ref):
    eye = (lax.broadcasted_iota(jnp.int32, (HEAD_DIM, HEAD_DIM), 0)
           == lax.broadcasted_iota(jnp.int32, (HEAD_DIM, HEAD_DIM), 1))
    to_col = lambda row: jnp.sum(jnp.where(eye, row, 0.0), axis=3, keepdims=True)
    to_row = lambda col: jnp.sum(jnp.where(eye, col, 0.0), axis=2, keepdims=True)
    s = s_ref[...]
    s_kk = jnp.sum(s * kk_ref[...], axis=3, keepdims=True)
    s = s * w_ref[...] - s_kk * b_ref[...] + to_col(v_ref[...]) * k_ref[...]
    snew_ref[...] = s
    o = to_row(jnp.sum(s * r_ref[...], axis=3, keepdims=True))
    mean = jnp.mean(o, axis=3, keepdims=True)
    d = o - mean
    var = jnp.mean(d * d, axis=3, keepdims=True)
    o_ref[...] = d * lax.rsqrt(var + GN_EPS) * g_ref[...] + beta_ref[...] + bonus_ref[...]


def _rwkv_sample_step(s0, rows, g_row, beta_row, group):
    n = s0.shape[0]
    h, d = RWKV_HEADS, HEAD_DIM
    st = pl.BlockSpec((group, h, d, d), lambda bi: (bi, 0, 0, 0))
    rowspec = pl.BlockSpec((group, h, 1, d), lambda bi: (bi, 0, 0, 0))
    cconst = pl.BlockSpec((h, 1, d), lambda bi: (0, 0, 0))
    return pl.pallas_call(
        _rwkv_sample_step_kernel,
        grid=(n // group,),
        in_specs=[st] + [rowspec] * 7 + [cconst] * 2,
        out_specs=(rowspec, st),
        out_shape=(jax.ShapeDtypeStruct((n, h, 1, d), F32),
                   jax.ShapeDtypeStruct((n, h, d, d), F32)),
        compiler_params=_params("parallel"),
        name="rwkv_sample_step",
    )(s0, *rows, g_row, beta_row)


def _sb_prompt_kernel(bias_ref, q_ref, kT_ref, vT_ref, cs_ref, o_ref):
    pair = pl.program_id(1)
    qi = pl.program_id(2)
    blk = SB_BLOCK
    cs_mat = cs_ref[...]
    q_all = q_ref[...] * ATTN_SCALE
    valid = (lax.broadcasted_iota(jnp.int32, (blk, blk), 1)
             < lax.broadcasted_iota(jnp.int32, (blk, blk), 0))

    heads = range(2)
    rows = [slice(hh * HEAD_DIM, (hh + 1) * HEAD_DIM) for hh in heads]
    biases = [bias_ref[2 * pair + hh] for hh in heads]
    chains = [(hh, u) for u in range(SB_Q_SUB) for hh in heads]
    qs = {(hh, u): q_all[u * blk:(u + 1) * blk, rows[hh]].astype(BF16) for hh, u in chains}

    def run(blocks, items, state):
        starts = [pl.multiple_of(j * blk, blk) for j in blocks]
        carries, accs = dict(zip(chains, state[0])), dict(zip(chains, state[1]))
        k16, v16, tmp = {}, {}, {}

        def stage(s, it):
            bid, ch, diag = items[it]
            hh = ch[0]
            if s == 0:
                if (bid, hh) not in k16:
                    k16[bid, hh] = kT_ref[rows[hh], pl.ds(starts[bid], blk)].astype(BF16)
                tmp[it] = {"z": _dot(qs[ch], k16[bid, hh]) + biases[hh]}
            elif s == 1:
                t = tmp[it]
                sp, t["zs"] = _softplus_pair(t.pop("z"))
                if diag:
                    sp = jnp.where(valid, sp, 0.0)
                t["first"] = sp[:, 0:1]
                t["parts"] = sp.astype(BF16)
            elif s == 2:
                t = tmp[it]
                t["cs"] = _dot(t.pop("parts"), cs_mat)
            elif s == 3:
                t = tmp[it]
                cs = t.pop("cs")
                a = jnp.exp(t.pop("zs") - cs - carries[ch])
                if diag:
                    a = jnp.where(valid, a, 0.0)
                t["a"] = a.astype(BF16)
                carries[ch] = carries[ch] + (cs[:, 0:1] + t.pop("first"))
            else:
                if (bid, hh) not in v16:
                    v16[bid, hh] = vT_ref[rows[hh], pl.ds(starts[bid], blk)].astype(BF16)
                accs[ch] = accs[ch] + _dot(tmp.pop(it)["a"], v16[bid, hh], NT)

        n_stage = 5
        for wave in range(len(items) + n_stage - 1):
            for s in reversed(range(n_stage)):
                if 0 <= wave - s < len(items):
                    stage(s, wave - s)
        return (tuple(carries[ch] for ch in chains), tuple(accs[ch] for ch in chains))

    zeros_c = jnp.zeros((blk, 1), F32)
    zeros_o = jnp.zeros((blk, HEAD_DIM), F32)
    state = ((zeros_c,) * len(chains), (zeros_o,) * len(chains))
    top = SB_Q_SUB * qi
    order = list(reversed(range(SB_Q_SUB)))
    state = run([top + d for d in order],
                [(i, ch, ch[1] == d) for i, d in enumerate(order) for ch in chains if ch[1] >= d],
                state)

    def body(jj, st):
        first = top - 1 - jj * SB_Q_SUB
        return run([first - r for r in range(SB_Q_SUB)],
                   [(r, ch, False) for r in range(SB_Q_SUB) for ch in chains], st)

    state = lax.fori_loop(0, qi, body, state)
    accs = dict(zip(chains, state[1]))
    o_ref[...] = jnp.concatenate(
        [jnp.concatenate([accs[(hh, u)] for hh in heads], axis=1) for u in range(SB_Q_SUB)], axis=0)


def _sb_prompt(q, kT, vT, bias, cs_mat):
    b, t, _ = q.shape
    blk = SB_BLOCK * SB_Q_SUB
    assert t % blk == 0, "sequence length must be a multiple of the query rows per step"
    npair = SB_HEADS // 2
    kv = pl.BlockSpec((None, PAIR_W, t), lambda bi, p, i: (bi, p, 0))
    qo = pl.BlockSpec((None, blk, PAIR_W), lambda bi, p, i: (bi, i, p))
    return pl.pallas_call(
        _sb_prompt_kernel,
        grid=(b, npair, t // blk),
        in_specs=[
            pl.BlockSpec(memory_space=pltpu.SMEM),
            qo, kv, kv,
            pl.BlockSpec((SB_BLOCK, SB_BLOCK), lambda bi, p, i: (0, 0)),
        ],
        out_specs=qo,
        out_shape=jax.ShapeDtypeStruct((b, t, SB_W), F32),
        compiler_params=_params("parallel", "parallel", "arbitrary"),
        name="sb_prompt",
    )(bias, q, kT, vT, cs_mat)


def _sb_sample_kernel(pt_ref, *refs):
    npg = PAGES_PER_STEP
    k_refs = refs[:npg]
    v_refs = refs[npg:2 * npg]
    q_ref, bias_ref, sfx_ref, pages_ref, o_ref, acc_ref, carry_ref = refs[2 * npg:]
    j = pl.program_id(1)

    @pl.when(j == 0)
    def _():
        acc_ref[...] = jnp.zeros_like(acc_ref)
        carry_ref[...] = jnp.zeros_like(carry_ref)

    heads = range(SB_HEADS)
    nrow = SB_HEADS * npg
    pad = SB_SAMPLE_ROWS - nrow
    z = jnp.concatenate(
        [jnp.sum(k_refs[u][h] * q_ref[h], axis=0, keepdims=True) + bias_ref[h]
         for h in heads for u in range(npg)] + [jnp.zeros((pad, LANES), F32)], axis=0)
    sp, _ = _softplus_pair(z)
    sums = _dot_exact_rhs(sp, sfx_ref[...])
    suffix, tot = sums[:, :LANES], sums[:, LANES:]
    later = _dot_exact_lhs(pages_ref[...], tot)
    carry = jnp.concatenate(
        [jnp.broadcast_to(carry_ref[h], (npg, LANES)) for h in heads]
        + [jnp.zeros((pad, LANES), F32)], axis=0)
    a = jnp.exp(z - suffix - later - carry)
    for h in heads:
        acc = acc_ref[h]
        for u in range(npg):
            r = h * npg + u
            acc = acc + v_refs[u][h] * a[r:r + 1, :]
        acc_ref[h] = acc
        last = (h + 1) * npg - 1
        carry_ref[h] = carry_ref[h] + (later[last:last + 1, :] + tot[last:last + 1, :])

    @pl.when(j == pl.num_programs(1) - 1)
    def _():
        o_ref[...] = jnp.sum(acc_ref[...], axis=2)


def _sb_sample(page_table, cache_k, cache_v, q_bcast, bias_bcast):
    n, n_pages = page_table.shape
    npg = PAGES_PER_STEP
    h, d, pg = cache_k.shape[1:]
    last = n_pages - 1

    def page_spec(u):
        return pl.BlockSpec(
            (None, h, d, pg),
            lambda bi, j, pt: (pt[bi, last - (j * npg + u)], 0, 0, 0))

    nrow = SB_SAMPLE_ROWS
    assert h * npg <= nrow and pg == LANES
    tok = jnp.arange(pg)
    sfx_mat = jnp.concatenate(
        [(tok[:, None] >= tok[None, :]).astype(BF16), jnp.ones((pg, pg), BF16)], axis=1)
    r = jnp.arange(nrow)
    pages_mat = ((r[:, None] // npg == r[None, :] // npg) & (r[None, :] < r[:, None])
                 & (r[:, None] < h * npg)).astype(BF16)
    const2 = lambda bi, j, pt: (0, 0)

    grid_spec = pltpu.PrefetchScalarGridSpec(
        num_scalar_prefetch=1,
        grid=(n, n_pages // npg),
        in_specs=[page_spec(u) for u in range(npg)] * 2 + [
            pl.BlockSpec((None, h, d, pg), lambda bi, j, pt: (bi, 0, 0, 0)),
            pl.BlockSpec((h, 1, pg), lambda bi, j, pt: (0, 0, 0)),
            pl.BlockSpec((pg, 2 * pg), const2),
            pl.BlockSpec((nrow, nrow), const2),
        ],
        out_specs=pl.BlockSpec((None, h, d), lambda bi, j, pt: (bi, 0, 0)),
        scratch_shapes=[pltpu.VMEM((h, d, pg), F32), pltpu.VMEM((h, 1, pg), F32)],
    )
    return pl.pallas_call(
        _sb_sample_kernel,
        grid_spec=grid_spec,
        out_shape=jax.ShapeDtypeStruct((n, h, d), F32),
        compiler_params=_params("parallel", "arbitrary"),
        name="sb_sample",
    )(page_table, *([cache_k] * npg), *([cache_v] * npg), q_bcast, bias_bcast, sfx_mat, pages_mat)


def _head_ones(width):
    hid = jnp.arange(width) // HEAD_DIM
    return (hid[:, None] == hid[None, :]).astype(BF16)


def _rows_for(t, target):
    return target if t % target == 0 else t


def kernel(x_prompt, mem_prompt, x_sample, cache_sb_k, cache_sb_v, page_table, state_wkv, state_shift, cache_mem_k, cache_mem_v, norm_g, w_in, sb_bias, mu_shift, w0, w_lora_b, a0, a_lora_b, k_k, k_a, r_k, lnx_g, lnx_b, mem_norm_g, w_mem_k, w_mem_v, w_out, final_norm_g):
    depth = w_in.shape[0]
    assert depth == 1, "single-layer trunk"
    l = 0
    b_p, t_p, d = x_prompt.shape
    b_s = x_sample.shape[0]
    assert x_sample.shape[1] == 1
    n_mem = mem_prompt.shape[1]
    page = cache_sb_k.shape[2]

    w16 = w_in[l].astype(BF16)
    wkT = w_mem_k[l].T.astype(BF16)
    wvT = w_mem_v[l].T.astype(BF16)
    wo = w_out[l].astype(BF16)
    fg = final_norm_g.reshape(1, d)
    r_k_flat = r_k[l].reshape(1, RWKV_W)
    ones_all = _head_ones(RWKV_W)
    c = RWKV_CHUNK
    tok = jnp.arange(c * RWKV_SUB)
    tri = ((tok[None, :] <= tok[:, None]) & (tok[None, :] // c == tok[:, None] // c)).astype(BF16)
    blk = SB_BLOCK
    ar = jnp.arange(blk)
    cs_mat = (ar[:, None] > ar[None, :]).astype(BF16)

    mkT, mvT = _mem_kv(mem_prompt, mem_norm_g[l:l + 1], wkT, wvT)
    p_rw, q_sb, xq, gate, kT, vT = _in_proj(
        x_prompt, norm_g[l:l + 1], w16, _rows_for(t_p, 512), True)
    vecs = (mu_shift[l:l + 1], w0[l:l + 1], a0[l:l + 1], k_k[l:l + 1], k_a[l:l + 1],
            r_k_flat, lnx_g[l:l + 1], lnx_b[l:l + 1])
    o_rw, h_fin = _rwkv_prompt(
        p_rw, jnp.zeros((b_p, 1, RWKV_COLS), F32),
        jnp.zeros((b_p, RWKV_HEADS, HEAD_DIM, HEAD_DIM), F32),
        vecs, w_lora_b[l], a_lora_b[l], tri, ones_all)
    o_sb = _sb_prompt(q_sb, kT, vT, sb_bias[l], cs_mat)
    o_x = _xattn(xq, mkT, mvT, _rows_for(t_p, 512), 1)
    y_prompt = _out_proj(o_rw, o_sb, o_x, gate, x_prompt, wo, fg, _rows_for(t_p, 512))

    def tokens_major(xT, heads):
        return xT.reshape(1, xT.shape[0], heads, HEAD_DIM, xT.shape[2]).transpose(0, 1, 4, 2, 3)

    new_sb_k_p = tokens_major(kT, SB_HEADS)
    new_sb_v_p = tokens_major(vT, SB_HEADS)
    new_wkv_p = jnp.swapaxes(h_fin, -1, -2)[None]
    new_shift_p = p_rw[:, -1][None]
    new_mem_k_p = tokens_major(mkT, X_HEADS)
    new_mem_v_p = tokens_major(mvT, X_HEADS)

    xs2 = x_sample.reshape(1, b_s, d)
    p_rw_s, q_s, xq_s, gate_s, sbk_s, sbv_s = _in_proj(xs2, norm_g[l:l + 1], w16, b_s, False)
    p_rw_s = p_rw_s[0]
    svecs = (mu_shift[l:l + 1], w0[l:l + 1], a0[l:l + 1], k_k[l:l + 1], k_a[l:l + 1], r_k_flat)
    r_s, w_s, k2_s, v_s, kk_s, bv_s, bonus_s = _rwkv_sample_prep(
        p_rw_s, state_shift[l], svecs, w_lora_b[l], a_lora_b[l], ones_all)
    as_row = lambda a_: a_.reshape(b_s, RWKV_HEADS, 1, HEAD_DIM)
    o_rw_s, new_wkv_s = _rwkv_sample_step(
        state_wkv[l],
        [as_row(t_) for t_ in (r_s, w_s, k2_s, kk_s, bv_s, v_s, bonus_s)],
        lnx_g[l].reshape(RWKV_HEADS, 1, HEAD_DIM), lnx_b[l].reshape(RWKV_HEADS, 1, HEAD_DIM),
        SAMPLE_GROUP if b_s % SAMPLE_GROUP == 0 else 1)
    o_rw_s = o_rw_s.reshape(1, b_s, RWKV_W)

    ck = cache_sb_k[l].transpose(0, 2, 3, 1)
    cv = cache_sb_v[l].transpose(0, 2, 3, 1)
    q_b = jnp.broadcast_to(
        (q_s[0] * ATTN_SCALE).reshape(b_s, SB_HEADS, HEAD_DIM, 1), (b_s, SB_HEADS, HEAD_DIM, page))
    bias_b = jnp.broadcast_to(sb_bias[l].reshape(SB_HEADS, 1, 1), (SB_HEADS, 1, page))
    o_sb_s = _sb_sample(page_table, ck, cv, q_b, bias_b).reshape(1, b_s, SB_W)

    mk_s = cache_mem_k[l].transpose(0, 2, 3, 1).reshape(b_s, X_W, n_mem)
    mv_s = cache_mem_v[l].transpose(0, 2, 3, 1).reshape(b_s, X_W, n_mem)
    xq_rows = jnp.broadcast_to(xq_s[0][:, None, :], (b_s, 8, X_W))
    o_x_s = _xattn(xq_rows, mk_s, mv_s, 8, SAMPLE_GROUP if b_s % SAMPLE_GROUP == 0 else 1)[:, 0][None]
    y_sample = _out_proj(o_rw_s, o_sb_s, o_x_s, gate_s, xs2, wo, fg, b_s).reshape(b_s, 1, d)

    new_sb_k_s = sbk_s.reshape(1, b_s, 1, SB_HEADS, HEAD_DIM)
    new_sb_v_s = sbv_s.reshape(1, b_s, 1, SB_HEADS, HEAD_DIM)
    new_shift_s = p_rw_s[None]

    return (y_prompt, y_sample, new_sb_k_p, new_sb_v_p, new_wkv_p, new_shift_p,
            new_mem_k_p, new_mem_v_p, new_sb_k_s, new_sb_v_s, new_wkv_s[None], new_shift_s)
```

```python
import functools
import math

import jax
import jax.numpy as jnp
from jax import lax
from jax.experimental import pallas as pl
from jax.experimental.pallas import tpu as pltpu

F32 = jnp.float32
BF16 = jnp.bfloat16

HEAD_DIM = 64
RWKV_HEADS = 6
SB_HEADS = 6
X_HEADS = 4
RWKV_W = RWKV_HEADS * HEAD_DIM
SB_W = SB_HEADS * HEAD_DIM
X_W = X_HEADS * HEAD_DIM
LORA = 64
RWKV_COLS = 3 * RWKV_W + 2 * LORA
NORM_EPS = 1e-6
GN_EPS = 64e-5
DECAY_SCALE = math.exp(-0.5)
ATTN_SCALE = HEAD_DIM ** -0.5
LOG2E = math.log2(math.e)

LANES = 128
PAIR_W = 2 * HEAD_DIM
RWKV_CHUNK = 128
RWKV_SUB = 2
INV_BASE = 8
SB_BLOCK = 256
SB_Q_SUB = 4
PAGES_PER_STEP = 16
PAGE_RING = 3
SAMPLE_GROUP = 8
VMEM_LIMIT = 48 * 1024 * 1024

NN = (((1,), (0,)), ((), ()))
NT = (((1,), (1,)), ((), ()))
TN = (((0,), (0,)), ((), ()))


def _dot(a, b, dims=NN):
    return lax.dot_general(a, b, dims, preferred_element_type=F32)


def _split2(x):
    hi = x.astype(BF16)
    lo = (x - hi.astype(F32)).astype(BF16)
    return hi, lo


def _split3(x):
    hi = x.astype(BF16)
    r1 = x - hi.astype(F32)
    mid = r1.astype(BF16)
    lo = (r1 - mid.astype(F32)).astype(BF16)
    return hi, mid, lo


def _dot_f32(a, b, dims=NN):
    (ka,), (kb,) = dims[0]
    ah, al = _split2(a)
    bh, bl = _split2(b)
    return _dot(jnp.concatenate([ah, ah, al], axis=ka), jnp.concatenate([bh, bl, bh], axis=kb), dims)


def _dot_bf16(a, b, dims=NN):
    return _dot(a.astype(BF16), b.astype(BF16), dims)


def _dot_exact_rhs(a, b_bf16, dims=NN):
    hi, mid, lo = _split3(a)
    return _dot(hi, b_bf16, dims) + (_dot(mid, b_bf16, dims) + _dot(lo, b_bf16, dims))


def _dot_exact_lhs(a_bf16, b, dims=NN):
    hi, mid, lo = _split3(b)
    return _dot(a_bf16, hi, dims) + (_dot(a_bf16, mid, dims) + _dot(a_bf16, lo, dims))


def _sigmoid(x):
    return 1.0 / (1.0 + jnp.exp(-x))


def _softplus_pair(z):
    sp = jnp.maximum(z, 0.0) + jnp.log(1.0 + jnp.exp2(jnp.abs(z) * -LOG2E))
    return sp, z - sp


def _rmsnorm_rows(x, g):
    ms = jnp.mean(x * x, axis=-1, keepdims=True)
    return x * lax.rsqrt(ms + NORM_EPS) * g


def _params(*sem):
    return pltpu.CompilerParams(dimension_semantics=sem, vmem_limit_bytes=VMEM_LIMIT)


KV_COL0 = RWKV_COLS + SB_W
KV_COL1 = KV_COL0 + 2 * SB_W


def _in_proj_kernel(x_ref, g_ref, w_ref, prw_ref, q_ref, xq_ref, gate_ref, k_ref, v_ref,
                    *, kv_feature_major):
    h = _rmsnorm_rows(x_ref[...], g_ref[...]).astype(BF16)
    pa = _dot(h, w_ref[:, :KV_COL0])
    prw_ref[...] = pa[:, :RWKV_COLS]
    q_ref[...] = pa[:, RWKV_COLS:]
    kv = _dot(h, w_ref[:, KV_COL0:KV_COL1])
    if kv_feature_major:
        k_ref[...] = kv[:, :SB_W].T
        v_ref[...] = kv[:, SB_W:].T
    else:
        k_ref[...] = kv[:, :SB_W]
        v_ref[...] = kv[:, SB_W:]
    pb = _dot(h, w_ref[:, KV_COL1:])
    xq_ref[...] = pb[:, :X_W]
    gate_ref[...] = pb[:, X_W:]


def _in_proj(x, norm_g, w16, rows, kv_feature_major):
    b, t, d = x.shape
    n_in = w16.shape[1]
    n_gate = n_in - KV_COL1 - X_W
    row = lambda bi, i: (bi, i, 0)
    col = lambda bi, i: (bi, 0, i)
    const = lambda bi, i: (0, 0)
    if kv_feature_major:
        kv_shape = jax.ShapeDtypeStruct((b, SB_W, t), F32)
        kv_spec = pl.BlockSpec((None, SB_W, rows), col)
    else:
        kv_shape = jax.ShapeDtypeStruct((b, t, SB_W), F32)
        kv_spec = pl.BlockSpec((None, rows, SB_W), row)
    out_shape = (
        jax.ShapeDtypeStruct((b, t, RWKV_COLS), F32),
        jax.ShapeDtypeStruct((b, t, SB_W), F32),
        jax.ShapeDtypeStruct((b, t, X_W), F32),
        jax.ShapeDtypeStruct((b, t, n_gate), F32),
        kv_shape, kv_shape,
    )
    return pl.pallas_call(
        functools.partial(_in_proj_kernel, kv_feature_major=kv_feature_major),
        grid=(b, t // rows),
        in_specs=[
            pl.BlockSpec((None, rows, d), row),
            pl.BlockSpec((1, d), const),
            pl.BlockSpec((d, n_in), const),
        ],
        out_specs=(
            pl.BlockSpec((None, rows, RWKV_COLS), row),
            pl.BlockSpec((None, rows, SB_W), row),
            pl.BlockSpec((None, rows, X_W), row),
            pl.BlockSpec((None, rows, n_gate), row),
            kv_spec, kv_spec,
        ),
        out_shape=out_shape,
        compiler_params=_params("parallel", "parallel"),
        name="in_proj",
    )(x, norm_g, w16)


def _mem_kv_kernel(mem_ref, g_ref, wkT_ref, wvT_ref, kT_ref, vT_ref):
    h = _rmsnorm_rows(mem_ref[...], g_ref[...]).astype(BF16)
    kT_ref[...] = _dot(wkT_ref[...], h, NT)
    vT_ref[...] = _dot(wvT_ref[...], h, NT)


def _mem_kv(mem, g, wkT, wvT):
    b, m, d = mem.shape
    const = lambda bi: (0, 0)
    blk = lambda bi: (bi, 0, 0)
    return pl.pallas_call(
        _mem_kv_kernel,
        grid=(b,),
        in_specs=[
            pl.BlockSpec((None, m, d), blk),
            pl.BlockSpec((1, d), const),
            pl.BlockSpec((X_W, d), const),
            pl.BlockSpec((X_W, d), const),
        ],
        out_specs=(pl.BlockSpec((None, X_W, m), blk), pl.BlockSpec((None, X_W, m), blk)),
        out_shape=(jax.ShapeDtypeStruct((b, X_W, m), F32),) * 2,
        compiler_params=_params("parallel"),
        name="mem_kv",
    )(mem, g, wkT, wvT)


def _xattn_kernel(xq_ref, kT_ref, vT_ref, o_ref):
    heads = range(X_HEADS)
    sl = [slice(h * HEAD_DIM, (h + 1) * HEAD_DIM) for h in heads]
    for g in range(xq_ref.shape[0]):
        s = [_dot((xq_ref[g, :, sl[h]] * ATTN_SCALE).astype(BF16), kT_ref[g, sl[h], :].astype(BF16))
             for h in heads]
        e = [jnp.exp(s[h] - jnp.max(s[h], axis=-1, keepdims=True)) for h in heads]
        l = [jnp.sum(e[h], axis=-1, keepdims=True) for h in heads]
        o = [_dot(e[h].astype(BF16), vT_ref[g, sl[h], :].astype(BF16), NT) / l[h] for h in heads]
        o_ref[g] = jnp.concatenate(o, axis=1)


def _xattn(xq, kT, vT, rows, group):
    b, t, _ = xq.shape
    m = kT.shape[-1]
    row = lambda bi, i: (bi, i, 0)
    full = lambda bi, i: (bi, 0, 0)
    return pl.pallas_call(
        _xattn_kernel,
        grid=(b // group, t // rows),
        in_specs=[
            pl.BlockSpec((group, rows, X_W), row),
            pl.BlockSpec((group, X_W, m), full),
            pl.BlockSpec((group, X_W, m), full),
        ],
        out_specs=pl.BlockSpec((group, rows, X_W), row),
        out_shape=jax.ShapeDtypeStruct((b, t, X_W), F32),
        compiler_params=_params("parallel", "parallel"),
        name="xattn",
    )(xq, kT, vT)


def _out_proj_kernel(orw_ref, osb_ref, ox_ref, gate_ref, x_ref, w_ref, fg_ref, y_ref):
    g = gate_ref[...]
    sg = g * _sigmoid(g)
    a0, a1 = RWKV_W, RWKV_W + SB_W
    acc = _dot((orw_ref[...] * sg[:, :a0]).astype(BF16), w_ref[:a0, :])
    acc += _dot((osb_ref[...] * sg[:, a0:a1]).astype(BF16), w_ref[a0:a1, :])
    acc += _dot((ox_ref[...] * sg[:, a1:]).astype(BF16), w_ref[a1:, :])
    y_ref[...] = _rmsnorm_rows(x_ref[...] + acc, fg_ref[...])


def _out_proj(o_rw, o_sb, o_x, gate, x, w_out, fg, rows):
    b, t, d = x.shape
    dm = w_out.shape[0]
    row = lambda bi, i: (bi, i, 0)
    const = lambda bi, i: (0, 0)
    return pl.pallas_call(
        _out_proj_kernel,
        grid=(b, t // rows),
        in_specs=[
            pl.BlockSpec((None, rows, RWKV_W), row),
            pl.BlockSpec((None, rows, SB_W), row),
            pl.BlockSpec((None, rows, X_W), row),
            pl.BlockSpec((None, rows, dm), row),
            pl.BlockSpec((None, rows, d), row),
            pl.BlockSpec((dm, d), const),
            pl.BlockSpec((1, d), const),
        ],
        out_specs=pl.BlockSpec((None, rows, d), row),
        out_shape=jax.ShapeDtypeStruct((b, t, d), F32),
        compiler_params=_params("parallel", "parallel"),
        name="out_proj",
    )(o_rw, o_sb, o_x, gate, x, w_out, fg)


def _rwkv_token_math(r, k, v, xw, xa, w0, a0, k_k, k_a, r_k, wlb, alb, head_ones):
    log_w = -DECAY_SCALE * _sigmoid(w0 + _dot(jnp.tanh(xw).astype(BF16), wlb.astype(BF16)))
    a = _sigmoid(a0 + _dot(xa.astype(BF16), alb.astype(BF16)))
    kk = k * k_k
    ss = _dot_exact_rhs(kk * kk, head_ones)
    kk = kk * lax.rsqrt(jnp.maximum(ss, 1e-12))
    k2 = k * (1.0 + (a - 1.0) * k_a)
    bonus = _dot_exact_rhs(r * k2 * r_k, head_ones) * v
    return log_w, a, kk, k2, bonus


def _group_norm(o, g, b, head_ones):
    inv = 1.0 / HEAD_DIM
    mean = _dot_exact_rhs(o, head_ones) * inv
    d = o - mean
    var = _dot_exact_rhs(d * d, head_ones) * inv
    return d * lax.rsqrt(var + GN_EPS) * g + b


def _shift_mix(cur, last_row, mu, first_row_mask):
    prev = jnp.where(first_row_mask, last_row, pltpu.roll(cur, 1, axis=0))
    return cur + (prev - cur) * mu


def _rwkv_prompt_kernel(
        p_ref, prev_ref, first_ref, mu_ref,
        w0_ref, a0_ref, kk_ref, ka_ref, rk_ref, lg_ref, lb_ref, wlb_ref, alb_ref,
        h0_ref, tri_ref, ones_ref,
        o_ref, hfin_ref, h_scr, *, n_sub):
    i = pl.program_id(1)
    c = RWKV_CHUNK
    rows = c * n_sub
    w3 = 3 * RWKV_W

    @pl.when(i == 0)
    def _():
        h_scr[...] = h0_ref[...]

    first = lax.broadcasted_iota(jnp.int32, (rows, RWKV_COLS), 0) == 0
    last = jnp.where(i == 0, first_ref[...], prev_ref[7:8, :])
    xs = _shift_mix(p_ref[...], last, mu_ref[...], first)
    r, k, v = xs[:, :RWKV_W], xs[:, RWKV_W:2 * RWKV_W], xs[:, 2 * RWKV_W:w3]
    head_ones = ones_ref[...]
    log_w, a, kk, k2, bonus = _rwkv_token_math(
        r, k, v, xs[:, w3:w3 + LORA], xs[:, w3 + LORA:], w0_ref[...], a0_ref[...], kk_ref[...],
        ka_ref[...], rk_ref[...], wlb_ref[...], alb_ref[...], head_ones)

    subs = range(n_sub)
    lam = _dot_exact_lhs(tri_ref[...], log_w)
    lam_ends = [lam[(sc + 1) * c - 1:(sc + 1) * c, :] for sc in subs]
    lam_c = jnp.concatenate([jnp.broadcast_to(le, (c, RWKV_W)) for le in lam_ends], axis=0)
    w_in = jnp.exp(lam)
    w_ex = jnp.exp(lam - log_w)
    w_inv = jnp.exp(-lam)
    w_end = jnp.exp(lam_c - lam)
    w_c = [jnp.exp(le) for le in lam_ends]
    bvec = kk * a
    at = -kk * w_ex
    bt = bvec * w_inv
    kt = k2 * w_inv
    rt = r * w_in
    bh = bvec * w_end
    kh = k2 * w_end

    colid = lax.broadcasted_iota(jnp.int32, (c, c), 1)
    rowid = lax.broadcasted_iota(jnp.int32, (c, c), 0)
    strict = colid < rowid
    incl = colid <= rowid
    eye64 = (lax.broadcasted_iota(jnp.int32, (HEAD_DIM, HEAD_DIM), 0)
             == lax.broadcasted_iota(jnp.int32, (HEAD_DIM, HEAD_DIM), 1))

    heads = range(RWKV_HEADS)
    chains = [(sc, hh) for sc in subs for hh in heads]
    cut = lambda x: {(sc, hh): x[sc * c:(sc + 1) * c, hh * HEAD_DIM:(hh + 1) * HEAD_DIM]
                     for sc, hh in chains}
    at_h, bt_h, kt_h, rt_h, v_h, bh_h, kh_h = map(cut, (at, bt, kt, rt, v, bh, kh))
    each = lambda fn: {ch: fn(ch) for ch in chains}
    scores = each(lambda ch: _dot_f32(
        jnp.concatenate([at_h[ch], rt_h[ch]], axis=0),
        jnp.concatenate([bt_h[ch], kt_h[ch]], axis=0), NT))
    n = each(lambda ch: jnp.where(strict, scores[ch][:c, :c], 0.0))
    aak = each(lambda ch: jnp.where(strict, scores[ch][:c, c:], 0.0))
    mrb = each(lambda ch: jnp.where(incl, scores[ch][c:, :c], 0.0))
    mrk = each(lambda ch: jnp.where(incl, scores[ch][c:, c:], 0.0))
    rb, cb = rowid // INV_BASE, colid // INV_BASE
    dg = each(lambda ch: jnp.where(rb == cb, n[ch], 0.0))
    tm = each(lambda ch: jnp.where(colid == rowid, 1.0, dg[ch]))
    span = 1
    while 2 * span < INV_BASE:
        dg = each(lambda ch: _dot_bf16(dg[ch], dg[ch]))
        tm = each(lambda ch: tm[ch] + _dot_bf16(tm[ch], dg[ch]))
        span *= 2
    m = INV_BASE
    while m < c:
        off = (rowid // (2 * m) == colid // (2 * m)) & (rowid // m != colid // m)
        tn = each(lambda ch: _dot_bf16(tm[ch], jnp.where(off, n[ch], 0.0)))
        tm = each(lambda ch: tm[ch] + _dot_bf16(tn[ch], tm[ch]))
        m *= 2
    akv = each(lambda ch: _dot_f32(aak[ch], v_h[ch]))
    mkv = each(lambda ch: _dot_f32(mrk[ch], v_h[ch]))
    khv = each(lambda ch: _dot_f32(kh_h[ch], v_h[ch], TN))
    pq = each(lambda ch: _dot_f32(tm[ch], jnp.concatenate([at_h[ch], akv[ch]], axis=1)))
    mpq = each(lambda ch: _dot_f32(mrb[ch], pq[ch]))
    gj = each(lambda ch: _dot_f32(bh_h[ch], pq[ch], TN))
    p2 = each(lambda ch: rt_h[ch] + mpq[ch][:, :HEAD_DIM])
    q2 = each(lambda ch: mpq[ch][:, HEAD_DIM:] + mkv[ch])
    g = each(lambda ch: gj[ch][:, :HEAD_DIM] + jnp.where(
        eye64, w_c[ch[0]][:, ch[1] * HEAD_DIM:(ch[1] + 1) * HEAD_DIM], 0.0))
    jm = each(lambda ch: gj[ch][:, HEAD_DIM:] + khv[ch])

    state = [h_scr[hh] for hh in heads]
    outs = []
    for sc in subs:
        outs.append(jnp.concatenate(
            [_dot_f32(p2[(sc, hh)], state[hh]) + q2[(sc, hh)] for hh in heads], axis=1))
        state = [_dot_f32(g[(sc, hh)], state[hh]) + jm[(sc, hh)] for hh in heads]
    for hh in heads:
        h_scr[hh] = state[hh]

    o = jnp.concatenate(outs, axis=0)
    o_ref[...] = _group_norm(o, lg_ref[...], lb_ref[...], head_ones) + bonus

    @pl.when(i == pl.num_programs(1) - 1)
    def _():
        hfin_ref[...] = h_scr[...]


def _rwkv_prompt_specs(b, t, n_sub, index):
    c = RWKV_CHUNK * n_sub
    assert t % c == 0, "sequence length must be a multiple of the tokens per step"
    sub = 8
    const = index(lambda bi, i: (0, 0))
    hvec = pl.BlockSpec((1, RWKV_W), const)
    lora = pl.BlockSpec((LORA, RWKV_W), const)
    state = pl.BlockSpec((None, RWKV_HEADS, HEAD_DIM, HEAD_DIM), index(lambda bi, i: (bi, 0, 0, 0)))
    in_specs = [
        pl.BlockSpec((None, c, RWKV_COLS), index(lambda bi, i: (bi, i, 0))),
        pl.BlockSpec((None, sub, RWKV_COLS),
                     index(lambda bi, i: (bi, jnp.maximum(i * (c // sub) - 1, 0), 0))),
        pl.BlockSpec((None, 1, RWKV_COLS), index(lambda bi, i: (bi, 0, 0))),
        pl.BlockSpec((1, RWKV_COLS), const),
        hvec, hvec, hvec, hvec, hvec, hvec, hvec, lora, lora,
        state,
        pl.BlockSpec((c, c), const),
        pl.BlockSpec((RWKV_W, RWKV_W), const),
    ]
    out_specs = [pl.BlockSpec((None, c, RWKV_W), index(lambda bi, i: (bi, i, 0))), state]
    out_shape = [jax.ShapeDtypeStruct((b, t, RWKV_W), F32),
                 jax.ShapeDtypeStruct((b, RWKV_HEADS, HEAD_DIM, HEAD_DIM), F32)]
    scratch = [pltpu.VMEM((RWKV_HEADS, HEAD_DIM, HEAD_DIM), F32)]
    return in_specs, out_specs, out_shape, scratch


def _chunk_tri(n_sub):
    c = RWKV_CHUNK
    tok = jnp.arange(c * n_sub)
    return ((tok[None, :] <= tok[:, None]) & (tok[None, :] // c == tok[:, None] // c)).astype(BF16)


def _rwkv_prompt(p_rw, prev0, h0, vecs, wlb, alb, ones):
    b, t, _ = p_rw.shape
    mu, w0, a0, k_k, k_a, r_k, lnx_g, lnx_b = vecs
    in_specs, out_specs, out_shape, scratch = _rwkv_prompt_specs(b, t, RWKV_SUB, lambda fn: fn)
    return pl.pallas_call(
        functools.partial(_rwkv_prompt_kernel, n_sub=RWKV_SUB),
        grid=(b, t // (RWKV_CHUNK * RWKV_SUB)),
        in_specs=in_specs,
        out_specs=tuple(out_specs),
        out_shape=tuple(out_shape),
        scratch_shapes=scratch,
        compiler_params=_params("parallel", "arbitrary"),
        name="rwkv_prompt",
    )(p_rw, p_rw, prev0, mu, w0, a0, k_k, k_a, r_k, lnx_g, lnx_b, wlb, alb, h0,
      _chunk_tri(RWKV_SUB), ones)


def _rwkv_sample_prep_kernel(p_ref, prev_ref, mu_ref, w0_ref, a0_ref, kk_ref, ka_ref, rk_ref,
                             wlb_ref, alb_ref, ones_ref,
                             r_ref, w_ref, k2_ref, v_ref, kkn_ref, b_ref, bonus_ref):
    pf = p_ref[...]
    xs = pf + (prev_ref[...] - pf) * mu_ref[...]
    w3 = 3 * RWKV_W
    r, k, v = xs[:, :RWKV_W], xs[:, RWKV_W:2 * RWKV_W], xs[:, 2 * RWKV_W:w3]
    log_w, a, kk, k2, bonus = _rwkv_token_math(
        r, k, v, xs[:, w3:w3 + LORA], xs[:, w3 + LORA:], w0_ref[...], a0_ref[...], kk_ref[...],
        ka_ref[...], rk_ref[...], wlb_ref[...], alb_ref[...], ones_ref[...])
    r_ref[...] = r
    w_ref[...] = jnp.exp(log_w)
    k2_ref[...] = k2
    v_ref[...] = v
    kkn_ref[...] = kk
    b_ref[...] = kk * a
    bonus_ref[...] = bonus


def _rwkv_sample_prep(p, prev, vecs, wlb, alb, ones):
    n = p.shape[0]
    mu, w0, a0, k_k, k_a, r_k = vecs
    return pl.pallas_call(
        _rwkv_sample_prep_kernel,
        out_shape=(jax.ShapeDtypeStruct((n, RWKV_W), F32),) * 7,
        name="rwkv_sample_prep",
    )(p, prev, mu, w0, a0, k_k, k_a, r_k, wlb, alb, ones)


def _rwkv_sample_step_kernel(s_ref, r_ref, w_ref, k_ref, kk_ref, b_ref, v_ref, bonus_ref,
                             g_ref, beta_ref, o_ref, snew_ref):
    eye = (lax.broadcasted_iota(jnp.int32, (HEAD_DIM, HEAD_DIM), 0)
           == lax.broadcasted_iota(jnp.int32, (HEAD_DIM, HEAD_DIM), 1))
    to_col = lambda row: jnp.sum(jnp.where(eye, row, 0.0), axis=3, keepdims=True)
    to_row = lambda col: jnp.sum(jnp.where(eye, col, 0.0), axis=2, keepdims=True)
    s = s_ref[...]
    s_kk = jnp.sum(s * kk_ref[...], axis=3, keepdims=True)
    s = s * w_ref[...] - s_kk * b_ref[...] + to_col(v_ref[...]) * k_ref[...]
    snew_ref[...] = s
    o = to_row(jnp.sum(s * r_ref[...], axis=3, keepdims=True))
    mean = jnp.mean(o, axis=3, keepdims=True)
    d = o - mean
    var = jnp.mean(d * d, axis=3, keepdims=True)
    o_ref[...] = d * lax.rsqrt(var + GN_EPS) * g_ref[...] + beta_ref[...] + bonus_ref[...]


def _rwkv_sample_step(s0, rows, g_row, beta_row, group):
    n = s0.shape[0]
    h, d = RWKV_HEADS, HEAD_DIM
    st = pl.BlockSpec((group, h, d, d), lambda bi: (bi, 0, 0, 0))
    rowspec = pl.BlockSpec((group, h, 1, d), lambda bi: (bi, 0, 0, 0))
    cconst = pl.BlockSpec((h, 1, d), lambda bi: (0, 0, 0))
    return pl.pallas_call(
        _rwkv_sample_step_kernel,
        grid=(n // group,),
        in_specs=[st] + [rowspec] * 7 + [cconst] * 2,
        out_specs=(rowspec, st),
        out_shape=(jax.ShapeDtypeStruct((n, h, 1, d), F32),
                   jax.ShapeDtypeStruct((n, h, d, d), F32)),
        compiler_params=_params("parallel"),
        name="rwkv_sample_step",
    )(s0, *rows, g_row, beta_row)


def _sb_prompt_kernel(bias_ref, q_ref, kT_ref, vT_ref, cs_ref, o_ref):
    pair = pl.program_id(1)
    qi = pl.program_id(2)
    blk = SB_BLOCK
    cs_mat = cs_ref[...]
    q_all = q_ref[...] * ATTN_SCALE
    valid = (lax.broadcasted_iota(jnp.int32, (blk, blk), 1)
             < lax.broadcasted_iota(jnp.int32, (blk, blk), 0))

    heads = range(2)
    rows = [slice(hh * HEAD_DIM, (hh + 1) * HEAD_DIM) for hh in heads]
    biases = [bias_ref[2 * pair + hh] for hh in heads]
    chains = [(hh, u) for u in range(SB_Q_SUB) for hh in heads]
    qs = {(hh, u): q_all[u * blk:(u + 1) * blk, rows[hh]].astype(BF16) for hh, u in chains}

    def run(blocks, items, state):
        starts = [pl.multiple_of(j * blk, blk) for j in blocks]
        carries, accs = dict(zip(chains, state[0])), dict(zip(chains, state[1]))
        k16, v16, tmp = {}, {}, {}

        def stage(s, it):
            bid, ch, diag = items[it]
            hh = ch[0]
            if s == 0:
                if (bid, hh) not in k16:
                    k16[bid, hh] = kT_ref[rows[hh], pl.ds(starts[bid], blk)].astype(BF16)
                tmp[it] = {"z": _dot(qs[ch], k16[bid, hh]) + biases[hh]}
            elif s == 1:
                t = tmp[it]
                sp, t["zs"] = _softplus_pair(t.pop("z"))
                if diag:
                    sp = jnp.where(valid, sp, 0.0)
                t["first"] = sp[:, 0:1]
                t["parts"] = sp.astype(BF16)
            elif s == 2:
                t = tmp[it]
                t["cs"] = _dot(t.pop("parts"), cs_mat)
            elif s == 3:
                t = tmp[it]
                cs = t.pop("cs")
                a = jnp.exp(t.pop("zs") - cs - carries[ch])
                if diag:
                    a = jnp.where(valid, a, 0.0)
                t["a"] = a.astype(BF16)
                carries[ch] = carries[ch] + (cs[:, 0:1] + t.pop("first"))
            else:
                if (bid, hh) not in v16:
                    v16[bid, hh] = vT_ref[rows[hh], pl.ds(starts[bid], blk)].astype(BF16)
                accs[ch] = accs[ch] + _dot(tmp.pop(it)["a"], v16[bid, hh], NT)

        n_stage = 5
        for wave in range(len(items) + n_stage - 1):
            for s in reversed(range(n_stage)):
                if 0 <= wave - s < len(items):
                    stage(s, wave - s)
        return (tuple(carries[ch] for ch in chains), tuple(accs[ch] for ch in chains))

    zeros_c = jnp.zeros((blk, 1), F32)
    zeros_o = jnp.zeros((blk, HEAD_DIM), F32)
    state = ((zeros_c,) * len(chains), (zeros_o,) * len(chains))
    top = SB_Q_SUB * qi
    order = list(reversed(range(SB_Q_SUB)))
    state = run([top + d for d in order],
                [(i, ch, ch[1] == d) for i, d in enumerate(order) for ch in chains if ch[1] >= d],
                state)

    def body(jj, st):
        first = top - 1 - jj * SB_Q_SUB
        return run([first - r for r in range(SB_Q_SUB)],
                   [(r, ch, False) for r in range(SB_Q_SUB) for ch in chains], st)

    state = lax.fori_loop(0, qi, body, state)
    accs = dict(zip(chains, state[1]))
    o_ref[...] = jnp.concatenate(
        [jnp.concatenate([accs[(hh, u)] for hh in heads], axis=1) for u in range(SB_Q_SUB)], axis=0)


def _sb_prompt(q, kT, vT, bias, cs_mat):
    b, t, _ = q.shape
    blk = SB_BLOCK * SB_Q_SUB
    assert t % blk == 0, "sequence length must be a multiple of the query rows per step"
    npair = SB_HEADS // 2
    kv = pl.BlockSpec((None, PAIR_W, t), lambda bi, p, i: (bi, p, 0))
    qo = pl.BlockSpec((None, blk, PAIR_W), lambda bi, p, i: (bi, i, p))
    return pl.pallas_call(
        _sb_prompt_kernel,
        grid=(b, npair, t // blk),
        in_specs=[
            pl.BlockSpec(memory_space=pltpu.SMEM),
            qo, kv, kv,
            pl.BlockSpec((SB_BLOCK, SB_BLOCK), lambda bi, p, i: (0, 0)),
        ],
        out_specs=qo,
        out_shape=jax.ShapeDtypeStruct((b, t, SB_W), F32),
        compiler_params=_params("parallel", "parallel", "arbitrary"),
        name="sb_prompt",
    )(bias, q, kT, vT, cs_mat)


def _sb_sample_pieces(k_refs, v_refs, q_ref, bias_ref, sfx_ref, pages_ref, o_ref, acc_ref, carry_ref,
                      first_step, last_step):
    npg = len(k_refs)
    heads = range(SB_HEADS)
    nrow = SB_HEADS * npg
    pad = -nrow % LANES
    st = {}

    def init():
        @pl.when(first_step)
        def _():
            acc_ref[...] = jnp.zeros_like(acc_ref)
            carry_ref[...] = jnp.zeros_like(carry_ref)

    def score_row(h, u):
        st["z", h, u] = jnp.sum(k_refs[u][h] * q_ref[h], axis=0, keepdims=True) + bias_ref[h]

    def weights():
        zpad = [jnp.zeros((pad, LANES), F32)] if pad else []
        z = jnp.concatenate([st.pop(("z", h, u)) for h in heads for u in range(npg)] + zpad, axis=0)
        sp, _ = _softplus_pair(z)
        sums = _dot_exact_rhs(sp, sfx_ref[...])
        suffix, tot = sums[:, :LANES], sums[:, LANES:]
        later = _dot_exact_lhs(pages_ref[...], tot)
        carry = jnp.concatenate(
            [jnp.broadcast_to(carry_ref[h], (npg, LANES)) for h in heads] + zpad, axis=0)
        st["a"] = jnp.exp(z - suffix - later - carry)
        st["done"] = later + tot

    def value_row(h, u):
        r = h * npg + u
        acc = acc_ref[h] if u == 0 else st.pop(("acc", h))
        acc = acc + v_refs[u][h] * st["a"][r:r + 1, :]
        if u < npg - 1:
            st["acc", h] = acc
        else:
            acc_ref[h] = acc
            carry_ref[h] = carry_ref[h] + st["done"][r:r + 1, :]

    def finish():
        @pl.when(last_step)
        def _():
            o_ref[...] = jnp.sum(acc_ref[...], axis=2)

    pairs = [(h, u) for h in heads for u in range(npg)]
    return ([init] + [functools.partial(score_row, h, u) for h, u in pairs] + [weights]
            + [functools.partial(value_row, h, u) for h, u in pairs] + [finish])


def _sb_sample_kernel(pt_ref, k_hbm, v_hbm, q_ref, bias_ref, sfx_ref, pages_ref, o_ref,
                      kbuf, vbuf, sem, acc_ref, carry_ref, *, npg, depth):
    n_seq, n_steps = pl.num_programs(0), pl.num_programs(1)
    j = pl.program_id(1)
    g = pl.program_id(0) * n_steps + j
    total = n_seq * n_steps
    last_page = n_steps * npg - 1

    def page_copies(step, u, page):
        slot = step % depth
        return (pltpu.make_async_copy(k_hbm.at[page], kbuf.at[slot, u], sem.at[slot, 0]),
                pltpu.make_async_copy(v_hbm.at[page], vbuf.at[slot, u], sem.at[slot, 1]))

    def start_step(step):
        seq, sj = step // n_steps, step % n_steps
        for u in range(npg):
            for cp in page_copies(step, u, pt_ref[seq, last_page - (sj * npg + u)]):
                cp.start()

    @pl.when(g == 0)
    def _():
        for s in range(depth - 1):
            @pl.when(s < total)
            def _():
                start_step(jnp.int32(s))

    ahead = g + (depth - 1)

    @pl.when(ahead < total)
    def _():
        start_step(ahead)

    for u in range(npg):
        for cp in page_copies(g, u, 0):
            cp.wait()

    slot = g % depth
    k_refs = [kbuf.at[slot, u] for u in range(npg)]
    v_refs = [vbuf.at[slot, u] for u in range(npg)]
    for piece in _sb_sample_pieces(k_refs, v_refs, q_ref, bias_ref, sfx_ref, pages_ref, o_ref,
                                   acc_ref, carry_ref, j == 0, j == n_steps - 1):
        piece()


def _sb_sample_consts(h, npg, pg):
    assert pg == LANES
    nrow = h * npg + (-(h * npg) % LANES)
    tok = jnp.arange(pg)
    sfx_mat = jnp.concatenate(
        [(tok[:, None] >= tok[None, :]).astype(BF16), jnp.ones((pg, pg), BF16)], axis=1)
    r = jnp.arange(nrow)
    pages_mat = ((r[:, None] // npg == r[None, :] // npg) & (r[None, :] < r[:, None])
                 & (r[:, None] < h * npg)).astype(BF16)
    return sfx_mat, pages_mat


def _sb_sample(page_table, cache_k, cache_v, q_bcast, bias_bcast):
    n, n_pages = page_table.shape
    npg = PAGES_PER_STEP
    depth = PAGE_RING
    h, d, pg = cache_k.shape[1:]
    assert n_pages % npg == 0

    sfx_mat, pages_mat = _sb_sample_consts(h, npg, pg)
    const2 = lambda bi, j, pt: (0, 0)
    grid_spec = pltpu.PrefetchScalarGridSpec(
        num_scalar_prefetch=1,
        grid=(n, n_pages // npg),
        in_specs=[
            pl.BlockSpec(memory_space=pl.ANY),
            pl.BlockSpec(memory_space=pl.ANY),
            pl.BlockSpec((None, h, d, pg), lambda bi, j, pt: (bi, 0, 0, 0)),
            pl.BlockSpec((h, 1, pg), lambda bi, j, pt: (0, 0, 0)),
            pl.BlockSpec(sfx_mat.shape, const2),
            pl.BlockSpec(pages_mat.shape, const2),
        ],
        out_specs=pl.BlockSpec((None, h, d), lambda bi, j, pt: (bi, 0, 0)),
        scratch_shapes=[
            pltpu.VMEM((depth, npg, h, d, pg), F32),
            pltpu.VMEM((depth, npg, h, d, pg), F32),
            pltpu.SemaphoreType.DMA((depth, 2)),
            pltpu.VMEM((h, d, pg), F32),
            pltpu.VMEM((h, 1, pg), F32),
        ],
    )
    return pl.pallas_call(
        functools.partial(_sb_sample_kernel, npg=npg, depth=depth),
        grid_spec=grid_spec,
        out_shape=jax.ShapeDtypeStruct((n, h, d), F32),
        compiler_params=_params("arbitrary", "arbitrary"),
        name="sb_sample",
    )(page_table, cache_k, cache_v, q_bcast, bias_bcast, sfx_mat, pages_mat)


def _head_ones(width):
    hid = jnp.arange(width) // HEAD_DIM
    return (hid[:, None] == hid[None, :]).astype(BF16)


def _rows_for(t, target):
    return target if t % target == 0 else t


def kernel(x_prompt, mem_prompt, x_sample, cache_sb_k, cache_sb_v, page_table, state_wkv, state_shift, cache_mem_k, cache_mem_v, norm_g, w_in, sb_bias, mu_shift, w0, w_lora_b, a0, a_lora_b, k_k, k_a, r_k, lnx_g, lnx_b, mem_norm_g, w_mem_k, w_mem_v, w_out, final_norm_g):
    depth = w_in.shape[0]
    assert depth == 1, "single-layer trunk"
    l = 0
    b_p, t_p, d = x_prompt.shape
    b_s = x_sample.shape[0]
    assert x_sample.shape[1] == 1
    n_mem = mem_prompt.shape[1]
    page = cache_sb_k.shape[2]

    w16 = w_in[l].astype(BF16)
    wkT = w_mem_k[l].T.astype(BF16)
    wvT = w_mem_v[l].T.astype(BF16)
    wo = w_out[l].astype(BF16)
    fg = final_norm_g.reshape(1, d)
    r_k_flat = r_k[l].reshape(1, RWKV_W)
    ones_all = _head_ones(RWKV_W)
    blk = SB_BLOCK
    ar = jnp.arange(blk)
    cs_mat = (ar[:, None] > ar[None, :]).astype(BF16)

    mkT, mvT = _mem_kv(mem_prompt, mem_norm_g[l:l + 1], wkT, wvT)
    p_rw, q_sb, xq, gate, kT, vT = _in_proj(
        x_prompt, norm_g[l:l + 1], w16, _rows_for(t_p, 512), True)
    vecs = (mu_shift[l:l + 1], w0[l:l + 1], a0[l:l + 1], k_k[l:l + 1], k_a[l:l + 1],
            r_k_flat, lnx_g[l:l + 1], lnx_b[l:l + 1])
    o_sb = _sb_prompt(q_sb, kT, vT, sb_bias[l], cs_mat)

    xs2 = x_sample.reshape(1, b_s, d)
    p_rw_s, q_s, xq_s, gate_s, sbk_s, sbv_s = _in_proj(xs2, norm_g[l:l + 1], w16, b_s, False)
    ck = cache_sb_k[l].transpose(0, 2, 3, 1)
    cv = cache_sb_v[l].transpose(0, 2, 3, 1)
    q_b = jnp.broadcast_to(
        (q_s[0] * ATTN_SCALE).reshape(b_s, SB_HEADS, HEAD_DIM, 1), (b_s, SB_HEADS, HEAD_DIM, page))
    bias_b = jnp.broadcast_to(sb_bias[l].reshape(SB_HEADS, 1, 1), (SB_HEADS, 1, page))

    prev0 = jnp.zeros((b_p, 1, RWKV_COLS), F32)
    h0 = jnp.zeros((b_p, RWKV_HEADS, HEAD_DIM, HEAD_DIM), F32)
    o_rw, h_fin = _rwkv_prompt(p_rw, prev0, h0, vecs, w_lora_b[l], a_lora_b[l], ones_all)
    o_sb_s = _sb_sample(page_table, ck, cv, q_b, bias_b).reshape(1, b_s, SB_W)
    o_x = _xattn(xq, mkT, mvT, _rows_for(t_p, 512), 1)
    y_prompt = _out_proj(o_rw, o_sb, o_x, gate, x_prompt, wo, fg, _rows_for(t_p, 512))

    def tokens_major(xT, heads):
        return xT.reshape(1, xT.shape[0], heads, HEAD_DIM, xT.shape[2]).transpose(0, 1, 4, 2, 3)

    new_sb_k_p = tokens_major(kT, SB_HEADS)
    new_sb_v_p = tokens_major(vT, SB_HEADS)
    new_wkv_p = jnp.swapaxes(h_fin, -1, -2)[None]
    new_shift_p = p_rw[:, -1][None]
    new_mem_k_p = tokens_major(mkT, X_HEADS)
    new_mem_v_p = tokens_major(mvT, X_HEADS)

    p_rw_s = p_rw_s[0]
    svecs = (mu_shift[l:l + 1], w0[l:l + 1], a0[l:l + 1], k_k[l:l + 1], k_a[l:l + 1], r_k_flat)
    r_s, w_s, k2_s, v_s, kk_s, bv_s, bonus_s = _rwkv_sample_prep(
        p_rw_s, state_shift[l], svecs, w_lora_b[l], a_lora_b[l], ones_all)
    as_row = lambda a_: a_.reshape(b_s, RWKV_HEADS, 1, HEAD_DIM)
    o_rw_s, new_wkv_s = _rwkv_sample_step(
        state_wkv[l],
        [as_row(t_) for t_ in (r_s, w_s, k2_s, kk_s, bv_s, v_s, bonus_s)],
        lnx_g[l].reshape(RWKV_HEADS, 1, HEAD_DIM), lnx_b[l].reshape(RWKV_HEADS, 1, HEAD_DIM),
        SAMPLE_GROUP if b_s % SAMPLE_GROUP == 0 else 1)
    o_rw_s = o_rw_s.reshape(1, b_s, RWKV_W)

    mk_s = cache_mem_k[l].transpose(0, 2, 3, 1).reshape(b_s, X_W, n_mem)
    mv_s = cache_mem_v[l].transpose(0, 2, 3, 1).reshape(b_s, X_W, n_mem)
    xq_rows = jnp.broadcast_to(xq_s[0][:, None, :], (b_s, 8, X_W))
    o_x_s = _xattn(xq_rows, mk_s, mv_s, 8, SAMPLE_GROUP if b_s % SAMPLE_GROUP == 0 else 1)[:, 0][None]
    y_sample = _out_proj(o_rw_s, o_sb_s, o_x_s, gate_s, xs2, wo, fg, b_s).reshape(b_s, 1, d)

    new_sb_k_s = sbk_s.reshape(1, b_s, 1, SB_HEADS, HEAD_DIM)
    new_sb_v_s = sbv_s.reshape(1, b_s, 1, SB_HEADS, HEAD_DIM)
    new_shift_s = p_rw_s[None]

    return (y_prompt, y_sample, new_sb_k_p, new_sb_v_p, new_wkv_p, new_shift_p,
            new_mem_k_p, new_mem_v_p, new_sb_k_s, new_sb_v_s, new_wkv_s[None], new_shift_s)
```

```python
import functools
import math

import jax
import jax.numpy as jnp
from jax import lax
from jax.experimental import pallas as pl
from jax.experimental.pallas import tpu as pltpu

F32 = jnp.float32
BF16 = jnp.bfloat16

HEAD_DIM = 64
RWKV_HEADS = 6
SB_HEADS = 6
X_HEADS = 4
RWKV_W = RWKV_HEADS * HEAD_DIM
SB_W = SB_HEADS * HEAD_DIM
X_W = X_HEADS * HEAD_DIM
LORA = 64
RWKV_COLS = 3 * RWKV_W + 2 * LORA
NORM_EPS = 1e-6
GN_EPS = 64e-5
DECAY_SCALE = math.exp(-0.5)
ATTN_SCALE = HEAD_DIM ** -0.5
LOG2E = math.log2(math.e)

LANES = 128
PAIR_W = 2 * HEAD_DIM
RWKV_CHUNK = 128
RWKV_SUB = 2
INV_BASE = 8
SB_BLOCK = 256
SB_Q_SUB = 4
PAGES_PER_STEP = 16
PAGE_RING = 3
SAMPLE_GROUP = 8
VMEM_LIMIT = 48 * 1024 * 1024

NN = (((1,), (0,)), ((), ()))
NT = (((1,), (1,)), ((), ()))
TN = (((0,), (0,)), ((), ()))


def _dot(a, b, dims=NN):
    return lax.dot_general(a, b, dims, preferred_element_type=F32)


def _split2(x):
    hi = x.astype(BF16)
    lo = (x - hi.astype(F32)).astype(BF16)
    return hi, lo


def _split3(x):
    hi = x.astype(BF16)
    r1 = x - hi.astype(F32)
    mid = r1.astype(BF16)
    lo = (r1 - mid.astype(F32)).astype(BF16)
    return hi, mid, lo


def _dot_f32(a, b, dims=NN):
    (ka,), (kb,) = dims[0]
    ah, al = _split2(a)
    bh, bl = _split2(b)
    return _dot(jnp.concatenate([ah, ah, al], axis=ka), jnp.concatenate([bh, bl, bh], axis=kb), dims)


def _dot_bf16(a, b, dims=NN):
    return _dot(a.astype(BF16), b.astype(BF16), dims)


def _dot_exact_rhs(a, b_bf16, dims=NN):
    hi, mid, lo = _split3(a)
    return _dot(hi, b_bf16, dims) + (_dot(mid, b_bf16, dims) + _dot(lo, b_bf16, dims))


def _dot_exact_lhs(a_bf16, b, dims=NN):
    hi, mid, lo = _split3(b)
    return _dot(a_bf16, hi, dims) + (_dot(a_bf16, mid, dims) + _dot(a_bf16, lo, dims))


def _sigmoid(x):
    return 1.0 / (1.0 + jnp.exp(-x))


def _softplus_pair(z):
    sp = jnp.maximum(z, 0.0) + jnp.log(1.0 + jnp.exp2(jnp.abs(z) * -LOG2E))
    return sp, z - sp


def _rmsnorm_rows(x, g):
    ms = jnp.mean(x * x, axis=-1, keepdims=True)
    return x * lax.rsqrt(ms + NORM_EPS) * g


def _params(*sem):
    return pltpu.CompilerParams(dimension_semantics=sem, vmem_limit_bytes=VMEM_LIMIT)


KV_COL0 = RWKV_COLS + SB_W
KV_COL1 = KV_COL0 + 2 * SB_W


def _in_proj_kernel(x_ref, g_ref, w_ref, prw_ref, q_ref, xq_ref, gate_ref, k_ref, v_ref,
                    *, kv_feature_major):
    h = _rmsnorm_rows(x_ref[...], g_ref[...]).astype(BF16)
    pa = _dot(h, w_ref[:, :KV_COL0])
    prw_ref[...] = pa[:, :RWKV_COLS]
    q_ref[...] = pa[:, RWKV_COLS:]
    kv = _dot(h, w_ref[:, KV_COL0:KV_COL1])
    if kv_feature_major:
        k_ref[...] = kv[:, :SB_W].T
        v_ref[...] = kv[:, SB_W:].T
    else:
        k_ref[...] = kv[:, :SB_W]
        v_ref[...] = kv[:, SB_W:]
    pb = _dot(h, w_ref[:, KV_COL1:])
    xq_ref[...] = pb[:, :X_W]
    gate_ref[...] = pb[:, X_W:]


def _in_proj(x, norm_g, w16, rows, kv_feature_major):
    b, t, d = x.shape
    n_in = w16.shape[1]
    n_gate = n_in - KV_COL1 - X_W
    row = lambda bi, i: (bi, i, 0)
    col = lambda bi, i: (bi, 0, i)
    const = lambda bi, i: (0, 0)
    if kv_feature_major:
        kv_shape = jax.ShapeDtypeStruct((b, SB_W, t), F32)
        kv_spec = pl.BlockSpec((None, SB_W, rows), col)
    else:
        kv_shape = jax.ShapeDtypeStruct((b, t, SB_W), F32)
        kv_spec = pl.BlockSpec((None, rows, SB_W), row)
    out_shape = (
        jax.ShapeDtypeStruct((b, t, RWKV_COLS), F32),
        jax.ShapeDtypeStruct((b, t, SB_W), F32),
        jax.ShapeDtypeStruct((b, t, X_W), F32),
        jax.ShapeDtypeStruct((b, t, n_gate), F32),
        kv_shape, kv_shape,
    )
    return pl.pallas_call(
        functools.partial(_in_proj_kernel, kv_feature_major=kv_feature_major),
        grid=(b, t // rows),
        in_specs=[
            pl.BlockSpec((None, rows, d), row),
            pl.BlockSpec((1, d), const),
            pl.BlockSpec((d, n_in), const),
        ],
        out_specs=(
            pl.BlockSpec((None, rows, RWKV_COLS), row),
            pl.BlockSpec((None, rows, SB_W), row),
            pl.BlockSpec((None, rows, X_W), row),
            pl.BlockSpec((None, rows, n_gate), row),
            kv_spec, kv_spec,
        ),
        out_shape=out_shape,
        compiler_params=_params("parallel", "parallel"),
        name="in_proj",
    )(x, norm_g, w16)


def _mem_kv_kernel(mem_ref, g_ref, wkT_ref, wvT_ref, kT_ref, vT_ref):
    h = _rmsnorm_rows(mem_ref[...], g_ref[...]).astype(BF16)
    kT_ref[...] = _dot(wkT_ref[...], h, NT)
    vT_ref[...] = _dot(wvT_ref[...], h, NT)


def _mem_kv(mem, g, wkT, wvT):
    b, m, d = mem.shape
    const = lambda bi: (0, 0)
    blk = lambda bi: (bi, 0, 0)
    return pl.pallas_call(
        _mem_kv_kernel,
        grid=(b,),
        in_specs=[
            pl.BlockSpec((None, m, d), blk),
            pl.BlockSpec((1, d), const),
            pl.BlockSpec((X_W, d), const),
            pl.BlockSpec((X_W, d), const),
        ],
        out_specs=(pl.BlockSpec((None, X_W, m), blk), pl.BlockSpec((None, X_W, m), blk)),
        out_shape=(jax.ShapeDtypeStruct((b, X_W, m), F32),) * 2,
        compiler_params=_params("parallel"),
        name="mem_kv",
    )(mem, g, wkT, wvT)


def _xattn_rows(xq, kT_ref, vT_ref):
    heads = range(X_HEADS)
    sl = [slice(h * HEAD_DIM, (h + 1) * HEAD_DIM) for h in heads]
    s = [_dot((xq[:, sl[h]] * ATTN_SCALE).astype(BF16), kT_ref[sl[h], :].astype(BF16)) for h in heads]
    e = [jnp.exp(s[h] - jnp.max(s[h], axis=-1, keepdims=True)) for h in heads]
    l = [jnp.sum(e[h], axis=-1, keepdims=True) for h in heads]
    o = [_dot(e[h].astype(BF16), vT_ref[sl[h], :].astype(BF16), NT) / l[h] for h in heads]
    return jnp.concatenate(o, axis=1)


def _xattn_kernel(xq_ref, kT_ref, vT_ref, o_ref):
    for g in range(xq_ref.shape[0]):
        o_ref[g] = _xattn_rows(xq_ref[g], kT_ref.at[g], vT_ref.at[g])


def _xattn(xq, kT, vT, rows, group):
    b, t, _ = xq.shape
    m = kT.shape[-1]
    row = lambda bi, i: (bi, i, 0)
    full = lambda bi, i: (bi, 0, 0)
    return pl.pallas_call(
        _xattn_kernel,
        grid=(b // group, t // rows),
        in_specs=[
            pl.BlockSpec((group, rows, X_W), row),
            pl.BlockSpec((group, X_W, m), full),
            pl.BlockSpec((group, X_W, m), full),
        ],
        out_specs=pl.BlockSpec((group, rows, X_W), row),
        out_shape=jax.ShapeDtypeStruct((b, t, X_W), F32),
        compiler_params=_params("parallel", "parallel"),
        name="xattn",
    )(xq, kT, vT)


def _out_proj_kernel(orw_ref, osb_ref, x3_ref, gate_ref, x_ref, w_ref, fg_ref, *rest, fused_xattn):
    if fused_xattn:
        kT_ref, vT_ref, y_ref = rest
        o_x = _xattn_rows(x3_ref[...], kT_ref, vT_ref)
    else:
        (y_ref,) = rest
        o_x = x3_ref[...]
    g = gate_ref[...]
    sg = g * _sigmoid(g)
    a0, a1 = RWKV_W, RWKV_W + SB_W
    acc = _dot((orw_ref[...] * sg[:, :a0]).astype(BF16), w_ref[:a0, :])
    acc += _dot((osb_ref[...] * sg[:, a0:a1]).astype(BF16), w_ref[a0:a1, :])
    acc += _dot((o_x * sg[:, a1:]).astype(BF16), w_ref[a1:, :])
    y_ref[...] = _rmsnorm_rows(x_ref[...] + acc, fg_ref[...])


def _out_proj(o_rw, o_sb, x3, gate, x, w_out, fg, rows, mem_kv=None):
    b, t, d = x.shape
    dm = w_out.shape[0]
    row = lambda bi, i: (bi, i, 0)
    const = lambda bi, i: (0, 0)
    in_specs = [
        pl.BlockSpec((None, rows, RWKV_W), row),
        pl.BlockSpec((None, rows, SB_W), row),
        pl.BlockSpec((None, rows, X_W), row),
        pl.BlockSpec((None, rows, dm), row),
        pl.BlockSpec((None, rows, d), row),
        pl.BlockSpec((dm, d), const),
        pl.BlockSpec((1, d), const),
    ]
    operands = [o_rw, o_sb, x3, gate, x, w_out, fg]
    if mem_kv is not None:
        m = mem_kv[0].shape[-1]
        in_specs += [pl.BlockSpec((None, X_W, m), lambda bi, i: (bi, 0, 0))] * 2
        operands += list(mem_kv)
    return pl.pallas_call(
        functools.partial(_out_proj_kernel, fused_xattn=mem_kv is not None),
        grid=(b, t // rows),
        in_specs=in_specs,
        out_specs=pl.BlockSpec((None, rows, d), row),
        out_shape=jax.ShapeDtypeStruct((b, t, d), F32),
        compiler_params=_params("parallel", "parallel"),
        name="out_proj",
    )(*operands)


def _rwkv_token_math(r, k, v, xw, xa, w0, a0, k_k, k_a, r_k, wlb, alb, head_ones):
    log_w = -DECAY_SCALE * _sigmoid(w0 + _dot(jnp.tanh(xw).astype(BF16), wlb.astype(BF16)))
    a = _sigmoid(a0 + _dot(xa.astype(BF16), alb.astype(BF16)))
    kk = k * k_k
    ss = _dot_exact_rhs(kk * kk, head_ones)
    kk = kk * lax.rsqrt(jnp.maximum(ss, 1e-12))
    k2 = k * (1.0 + (a - 1.0) * k_a)
    bonus = _dot_exact_rhs(r * k2 * r_k, head_ones) * v
    return log_w, a, kk, k2, bonus


def _group_norm(o, g, b, head_ones):
    inv = 1.0 / HEAD_DIM
    mean = _dot_exact_rhs(o, head_ones) * inv
    d = o - mean
    var = _dot_exact_rhs(d * d, head_ones) * inv
    return d * lax.rsqrt(var + GN_EPS) * g + b


def _shift_mix(cur, last_row, mu, first_row_mask):
    prev = jnp.where(first_row_mask, last_row, pltpu.roll(cur, 1, axis=0))
    return cur + (prev - cur) * mu


def _rwkv_prompt_kernel(
        p_ref, prev_ref, first_ref, mu_ref,
        w0_ref, a0_ref, kk_ref, ka_ref, rk_ref, lg_ref, lb_ref, wlb_ref, alb_ref,
        h0_ref, tri_ref, ones_ref,
        o_ref, hfin_ref, h_scr, *, n_sub):
    i = pl.program_id(1)
    c = RWKV_CHUNK
    rows = c * n_sub
    w3 = 3 * RWKV_W

    @pl.when(i == 0)
    def _():
        h_scr[...] = h0_ref[...]

    first = lax.broadcasted_iota(jnp.int32, (rows, RWKV_COLS), 0) == 0
    last = jnp.where(i == 0, first_ref[...], prev_ref[7:8, :])
    xs = _shift_mix(p_ref[...], last, mu_ref[...], first)
    r, k, v = xs[:, :RWKV_W], xs[:, RWKV_W:2 * RWKV_W], xs[:, 2 * RWKV_W:w3]
    head_ones = ones_ref[...]
    log_w, a, kk, k2, bonus = _rwkv_token_math(
        r, k, v, xs[:, w3:w3 + LORA], xs[:, w3 + LORA:], w0_ref[...], a0_ref[...], kk_ref[...],
        ka_ref[...], rk_ref[...], wlb_ref[...], alb_ref[...], head_ones)

    subs = range(n_sub)
    lam = _dot_exact_lhs(tri_ref[...], log_w)
    lam_ends = [lam[(sc + 1) * c - 1:(sc + 1) * c, :] for sc in subs]
    lam_c = jnp.concatenate([jnp.broadcast_to(le, (c, RWKV_W)) for le in lam_ends], axis=0)
    w_in = jnp.exp(lam)
    w_ex = jnp.exp(lam - log_w)
    w_inv = jnp.exp(-lam)
    w_end = jnp.exp(lam_c - lam)
    w_c = [jnp.exp(le) for le in lam_ends]
    bvec = kk * a
    at = -kk * w_ex
    bt = bvec * w_inv
    kt = k2 * w_inv
    rt = r * w_in
    bh = bvec * w_end
    kh = k2 * w_end

    colid = lax.broadcasted_iota(jnp.int32, (c, c), 1)
    rowid = lax.broadcasted_iota(jnp.int32, (c, c), 0)
    strict = colid < rowid
    incl = colid <= rowid
    eye64 = (lax.broadcasted_iota(jnp.int32, (HEAD_DIM, HEAD_DIM), 0)
             == lax.broadcasted_iota(jnp.int32, (HEAD_DIM, HEAD_DIM), 1))

    heads = range(RWKV_HEADS)
    chains = [(sc, hh) for sc in subs for hh in heads]
    cut = lambda x: {(sc, hh): x[sc * c:(sc + 1) * c, hh * HEAD_DIM:(hh + 1) * HEAD_DIM]
                     for sc, hh in chains}
    at_h, bt_h, kt_h, rt_h, v_h, bh_h, kh_h = map(cut, (at, bt, kt, rt, v, bh, kh))
    each = lambda fn: {ch: fn(ch) for ch in chains}
    scores = each(lambda ch: _dot_f32(
        jnp.concatenate([at_h[ch], rt_h[ch]], axis=0),
        jnp.concatenate([bt_h[ch], kt_h[ch]], axis=0), NT))
    n = each(lambda ch: jnp.where(strict, scores[ch][:c, :c], 0.0))
    aak = each(lambda ch: jnp.where(strict, scores[ch][:c, c:], 0.0))
    mrb = each(lambda ch: jnp.where(incl, scores[ch][c:, :c], 0.0))
    mrk = each(lambda ch: jnp.where(incl, scores[ch][c:, c:], 0.0))
    rb, cb = rowid // INV_BASE, colid // INV_BASE
    dg = each(lambda ch: jnp.where(rb == cb, n[ch], 0.0))
    tm = each(lambda ch: jnp.where(colid == rowid, 1.0, dg[ch]))
    span = 1
    while 2 * span < INV_BASE:
        dg = each(lambda ch: _dot_bf16(dg[ch], dg[ch]))
        tm = each(lambda ch: tm[ch] + _dot_bf16(tm[ch], dg[ch]))
        span *= 2
    m = INV_BASE
    while m < c:
        off = (rowid // (2 * m) == colid // (2 * m)) & (rowid // m != colid // m)
        tn = each(lambda ch: _dot_bf16(tm[ch], jnp.where(off, n[ch], 0.0)))
        tm = each(lambda ch: tm[ch] + _dot_bf16(tn[ch], tm[ch]))
        m *= 2
    akv = each(lambda ch: _dot_f32(aak[ch], v_h[ch]))
    mkv = each(lambda ch: _dot_f32(mrk[ch], v_h[ch]))
    khv = each(lambda ch: _dot_f32(kh_h[ch], v_h[ch], TN))
    pq = each(lambda ch: _dot_f32(tm[ch], jnp.concatenate([at_h[ch], akv[ch]], axis=1)))
    mpq = each(lambda ch: _dot_f32(mrb[ch], pq[ch]))
    gj = each(lambda ch: _dot_f32(bh_h[ch], pq[ch], TN))
    p2 = each(lambda ch: rt_h[ch] + mpq[ch][:, :HEAD_DIM])
    q2 = each(lambda ch: mpq[ch][:, HEAD_DIM:] + mkv[ch])
    g = each(lambda ch: gj[ch][:, :HEAD_DIM] + jnp.where(
        eye64, w_c[ch[0]][:, ch[1] * HEAD_DIM:(ch[1] + 1) * HEAD_DIM], 0.0))
    jm = each(lambda ch: gj[ch][:, HEAD_DIM:] + khv[ch])

    state = [h_scr[hh] for hh in heads]
    outs = []
    for sc in subs:
        outs.append(jnp.concatenate(
            [_dot_f32(p2[(sc, hh)], state[hh]) + q2[(sc, hh)] for hh in heads], axis=1))
        state = [_dot_f32(g[(sc, hh)], state[hh]) + jm[(sc, hh)] for hh in heads]
    for hh in heads:
        h_scr[hh] = state[hh]

    o = jnp.concatenate(outs, axis=0)
    o_ref[...] = _group_norm(o, lg_ref[...], lb_ref[...], head_ones) + bonus

    @pl.when(i == pl.num_programs(1) - 1)
    def _():
        hfin_ref[...] = h_scr[...]


def _rwkv_prompt_specs(b, t, n_sub, index):
    c = RWKV_CHUNK * n_sub
    assert t % c == 0, "sequence length must be a multiple of the tokens per step"
    sub = 8
    const = index(lambda bi, i: (0, 0))
    hvec = pl.BlockSpec((1, RWKV_W), const)
    lora = pl.BlockSpec((LORA, RWKV_W), const)
    state = pl.BlockSpec((None, RWKV_HEADS, HEAD_DIM, HEAD_DIM), index(lambda bi, i: (bi, 0, 0, 0)))
    in_specs = [
        pl.BlockSpec((None, c, RWKV_COLS), index(lambda bi, i: (bi, i, 0))),
        pl.BlockSpec((None, sub, RWKV_COLS),
                     index(lambda bi, i: (bi, jnp.maximum(i * (c // sub) - 1, 0), 0))),
        pl.BlockSpec((None, 1, RWKV_COLS), index(lambda bi, i: (bi, 0, 0))),
        pl.BlockSpec((1, RWKV_COLS), const),
        hvec, hvec, hvec, hvec, hvec, hvec, hvec, lora, lora,
        state,
        pl.BlockSpec((c, c), const),
        pl.BlockSpec((RWKV_W, RWKV_W), const),
    ]
    out_specs = [pl.BlockSpec((None, c, RWKV_W), index(lambda bi, i: (bi, i, 0))), state]
    out_shape = [jax.ShapeDtypeStruct((b, t, RWKV_W), F32),
                 jax.ShapeDtypeStruct((b, RWKV_HEADS, HEAD_DIM, HEAD_DIM), F32)]
    scratch = [pltpu.VMEM((RWKV_HEADS, HEAD_DIM, HEAD_DIM), F32)]
    return in_specs, out_specs, out_shape, scratch


def _chunk_tri(n_sub):
    c = RWKV_CHUNK
    tok = jnp.arange(c * n_sub)
    return ((tok[None, :] <= tok[:, None]) & (tok[None, :] // c == tok[:, None] // c)).astype(BF16)


def _rwkv_prompt(p_rw, prev0, h0, vecs, wlb, alb, ones):
    b, t, _ = p_rw.shape
    mu, w0, a0, k_k, k_a, r_k, lnx_g, lnx_b = vecs
    in_specs, out_specs, out_shape, scratch = _rwkv_prompt_specs(b, t, RWKV_SUB, lambda fn: fn)
    return pl.pallas_call(
        functools.partial(_rwkv_prompt_kernel, n_sub=RWKV_SUB),
        grid=(b, t // (RWKV_CHUNK * RWKV_SUB)),
        in_specs=in_specs,
        out_specs=tuple(out_specs),
        out_shape=tuple(out_shape),
        scratch_shapes=scratch,
        compiler_params=_params("parallel", "arbitrary"),
        name="rwkv_prompt",
    )(p_rw, p_rw, prev0, mu, w0, a0, k_k, k_a, r_k, lnx_g, lnx_b, wlb, alb, h0,
      _chunk_tri(RWKV_SUB), ones)


def _rwkv_sample_prep_kernel(p_ref, prev_ref, mu_ref, w0_ref, a0_ref, kk_ref, ka_ref, rk_ref,
                             wlb_ref, alb_ref, ones_ref,
                             r_ref, w_ref, k2_ref, v_ref, kkn_ref, b_ref, bonus_ref):
    pf = p_ref[...]
    xs = pf + (prev_ref[...] - pf) * mu_ref[...]
    w3 = 3 * RWKV_W
    r, k, v = xs[:, :RWKV_W], xs[:, RWKV_W:2 * RWKV_W], xs[:, 2 * RWKV_W:w3]
    log_w, a, kk, k2, bonus = _rwkv_token_math(
        r, k, v, xs[:, w3:w3 + LORA], xs[:, w3 + LORA:], w0_ref[...], a0_ref[...], kk_ref[...],
        ka_ref[...], rk_ref[...], wlb_ref[...], alb_ref[...], ones_ref[...])
    r_ref[...] = r
    w_ref[...] = jnp.exp(log_w)
    k2_ref[...] = k2
    v_ref[...] = v
    kkn_ref[...] = kk
    b_ref[...] = kk * a
    bonus_ref[...] = bonus


def _rwkv_sample_prep(p, prev, vecs, wlb, alb, ones):
    n = p.shape[0]
    mu, w0, a0, k_k, k_a, r_k = vecs
    return pl.pallas_call(
        _rwkv_sample_prep_kernel,
        out_shape=(jax.ShapeDtypeStruct((n, RWKV_W), F32),) * 7,
        name="rwkv_sample_prep",
    )(p, prev, mu, w0, a0, k_k, k_a, r_k, wlb, alb, ones)


def _rwkv_sample_step_kernel(s_ref, r_ref, w_ref, k_ref, kk_ref, b_ref, v_ref, bonus_ref,
                             g_ref, beta_ref, o_ref, snew_ref):
    eye = (lax.broadcasted_iota(jnp.int32, (HEAD_DIM, HEAD_DIM), 0)
           == lax.broadcasted_iota(jnp.int32, (HEAD_DIM, HEAD_DIM), 1))
    to_col = lambda row: jnp.sum(jnp.where(eye, row, 0.0), axis=3, keepdims=True)
    to_row = lambda col: jnp.sum(jnp.where(eye, col, 0.0), axis=2, keepdims=True)
    s = s_ref[...]
    s_kk = jnp.sum(s * kk_ref[...], axis=3, keepdims=True)
    s = s * w_ref[...] - s_kk * b_ref[...] + to_col(v_ref[...]) * k_ref[...]
    snew_ref[...] = s
    o = to_row(jnp.sum(s * r_ref[...], axis=3, keepdims=True))
    mean = jnp.mean(o, axis=3, keepdims=True)
    d = o - mean
    var = jnp.mean(d * d, axis=3, keepdims=True)
    o_ref[...] = d * lax.rsqrt(var + GN_EPS) * g_ref[...] + beta_ref[...] + bonus_ref[...]


def _rwkv_sample_step(s0, rows, g_row, beta_row, group):
    n = s0.shape[0]
    h, d = RWKV_HEADS, HEAD_DIM
    st = pl.BlockSpec((group, h, d, d), lambda bi: (bi, 0, 0, 0))
    rowspec = pl.BlockSpec((group, h, 1, d), lambda bi: (bi, 0, 0, 0))
    cconst = pl.BlockSpec((h, 1, d), lambda bi: (0, 0, 0))
    return pl.pallas_call(
        _rwkv_sample_step_kernel,
        grid=(n // group,),
        in_specs=[st] + [rowspec] * 7 + [cconst] * 2,
        out_specs=(rowspec, st),
        out_shape=(jax.ShapeDtypeStruct((n, h, 1, d), F32),
                   jax.ShapeDtypeStruct((n, h, d, d), F32)),
        compiler_params=_params("parallel"),
        name="rwkv_sample_step",
    )(s0, *rows, g_row, beta_row)


def _sb_prompt_kernel(bias_ref, q_ref, kT_ref, vT_ref, cs_ref, o_ref):
    pair = pl.program_id(1)
    qi = pl.program_id(2)
    blk = SB_BLOCK
    cs_mat = cs_ref[...]
    q_all = q_ref[...] * ATTN_SCALE
    valid = (lax.broadcasted_iota(jnp.int32, (blk, blk), 1)
             < lax.broadcasted_iota(jnp.int32, (blk, blk), 0))

    heads = range(2)
    rows = [slice(hh * HEAD_DIM, (hh + 1) * HEAD_DIM) for hh in heads]
    lane = lax.broadcasted_iota(jnp.int32, (blk, HEAD_DIM), 1)
    ones2 = jnp.where(lane < 2, 1.0, 0.0)
    krow = lax.broadcasted_iota(jnp.int32, (HEAD_DIM, blk), 0)
    bias_rows = []
    for hh in heads:
        b_hi, b_lo = _split2(jnp.full((HEAD_DIM, blk), bias_ref[2 * pair + hh], F32))
        bias_rows.append(
            jnp.where(krow == 0, b_hi.astype(F32), jnp.where(krow == 1, b_lo.astype(F32), 0.0))
            .astype(BF16))
    chains = [(hh, u) for u in range(SB_Q_SUB) for hh in heads]
    qs = {(hh, u): jnp.concatenate([q_all[u * blk:(u + 1) * blk, rows[hh]], ones2], axis=1).astype(BF16)
          for hh, u in chains}

    def run(blocks, items, state):
        starts = [pl.multiple_of(j * blk, blk) for j in blocks]
        carries, accs = dict(zip(chains, state[0])), dict(zip(chains, state[1]))
        k16, v16, tmp = {}, {}, {}

        def stage(s, it):
            bid, ch, diag = items[it]
            hh = ch[0]
            if s == 0:
                if (bid, hh) not in k16:
                    k16[bid, hh] = jnp.concatenate(
                        [kT_ref[rows[hh], pl.ds(starts[bid], blk)].astype(BF16), bias_rows[hh]], axis=0)
                tmp[it] = {"z": _dot(qs[ch], k16[bid, hh])}
            elif s == 1:
                t = tmp[it]
                sp, t["zs"] = _softplus_pair(t.pop("z"))
                if diag:
                    sp = jnp.where(valid, sp, 0.0)
                t["first"] = sp[:, 0:1]
                t["parts"] = sp.astype(BF16)
            elif s == 2:
                t = tmp[it]
                t["cs"] = _dot(t.pop("parts"), cs_mat)
            elif s == 3:
                t = tmp[it]
                cs = t.pop("cs")
                a = jnp.exp(t.pop("zs") - cs - carries[ch])
                if diag:
                    a = jnp.where(valid, a, 0.0)
                t["a"] = a.astype(BF16)
                carries[ch] = carries[ch] + (cs[:, 0:1] + t.pop("first"))
            else:
                if (bid, hh) not in v16:
                    v16[bid, hh] = vT_ref[rows[hh], pl.ds(starts[bid], blk)].astype(BF16)
                accs[ch] = accs[ch] + _dot(tmp.pop(it)["a"], v16[bid, hh], NT)

        n_stage = 5
        for wave in range(len(items) + n_stage - 1):
            for s in reversed(range(n_stage)):
                if 0 <= wave - s < len(items):
                    stage(s, wave - s)
        return (tuple(carries[ch] for ch in chains), tuple(accs[ch] for ch in chains))

    zeros_c = jnp.zeros((blk, 1), F32)
    zeros_o = jnp.zeros((blk, HEAD_DIM), F32)
    state = ((zeros_c,) * len(chains), (zeros_o,) * len(chains))
    top = SB_Q_SUB * qi
    order = list(reversed(range(SB_Q_SUB)))
    state = run([top + d for d in order],
                [(i, ch, ch[1] == d) for i, d in enumerate(order) for ch in chains if ch[1] >= d],
                state)

    def body(jj, st):
        first = top - 1 - jj * SB_Q_SUB
        return run([first - r for r in range(SB_Q_SUB)],
                   [(r, ch, False) for r in range(SB_Q_SUB) for ch in chains], st)

    state = lax.fori_loop(0, qi, body, state)
    accs = dict(zip(chains, state[1]))
    o_ref[...] = jnp.concatenate(
        [jnp.concatenate([accs[(hh, u)] for hh in heads], axis=1) for u in range(SB_Q_SUB)], axis=0)


def _sb_prompt(q, kT, vT, bias, cs_mat):
    b, t, _ = q.shape
    blk = SB_BLOCK * SB_Q_SUB
    assert t % blk == 0, "sequence length must be a multiple of the query rows per step"
    npair = SB_HEADS // 2
    kv = pl.BlockSpec((None, PAIR_W, t), lambda bi, p, i: (bi, p, 0))
    qo = pl.BlockSpec((None, blk, PAIR_W), lambda bi, p, i: (bi, i, p))
    return pl.pallas_call(
        _sb_prompt_kernel,
        grid=(b, npair, t // blk),
        in_specs=[
            pl.BlockSpec(memory_space=pltpu.SMEM),
            qo, kv, kv,
            pl.BlockSpec((SB_BLOCK, SB_BLOCK), lambda bi, p, i: (0, 0)),
        ],
        out_specs=qo,
        out_shape=jax.ShapeDtypeStruct((b, t, SB_W), F32),
        compiler_params=_params("parallel", "parallel", "arbitrary"),
        name="sb_prompt",
    )(bias, q, kT, vT, cs_mat)


def _sb_sample_pieces(k_refs, v_refs, q_ref, bias_ref, sfx_ref, pages_ref, o_ref, acc_ref, carry_ref,
                      first_step, last_step):
    npg = len(k_refs)
    heads = range(SB_HEADS)
    nrow = SB_HEADS * npg
    pad = -nrow % LANES
    st = {}

    def init():
        @pl.when(first_step)
        def _():
            acc_ref[...] = jnp.zeros_like(acc_ref)
            carry_ref[...] = jnp.zeros_like(carry_ref)

    def score_row(h, u):
        st["z", h, u] = jnp.sum(k_refs[u][h] * q_ref[h], axis=0, keepdims=True) + bias_ref[h]

    def weights():
        zpad = [jnp.zeros((pad, LANES), F32)] if pad else []
        z = jnp.concatenate([st.pop(("z", h, u)) for h in heads for u in range(npg)] + zpad, axis=0)
        sp, _ = _softplus_pair(z)
        sums = _dot_exact_rhs(sp, sfx_ref[...])
        suffix, tot = sums[:, :LANES], sums[:, LANES:]
        later = _dot_exact_lhs(pages_ref[...], tot)
        carry = jnp.concatenate(
            [jnp.broadcast_to(carry_ref[h], (npg, LANES)) for h in heads] + zpad, axis=0)
        st["a"] = jnp.exp(z - suffix - later - carry)
        st["done"] = later + tot

    def value_row(h, u):
        r = h * npg + u
        acc = acc_ref[h] if u == 0 else st.pop(("acc", h))
        acc = acc + v_refs[u][h] * st["a"][r:r + 1, :]
        if u < npg - 1:
            st["acc", h] = acc
        else:
            acc_ref[h] = acc
            carry_ref[h] = carry_ref[h] + st["done"][r:r + 1, :]

    def finish():
        @pl.when(last_step)
        def _():
            o_ref[...] = jnp.sum(acc_ref[...], axis=2)

    pairs = [(h, u) for h in heads for u in range(npg)]
    return ([init] + [functools.partial(score_row, h, u) for h, u in pairs] + [weights]
            + [functools.partial(value_row, h, u) for h, u in pairs] + [finish])


def _sb_sample_kernel(pt_ref, k_hbm, v_hbm, q_ref, bias_ref, sfx_ref, pages_ref, o_ref,
                      kbuf, vbuf, sem, acc_ref, carry_ref, *, npg, depth):
    n_seq, n_steps = pl.num_programs(0), pl.num_programs(1)
    j = pl.program_id(1)
    g = pl.program_id(0) * n_steps + j
    total = n_seq * n_steps
    last_page = n_steps * npg - 1

    def page_copies(step, u, page):
        slot = step % depth
        return (pltpu.make_async_copy(k_hbm.at[page], kbuf.at[slot, u], sem.at[slot, 0]),
                pltpu.make_async_copy(v_hbm.at[page], vbuf.at[slot, u], sem.at[slot, 1]))

    def start_step(step):
        seq, sj = step // n_steps, step % n_steps
        for u in range(npg):
            for cp in page_copies(step, u, pt_ref[seq, last_page - (sj * npg + u)]):
                cp.start()

    @pl.when(g == 0)
    def _():
        for s in range(depth - 1):
            @pl.when(s < total)
            def _():
                start_step(jnp.int32(s))

    ahead = g + (depth - 1)

    @pl.when(ahead < total)
    def _():
        start_step(ahead)

    for u in range(npg):
        for cp in page_copies(g, u, 0):
            cp.wait()

    slot = g % depth
    k_refs = [kbuf.at[slot, u] for u in range(npg)]
    v_refs = [vbuf.at[slot, u] for u in range(npg)]
    for piece in _sb_sample_pieces(k_refs, v_refs, q_ref, bias_ref, sfx_ref, pages_ref, o_ref,
                                   acc_ref, carry_ref, j == 0, j == n_steps - 1):
        piece()


def _sb_sample_consts(h, npg, pg):
    assert pg == LANES
    nrow = h * npg + (-(h * npg) % LANES)
    tok = jnp.arange(pg)
    sfx_mat = jnp.concatenate(
        [(tok[:, None] >= tok[None, :]).astype(BF16), jnp.ones((pg, pg), BF16)], axis=1)
    r = jnp.arange(nrow)
    pages_mat = ((r[:, None] // npg == r[None, :] // npg) & (r[None, :] < r[:, None])
                 & (r[:, None] < h * npg)).astype(BF16)
    return sfx_mat, pages_mat


def _sb_sample(page_table, cache_k, cache_v, q_bcast, bias_bcast):
    n, n_pages = page_table.shape
    npg = PAGES_PER_STEP
    depth = PAGE_RING
    h, d, pg = cache_k.shape[1:]
    assert n_pages % npg == 0

    sfx_mat, pages_mat = _sb_sample_consts(h, npg, pg)
    const2 = lambda bi, j, pt: (0, 0)
    grid_spec = pltpu.PrefetchScalarGridSpec(
        num_scalar_prefetch=1,
        grid=(n, n_pages // npg),
        in_specs=[
            pl.BlockSpec(memory_space=pl.ANY),
            pl.BlockSpec(memory_space=pl.ANY),
            pl.BlockSpec((None, h, d, pg), lambda bi, j, pt: (bi, 0, 0, 0)),
            pl.BlockSpec((h, 1, pg), lambda bi, j, pt: (0, 0, 0)),
            pl.BlockSpec(sfx_mat.shape, const2),
            pl.BlockSpec(pages_mat.shape, const2),
        ],
        out_specs=pl.BlockSpec((None, h, d), lambda bi, j, pt: (bi, 0, 0)),
        scratch_shapes=[
            pltpu.VMEM((depth, npg, h, d, pg), F32),
            pltpu.VMEM((depth, npg, h, d, pg), F32),
            pltpu.SemaphoreType.DMA((depth, 2)),
            pltpu.VMEM((h, d, pg), F32),
            pltpu.VMEM((h, 1, pg), F32),
        ],
    )
    return pl.pallas_call(
        functools.partial(_sb_sample_kernel, npg=npg, depth=depth),
        grid_spec=grid_spec,
        out_shape=jax.ShapeDtypeStruct((n, h, d), F32),
        compiler_params=_params("arbitrary", "arbitrary"),
        name="sb_sample",
    )(page_table, cache_k, cache_v, q_bcast, bias_bcast, sfx_mat, pages_mat)


def _head_ones(width):
    hid = jnp.arange(width) // HEAD_DIM
    return (hid[:, None] == hid[None, :]).astype(BF16)


def _rows_for(t, target):
    return target if t % target == 0 else t


def kernel(x_prompt, mem_prompt, x_sample, cache_sb_k, cache_sb_v, page_table, state_wkv, state_shift, cache_mem_k, cache_mem_v, norm_g, w_in, sb_bias, mu_shift, w0, w_lora_b, a0, a_lora_b, k_k, k_a, r_k, lnx_g, lnx_b, mem_norm_g, w_mem_k, w_mem_v, w_out, final_norm_g):
    depth = w_in.shape[0]
    assert depth == 1, "single-layer trunk"
    l = 0
    b_p, t_p, d = x_prompt.shape
    b_s = x_sample.shape[0]
    assert x_sample.shape[1] == 1
    n_mem = mem_prompt.shape[1]
    page = cache_sb_k.shape[2]

    w16 = w_in[l].astype(BF16)
    wkT = w_mem_k[l].T.astype(BF16)
    wvT = w_mem_v[l].T.astype(BF16)
    wo = w_out[l].astype(BF16)
    fg = final_norm_g.reshape(1, d)
    r_k_flat = r_k[l].reshape(1, RWKV_W)
    ones_all = _head_ones(RWKV_W)
    blk = SB_BLOCK
    ar = jnp.arange(blk)
    cs_mat = (ar[:, None] > ar[None, :]).astype(BF16)

    mkT, mvT = _mem_kv(mem_prompt, mem_norm_g[l:l + 1], wkT, wvT)
    p_rw, q_sb, xq, gate, kT, vT = _in_proj(
        x_prompt, norm_g[l:l + 1], w16, _rows_for(t_p, 512), True)
    vecs = (mu_shift[l:l + 1], w0[l:l + 1], a0[l:l + 1], k_k[l:l + 1], k_a[l:l + 1],
            r_k_flat, lnx_g[l:l + 1], lnx_b[l:l + 1])
    o_sb = _sb_prompt(q_sb, kT, vT, sb_bias[l], cs_mat)

    xs2 = x_sample.reshape(1, b_s, d)
    p_rw_s, q_s, xq_s, gate_s, sbk_s, sbv_s = _in_proj(xs2, norm_g[l:l + 1], w16, b_s, False)
    ck = cache_sb_k[l].transpose(0, 2, 3, 1)
    cv = cache_sb_v[l].transpose(0, 2, 3, 1)
    q_b = jnp.broadcast_to(
        (q_s[0] * ATTN_SCALE).reshape(b_s, SB_HEADS, HEAD_DIM, 1), (b_s, SB_HEADS, HEAD_DIM, page))
    bias_b = jnp.broadcast_to(sb_bias[l].reshape(SB_HEADS, 1, 1), (SB_HEADS, 1, page))

    prev0 = jnp.zeros((b_p, 1, RWKV_COLS), F32)
    h0 = jnp.zeros((b_p, RWKV_HEADS, HEAD_DIM, HEAD_DIM), F32)
    o_rw, h_fin = _rwkv_prompt(p_rw, prev0, h0, vecs, w_lora_b[l], a_lora_b[l], ones_all)
    o_sb_s = _sb_sample(page_table, ck, cv, q_b, bias_b).reshape(1, b_s, SB_W)
    y_prompt = _out_proj(o_rw, o_sb, xq, gate, x_prompt, wo, fg, _rows_for(t_p, 512), (mkT, mvT))

    def tokens_major(xT, heads):
        return xT.reshape(1, xT.shape[0], heads, HEAD_DIM, xT.shape[2]).transpose(0, 1, 4, 2, 3)

    new_sb_k_p = tokens_major(kT, SB_HEADS)
    new_sb_v_p = tokens_major(vT, SB_HEADS)
    new_wkv_p = jnp.swapaxes(h_fin, -1, -2)[None]
    new_shift_p = p_rw[:, -1][None]
    new_mem_k_p = tokens_major(mkT, X_HEADS)
    new_mem_v_p = tokens_major(mvT, X_HEADS)

    p_rw_s = p_rw_s[0]
    svecs = (mu_shift[l:l + 1], w0[l:l + 1], a0[l:l + 1], k_k[l:l + 1], k_a[l:l + 1], r_k_flat)
    r_s, w_s, k2_s, v_s, kk_s, bv_s, bonus_s = _rwkv_sample_prep(
        p_rw_s, state_shift[l], svecs, w_lora_b[l], a_lora_b[l], ones_all)
    as_row = lambda a_: a_.reshape(b_s, RWKV_HEADS, 1, HEAD_DIM)
    o_rw_s, new_wkv_s = _rwkv_sample_step(
        state_wkv[l],
        [as_row(t_) for t_ in (r_s, w_s, k2_s, kk_s, bv_s, v_s, bonus_s)],
        lnx_g[l].reshape(RWKV_HEADS, 1, HEAD_DIM), lnx_b[l].reshape(RWKV_HEADS, 1, HEAD_DIM),
        SAMPLE_GROUP if b_s % SAMPLE_GROUP == 0 else 1)
    o_rw_s = o_rw_s.reshape(1, b_s, RWKV_W)

    mk_s = cache_mem_k[l].transpose(0, 2, 3, 1).reshape(b_s, X_W, n_mem)
    mv_s = cache_mem_v[l].transpose(0, 2, 3, 1).reshape(b_s, X_W, n_mem)
    xq_rows = jnp.broadcast_to(xq_s[0][:, None, :], (b_s, 8, X_W))
    o_x_s = _xattn(xq_rows, mk_s, mv_s, 8, SAMPLE_GROUP if b_s % SAMPLE_GROUP == 0 else 1)[:, 0][None]
    y_sample = _out_proj(o_rw_s, o_sb_s, o_x_s, gate_s, xs2, wo, fg, b_s).reshape(b_s, 1, d)

    new_sb_k_s = sbk_s.reshape(1, b_s, 1, SB_HEADS, HEAD_DIM)
    new_sb_v_s = sbv_s.reshape(1, b_s, 1, SB_HEADS, HEAD_DIM)
    new_shift_s = p_rw_s[None]

    return (y_prompt, y_sample, new_sb_k_p, new_sb_v_p, new_wkv_p, new_shift_p,
            new_mem_k_p, new_mem_v_p, new_sb_k_s, new_sb_v_s, new_wkv_s[None], new_shift_s)
```

```python
import functools
import math

import jax
import jax.numpy as jnp
from jax import lax
from jax.experimental import pallas as pl
from jax.experimental.pallas import tpu as pltpu

F32 = jnp.float32
BF16 = jnp.bfloat16

HEAD_DIM = 64
RWKV_HEADS = 6
SB_HEADS = 6
X_HEADS = 4
RWKV_W = RWKV_HEADS * HEAD_DIM
SB_W = SB_HEADS * HEAD_DIM
X_W = X_HEADS * HEAD_DIM
LORA = 64
RWKV_COLS = 3 * RWKV_W + 2 * LORA
NORM_EPS = 1e-6
GN_EPS = 64e-5
DECAY_SCALE = math.exp(-0.5)
ATTN_SCALE = HEAD_DIM ** -0.5
LOG2E = math.log2(math.e)

LANES = 128
PAIR_W = 2 * HEAD_DIM
RWKV_CHUNK = 128
RWKV_SUB = 2
INV_BASE = 8
SB_BLOCK = 256
SB_Q_SUB = 4
PAGES_PER_STEP = 16
PAGE_RING = 3
SAMPLE_GROUP = 8
VMEM_LIMIT = 48 * 1024 * 1024

NN = (((1,), (0,)), ((), ()))
NT = (((1,), (1,)), ((), ()))
TN = (((0,), (0,)), ((), ()))


def _dot(a, b, dims=NN):
    return lax.dot_general(a, b, dims, preferred_element_type=F32)


def _split2(x):
    hi = x.astype(BF16)
    lo = (x - hi.astype(F32)).astype(BF16)
    return hi, lo


def _dot_f32(a, b, dims=NN):
    (ka,), (kb,) = dims[0]
    ah, al = _split2(a)
    bh, bl = _split2(b)
    return _dot(jnp.concatenate([ah, ah, al], axis=ka), jnp.concatenate([bh, bl, bh], axis=kb), dims)


def _dot_bf16(a, b, dims=NN):
    return _dot(a.astype(BF16), b.astype(BF16), dims)


def _dot_exact_rhs(a, b_bf16, dims=NN):
    hi, lo = _split2(a)
    return _dot(hi, b_bf16, dims) + _dot(lo, b_bf16, dims)


def _dot_exact_lhs(a_bf16, b, dims=NN):
    hi, lo = _split2(b)
    return _dot(a_bf16, hi, dims) + _dot(a_bf16, lo, dims)


def _sigmoid(x):
    return 1.0 / (1.0 + jnp.exp(-x))


def _softplus_pair(z):
    sp = jnp.maximum(z, 0.0) + jnp.log(1.0 + jnp.exp2(jnp.abs(z) * -LOG2E))
    return sp, z - sp


def _rmsnorm_rows(x, g):
    ms = jnp.mean(x * x, axis=-1, keepdims=True)
    return x * lax.rsqrt(ms + NORM_EPS) * g


def _params(*sem):
    return pltpu.CompilerParams(dimension_semantics=sem, vmem_limit_bytes=VMEM_LIMIT)


KV_COL0 = RWKV_COLS + SB_W
KV_COL1 = KV_COL0 + 2 * SB_W


def _in_proj_kernel(x_ref, g_ref, w_ref, prw_ref, q_ref, xq_ref, gate_ref, k_ref, v_ref,
                    *, kv_feature_major):
    h = _rmsnorm_rows(x_ref[...], g_ref[...]).astype(BF16)
    pa = _dot(h, w_ref[:, :KV_COL0])
    prw_ref[...] = pa[:, :RWKV_COLS]
    q_ref[...] = pa[:, RWKV_COLS:]
    kv = _dot(h, w_ref[:, KV_COL0:KV_COL1])
    if kv_feature_major:
        k_ref[...] = kv[:, :SB_W].T
        v_ref[...] = kv[:, SB_W:].T
    else:
        k_ref[...] = kv[:, :SB_W]
        v_ref[...] = kv[:, SB_W:]
    pb = _dot(h, w_ref[:, KV_COL1:])
    xq_ref[...] = pb[:, :X_W]
    gate_ref[...] = pb[:, X_W:]


def _in_proj(x, norm_g, w16, rows, kv_feature_major):
    b, t, d = x.shape
    n_in = w16.shape[1]
    n_gate = n_in - KV_COL1 - X_W
    row = lambda bi, i: (bi, i, 0)
    col = lambda bi, i: (bi, 0, i)
    const = lambda bi, i: (0, 0)
    if kv_feature_major:
        kv_shape = jax.ShapeDtypeStruct((b, SB_W, t), F32)
        kv_spec = pl.BlockSpec((None, SB_W, rows), col)
    else:
        kv_shape = jax.ShapeDtypeStruct((b, t, SB_W), F32)
        kv_spec = pl.BlockSpec((None, rows, SB_W), row)
    out_shape = (
        jax.ShapeDtypeStruct((b, t, RWKV_COLS), F32),
        jax.ShapeDtypeStruct((b, t, SB_W), F32),
        jax.ShapeDtypeStruct((b, t, X_W), F32),
        jax.ShapeDtypeStruct((b, t, n_gate), F32),
        kv_shape, kv_shape,
    )
    return pl.pallas_call(
        functools.partial(_in_proj_kernel, kv_feature_major=kv_feature_major),
        grid=(b, t // rows),
        in_specs=[
            pl.BlockSpec((None, rows, d), row),
            pl.BlockSpec((1, d), const),
            pl.BlockSpec((d, n_in), const, pipeline_mode=pl.Buffered(1)),
        ],
        out_specs=(
            pl.BlockSpec((None, rows, RWKV_COLS), row),
            pl.BlockSpec((None, rows, SB_W), row),
            pl.BlockSpec((None, rows, X_W), row),
            pl.BlockSpec((None, rows, n_gate), row),
            kv_spec, kv_spec,
        ),
        out_shape=out_shape,
        compiler_params=_params("parallel", "parallel"),
        name="in_proj",
    )(x, norm_g, w16)


def _mem_kv_kernel(mem_ref, g_ref, wkT_ref, wvT_ref, kT_ref, vT_ref):
    h = _rmsnorm_rows(mem_ref[...], g_ref[...]).astype(BF16)
    kT_ref[...] = _dot(wkT_ref[...], h, NT)
    vT_ref[...] = _dot(wvT_ref[...], h, NT)


def _mem_kv(mem, g, wkT, wvT):
    b, m, d = mem.shape
    const = lambda bi: (0, 0)
    blk = lambda bi: (bi, 0, 0)
    return pl.pallas_call(
        _mem_kv_kernel,
        grid=(b,),
        in_specs=[
            pl.BlockSpec((None, m, d), blk),
            pl.BlockSpec((1, d), const),
            pl.BlockSpec((X_W, d), const),
            pl.BlockSpec((X_W, d), const),
        ],
        out_specs=(pl.BlockSpec((None, X_W, m), blk), pl.BlockSpec((None, X_W, m), blk)),
        out_shape=(jax.ShapeDtypeStruct((b, X_W, m), F32),) * 2,
        compiler_params=_params("parallel"),
        name="mem_kv",
    )(mem, g, wkT, wvT)


def _xattn_rows(xq, kT_ref, vT_ref):
    heads = range(X_HEADS)
    sl = [slice(h * HEAD_DIM, (h + 1) * HEAD_DIM) for h in heads]
    s = [_dot((xq[:, sl[h]] * ATTN_SCALE).astype(BF16), kT_ref[sl[h], :].astype(BF16)) for h in heads]
    e = [jnp.exp(s[h] - jnp.max(s[h], axis=-1, keepdims=True)) for h in heads]
    l = [jnp.sum(e[h], axis=-1, keepdims=True) for h in heads]
    o = [_dot(e[h].astype(BF16), vT_ref[sl[h], :].astype(BF16), NT) / l[h] for h in heads]
    return jnp.concatenate(o, axis=1)


def _xattn_kernel(xq_ref, kT_ref, vT_ref, o_ref):
    for g in range(xq_ref.shape[0]):
        o_ref[g] = _xattn_rows(xq_ref[g], kT_ref.at[g], vT_ref.at[g])


def _xattn(xq, kT, vT, rows, group):
    b, t, _ = xq.shape
    m = kT.shape[-1]
    row = lambda bi, i: (bi, i, 0)
    full = lambda bi, i: (bi, 0, 0)
    return pl.pallas_call(
        _xattn_kernel,
        grid=(b // group, t // rows),
        in_specs=[
            pl.BlockSpec((group, rows, X_W), row),
            pl.BlockSpec((group, X_W, m), full),
            pl.BlockSpec((group, X_W, m), full),
        ],
        out_specs=pl.BlockSpec((group, rows, X_W), row),
        out_shape=jax.ShapeDtypeStruct((b, t, X_W), F32),
        compiler_params=_params("parallel", "parallel"),
        name="xattn",
    )(xq, kT, vT)


def _out_proj_kernel(orw_ref, osb_ref, x3_ref, gate_ref, x_ref, w_ref, fg_ref, *rest, fused_xattn):
    if fused_xattn:
        kT_ref, vT_ref, y_ref = rest
        o_x = _xattn_rows(x3_ref[...], kT_ref, vT_ref)
    else:
        (y_ref,) = rest
        o_x = x3_ref[...]
    g = gate_ref[...]
    sg = g * _sigmoid(g)
    a0, a1 = RWKV_W, RWKV_W + SB_W
    acc = _dot((orw_ref[...] * sg[:, :a0]).astype(BF16), w_ref[:a0, :])
    acc += _dot((osb_ref[...] * sg[:, a0:a1]).astype(BF16), w_ref[a0:a1, :])
    acc += _dot((o_x * sg[:, a1:]).astype(BF16), w_ref[a1:, :])
    y_ref[...] = _rmsnorm_rows(x_ref[...] + acc, fg_ref[...])


def _out_proj(o_rw, o_sb, x3, gate, x, w_out, fg, rows, mem_kv=None):
    b, t, d = x.shape
    dm = w_out.shape[0]
    row = lambda bi, i: (bi, i, 0)
    const = lambda bi, i: (0, 0)
    in_specs = [
        pl.BlockSpec((None, rows, RWKV_W), row),
        pl.BlockSpec((None, rows, SB_W), row),
        pl.BlockSpec((None, rows, X_W), row),
        pl.BlockSpec((None, rows, dm), row),
        pl.BlockSpec((None, rows, d), row),
        pl.BlockSpec((dm, d), const),
        pl.BlockSpec((1, d), const),
    ]
    operands = [o_rw, o_sb, x3, gate, x, w_out, fg]
    if mem_kv is not None:
        m = mem_kv[0].shape[-1]
        in_specs += [pl.BlockSpec((None, X_W, m), lambda bi, i: (bi, 0, 0))] * 2
        operands += list(mem_kv)
    return pl.pallas_call(
        functools.partial(_out_proj_kernel, fused_xattn=mem_kv is not None),
        grid=(b, t // rows),
        in_specs=in_specs,
        out_specs=pl.BlockSpec((None, rows, d), row),
        out_shape=jax.ShapeDtypeStruct((b, t, d), F32),
        compiler_params=_params("parallel", "parallel"),
        name="out_proj",
    )(*operands)


def _rwkv_token_math(r, k, v, xw, xa, w0, a0, k_k, k_a, r_k, wlb, alb, head_ones):
    log_w = -DECAY_SCALE * _sigmoid(w0 + _dot(jnp.tanh(xw).astype(BF16), wlb.astype(BF16)))
    a = _sigmoid(a0 + _dot(xa.astype(BF16), alb.astype(BF16)))
    kk = k * k_k
    ss = _dot_exact_rhs(kk * kk, head_ones)
    kk = kk * lax.rsqrt(jnp.maximum(ss, 1e-12))
    k2 = k * (1.0 + (a - 1.0) * k_a)
    bonus = _dot_exact_rhs(r * k2 * r_k, head_ones) * v
    return log_w, a, kk, k2, bonus


def _group_norm(o, g, b, head_ones):
    inv = 1.0 / HEAD_DIM
    mean = _dot_exact_rhs(o, head_ones) * inv
    d = o - mean
    var = _dot_exact_rhs(d * d, head_ones) * inv
    return d * lax.rsqrt(var + GN_EPS) * g + b


def _shift_mix(cur, last_row, mu, first_row_mask):
    prev = jnp.where(first_row_mask, last_row, pltpu.roll(cur, 1, axis=0))
    return cur + (prev - cur) * mu


def _rwkv_prompt_kernel(
        p_ref, prev_ref, first_ref, mu_ref,
        w0_ref, a0_ref, kk_ref, ka_ref, rk_ref, lg_ref, lb_ref, wlb_ref, alb_ref,
        h0_ref, tri_ref, ones_ref,
        o_ref, hfin_ref, h_scr, *, n_sub):
    i = pl.program_id(1)
    c = RWKV_CHUNK
    rows = c * n_sub
    w3 = 3 * RWKV_W

    @pl.when(i == 0)
    def _():
        h_scr[...] = h0_ref[...]

    first = lax.broadcasted_iota(jnp.int32, (rows, RWKV_COLS), 0) == 0
    last = jnp.where(i == 0, first_ref[...], prev_ref[7:8, :])
    xs = _shift_mix(p_ref[...], last, mu_ref[...], first)
    r, k, v = xs[:, :RWKV_W], xs[:, RWKV_W:2 * RWKV_W], xs[:, 2 * RWKV_W:w3]
    head_ones = ones_ref[...]
    log_w, a, kk, k2, bonus = _rwkv_token_math(
        r, k, v, xs[:, w3:w3 + LORA], xs[:, w3 + LORA:], w0_ref[...], a0_ref[...], kk_ref[...],
        ka_ref[...], rk_ref[...], wlb_ref[...], alb_ref[...], head_ones)

    subs = range(n_sub)
    lam = _dot_exact_lhs(tri_ref[...], log_w)
    lam_ends = [lam[(sc + 1) * c - 1:(sc + 1) * c, :] for sc in subs]
    lam_c = jnp.concatenate([jnp.broadcast_to(le, (c, RWKV_W)) for le in lam_ends], axis=0)
    w_in = jnp.exp(lam)
    w_ex = jnp.exp(lam - log_w)
    w_inv = jnp.exp(-lam)
    w_end = jnp.exp(lam_c - lam)
    w_c = [jnp.exp(le) for le in lam_ends]
    bvec = kk * a
    at = -kk * w_ex
    bt = bvec * w_inv
    kt = k2 * w_inv
    rt = r * w_in
    bh = bvec * w_end
    kh = k2 * w_end

    colid = lax.broadcasted_iota(jnp.int32, (c, c), 1)
    rowid = lax.broadcasted_iota(jnp.int32, (c, c), 0)
    strict = colid < rowid
    incl = colid <= rowid
    eye64 = (lax.broadcasted_iota(jnp.int32, (HEAD_DIM, HEAD_DIM), 0)
             == lax.broadcasted_iota(jnp.int32, (HEAD_DIM, HEAD_DIM), 1))

    heads = range(RWKV_HEADS)
    chains = [(sc, hh) for sc in subs for hh in heads]
    cut = lambda x: {(sc, hh): x[sc * c:(sc + 1) * c, hh * HEAD_DIM:(hh + 1) * HEAD_DIM]
                     for sc, hh in chains}
    at_h, bt_h, kt_h, rt_h, v_h, bh_h, kh_h = map(cut, (at, bt, kt, rt, v, bh, kh))
    each = lambda fn: {ch: fn(ch) for ch in chains}
    scores = each(lambda ch: _dot_f32(
        jnp.concatenate([at_h[ch], rt_h[ch]], axis=0),
        jnp.concatenate([bt_h[ch], kt_h[ch]], axis=0), NT))
    n = each(lambda ch: jnp.where(strict, scores[ch][:c, :c], 0.0))
    aak = each(lambda ch: jnp.where(strict, scores[ch][:c, c:], 0.0))
    mrb = each(lambda ch: jnp.where(incl, scores[ch][c:, :c], 0.0))
    mrk = each(lambda ch: jnp.where(incl, scores[ch][c:, c:], 0.0))
    rb, cb = rowid // INV_BASE, colid // INV_BASE
    dg = each(lambda ch: jnp.where(rb == cb, n[ch], 0.0))
    tm = each(lambda ch: jnp.where(colid == rowid, 1.0, dg[ch]))
    span = 1
    while 2 * span < INV_BASE:
        dg = each(lambda ch: _dot_bf16(dg[ch], dg[ch]))
        tm = each(lambda ch: tm[ch] + _dot_bf16(tm[ch], dg[ch]))
        span *= 2
    m = INV_BASE
    while m < c:
        off = (rowid // (2 * m) == colid // (2 * m)) & (rowid // m != colid // m)
        tn = each(lambda ch: _dot_bf16(tm[ch], jnp.where(off, n[ch], 0.0)))
        tm = each(lambda ch: tm[ch] + _dot_bf16(tn[ch], tm[ch]))
        m *= 2
    akv = each(lambda ch: _dot_f32(aak[ch], v_h[ch]))
    mkv = each(lambda ch: _dot_f32(mrk[ch], v_h[ch]))
    khv = each(lambda ch: _dot_f32(kh_h[ch], v_h[ch], TN))
    pq = each(lambda ch: _dot_f32(tm[ch], jnp.concatenate([at_h[ch], akv[ch]], axis=1)))
    mpq = each(lambda ch: _dot_f32(mrb[ch], pq[ch]))
    gj = each(lambda ch: _dot_f32(bh_h[ch], pq[ch], TN))
    p2 = each(lambda ch: rt_h[ch] + mpq[ch][:, :HEAD_DIM])
    q2 = each(lambda ch: mpq[ch][:, HEAD_DIM:] + mkv[ch])
    g = each(lambda ch: gj[ch][:, :HEAD_DIM] + jnp.where(
        eye64, w_c[ch[0]][:, ch[1] * HEAD_DIM:(ch[1] + 1) * HEAD_DIM], 0.0))
    jm = each(lambda ch: gj[ch][:, HEAD_DIM:] + khv[ch])

    state = [h_scr[hh] for hh in heads]
    outs = []
    for sc in subs:
        outs.append(jnp.concatenate(
            [_dot_f32(p2[(sc, hh)], state[hh]) + q2[(sc, hh)] for hh in heads], axis=1))
        state = [_dot_f32(g[(sc, hh)], state[hh]) + jm[(sc, hh)] for hh in heads]
    for hh in heads:
        h_scr[hh] = state[hh]

    o = jnp.concatenate(outs, axis=0)
    o_ref[...] = _group_norm(o, lg_ref[...], lb_ref[...], head_ones) + bonus

    @pl.when(i == pl.num_programs(1) - 1)
    def _():
        hfin_ref[...] = h_scr[...]


def _rwkv_prompt_specs(b, t, n_sub, index):
    c = RWKV_CHUNK * n_sub
    assert t % c == 0, "sequence length must be a multiple of the tokens per step"
    sub = 8
    const = index(lambda bi, i: (0, 0))
    hvec = pl.BlockSpec((1, RWKV_W), const)
    lora = pl.BlockSpec((LORA, RWKV_W), const)
    state = pl.BlockSpec((None, RWKV_HEADS, HEAD_DIM, HEAD_DIM), index(lambda bi, i: (bi, 0, 0, 0)))
    in_specs = [
        pl.BlockSpec((None, c, RWKV_COLS), index(lambda bi, i: (bi, i, 0))),
        pl.BlockSpec((None, sub, RWKV_COLS),
                     index(lambda bi, i: (bi, jnp.maximum(i * (c // sub) - 1, 0), 0))),
        pl.BlockSpec((None, 1, RWKV_COLS), index(lambda bi, i: (bi, 0, 0))),
        pl.BlockSpec((1, RWKV_COLS), const),
        hvec, hvec, hvec, hvec, hvec, hvec, hvec, lora, lora,
        state,
        pl.BlockSpec((c, c), const),
        pl.BlockSpec((RWKV_W, RWKV_W), const),
    ]
    out_specs = [pl.BlockSpec((None, c, RWKV_W), index(lambda bi, i: (bi, i, 0))), state]
    out_shape = [jax.ShapeDtypeStruct((b, t, RWKV_W), F32),
                 jax.ShapeDtypeStruct((b, RWKV_HEADS, HEAD_DIM, HEAD_DIM), F32)]
    scratch = [pltpu.VMEM((RWKV_HEADS, HEAD_DIM, HEAD_DIM), F32)]
    return in_specs, out_specs, out_shape, scratch


def _chunk_tri(n_sub):
    c = RWKV_CHUNK
    tok = jnp.arange(c * n_sub)
    return ((tok[None, :] <= tok[:, None]) & (tok[None, :] // c == tok[:, None] // c)).astype(BF16)


def _rwkv_prompt(p_rw, prev0, h0, vecs, wlb, alb, ones):
    b, t, _ = p_rw.shape
    mu, w0, a0, k_k, k_a, r_k, lnx_g, lnx_b = vecs
    in_specs, out_specs, out_shape, scratch = _rwkv_prompt_specs(b, t, RWKV_SUB, lambda fn: fn)
    return pl.pallas_call(
        functools.partial(_rwkv_prompt_kernel, n_sub=RWKV_SUB),
        grid=(b, t // (RWKV_CHUNK * RWKV_SUB)),
        in_specs=in_specs,
        out_specs=tuple(out_specs),
        out_shape=tuple(out_shape),
        scratch_shapes=scratch,
        compiler_params=_params("parallel", "arbitrary"),
        name="rwkv_prompt",
    )(p_rw, p_rw, prev0, mu, w0, a0, k_k, k_a, r_k, lnx_g, lnx_b, wlb, alb, h0,
      _chunk_tri(RWKV_SUB), ones)


def _rwkv_sample_prep_kernel(p_ref, prev_ref, mu_ref, w0_ref, a0_ref, kk_ref, ka_ref, rk_ref,
                             wlb_ref, alb_ref, ones_ref,
                             r_ref, w_ref, k2_ref, v_ref, kkn_ref, b_ref, bonus_ref):
    pf = p_ref[...]
    xs = pf + (prev_ref[...] - pf) * mu_ref[...]
    w3 = 3 * RWKV_W
    r, k, v = xs[:, :RWKV_W], xs[:, RWKV_W:2 * RWKV_W], xs[:, 2 * RWKV_W:w3]
    log_w, a, kk, k2, bonus = _rwkv_token_math(
        r, k, v, xs[:, w3:w3 + LORA], xs[:, w3 + LORA:], w0_ref[...], a0_ref[...], kk_ref[...],
        ka_ref[...], rk_ref[...], wlb_ref[...], alb_ref[...], ones_ref[...])
    r_ref[...] = r
    w_ref[...] = jnp.exp(log_w)
    k2_ref[...] = k2
    v_ref[...] = v
    kkn_ref[...] = kk
    b_ref[...] = kk * a
    bonus_ref[...] = bonus


def _rwkv_sample_prep(p, prev, vecs, wlb, alb, ones):
    n = p.shape[0]
    mu, w0, a0, k_k, k_a, r_k = vecs
    return pl.pallas_call(
        _rwkv_sample_prep_kernel,
        out_shape=(jax.ShapeDtypeStruct((n, RWKV_W), F32),) * 7,
        name="rwkv_sample_prep",
    )(p, prev, mu, w0, a0, k_k, k_a, r_k, wlb, alb, ones)


def _rwkv_sample_step_kernel(s_ref, r_ref, w_ref, k_ref, kk_ref, b_ref, v_ref, bonus_ref,
                             g_ref, beta_ref, o_ref, snew_ref):
    eye = (lax.broadcasted_iota(jnp.int32, (HEAD_DIM, HEAD_DIM), 0)
           == lax.broadcasted_iota(jnp.int32, (HEAD_DIM, HEAD_DIM), 1))
    to_col = lambda row: jnp.sum(jnp.where(eye, row, 0.0), axis=3, keepdims=True)
    to_row = lambda col: jnp.sum(jnp.where(eye, col, 0.0), axis=2, keepdims=True)
    s = s_ref[...]
    s_kk = jnp.sum(s * kk_ref[...], axis=3, keepdims=True)
    s = s * w_ref[...] - s_kk * b_ref[...] + to_col(v_ref[...]) * k_ref[...]
    snew_ref[...] = s
    o = to_row(jnp.sum(s * r_ref[...], axis=3, keepdims=True))
    mean = jnp.mean(o, axis=3, keepdims=True)
    d = o - mean
    var = jnp.mean(d * d, axis=3, keepdims=True)
    o_ref[...] = d * lax.rsqrt(var + GN_EPS) * g_ref[...] + beta_ref[...] + bonus_ref[...]


def _rwkv_sample_step(s0, rows, g_row, beta_row, group):
    n = s0.shape[0]
    h, d = RWKV_HEADS, HEAD_DIM
    st = pl.BlockSpec((group, h, d, d), lambda bi: (bi, 0, 0, 0))
    rowspec = pl.BlockSpec((group, h, 1, d), lambda bi: (bi, 0, 0, 0))
    cconst = pl.BlockSpec((h, 1, d), lambda bi: (0, 0, 0))
    return pl.pallas_call(
        _rwkv_sample_step_kernel,
        grid=(n // group,),
        in_specs=[st] + [rowspec] * 7 + [cconst] * 2,
        out_specs=(rowspec, st),
        out_shape=(jax.ShapeDtypeStruct((n, h, 1, d), F32),
                   jax.ShapeDtypeStruct((n, h, d, d), F32)),
        compiler_params=_params("parallel"),
        name="rwkv_sample_step",
    )(s0, *rows, g_row, beta_row)


def _sb_prompt_kernel(bias_ref, q_ref, kT_ref, vT_ref, cs_ref, o_ref):
    pair = pl.program_id(1)
    qi = pl.program_id(2)
    blk = SB_BLOCK
    cs_mat = cs_ref[...]
    q_all = q_ref[...] * ATTN_SCALE
    valid = (lax.broadcasted_iota(jnp.int32, (blk, blk), 1)
             < lax.broadcasted_iota(jnp.int32, (blk, blk), 0))

    heads = range(2)
    rows = [slice(hh * HEAD_DIM, (hh + 1) * HEAD_DIM) for hh in heads]
    lane = lax.broadcasted_iota(jnp.int32, (blk, HEAD_DIM), 1)
    ones2 = jnp.where(lane < 2, 1.0, 0.0)
    krow = lax.broadcasted_iota(jnp.int32, (HEAD_DIM, blk), 0)
    bias_rows = []
    for hh in heads:
        b_hi, b_lo = _split2(jnp.full((HEAD_DIM, blk), bias_ref[2 * pair + hh], F32))
        bias_rows.append(
            jnp.where(krow == 0, b_hi.astype(F32), jnp.where(krow == 1, b_lo.astype(F32), 0.0))
            .astype(BF16))
    chains = [(hh, u) for u in range(SB_Q_SUB) for hh in heads]
    qs = {(hh, u): jnp.concatenate([q_all[u * blk:(u + 1) * blk, rows[hh]], ones2], axis=1).astype(BF16)
          for hh, u in chains}

    def run(blocks, items, state):
        starts = [pl.multiple_of(j * blk, blk) for j in blocks]
        carries, accs = dict(zip(chains, state[0])), dict(zip(chains, state[1]))
        k16, v16, tmp = {}, {}, {}

        def stage(s, it):
            bid, ch, diag = items[it]
            hh = ch[0]
            if s == 0:
                if (bid, hh) not in k16:
                    k16[bid, hh] = jnp.concatenate(
                        [kT_ref[rows[hh], pl.ds(starts[bid], blk)].astype(BF16), bias_rows[hh]], axis=0)
                tmp[it] = {"z": _dot(qs[ch], k16[bid, hh])}
            elif s == 1:
                t = tmp[it]
                sp, t["zs"] = _softplus_pair(t.pop("z"))
                if diag:
                    sp = jnp.where(valid, sp, 0.0)
                t["first"] = sp[:, 0:1]
                t["parts"] = sp.astype(BF16)
            elif s == 2:
                t = tmp[it]
                t["cs"] = _dot(t.pop("parts"), cs_mat)
            elif s == 3:
                t = tmp[it]
                cs = t.pop("cs")
                a = jnp.exp(t.pop("zs") - cs - carries[ch])
                if diag:
                    a = jnp.where(valid, a, 0.0)
                t["a"] = a.astype(BF16)
                carries[ch] = carries[ch] + (cs[:, 0:1] + t.pop("first"))
            else:
                if (bid, hh) not in v16:
                    v16[bid, hh] = vT_ref[rows[hh], pl.ds(starts[bid], blk)].astype(BF16)
                accs[ch] = accs[ch] + _dot(tmp.pop(it)["a"], v16[bid, hh], NT)

        n_stage = 5
        for wave in range(len(items) + n_stage - 1):
            for s in reversed(range(n_stage)):
                if 0 <= wave - s < len(items):
                    stage(s, wave - s)
        return (tuple(carries[ch] for ch in chains), tuple(accs[ch] for ch in chains))

    zeros_c = jnp.zeros((blk, 1), F32)
    zeros_o = jnp.zeros((blk, HEAD_DIM), F32)
    state = ((zeros_c,) * len(chains), (zeros_o,) * len(chains))
    top = SB_Q_SUB * qi
    order = list(reversed(range(SB_Q_SUB)))
    state = run([top + d for d in order],
                [(i, ch, ch[1] == d) for i, d in enumerate(order) for ch in chains if ch[1] >= d],
                state)

    def body(jj, st):
        first = top - 1 - jj * SB_Q_SUB
        return run([first - r for r in range(SB_Q_SUB)],
                   [(r, ch, False) for r in range(SB_Q_SUB) for ch in chains], st)

    state = lax.fori_loop(0, qi, body, state)
    accs = dict(zip(chains, state[1]))
    o_ref[...] = jnp.concatenate(
        [jnp.concatenate([accs[(hh, u)] for hh in heads], axis=1) for u in range(SB_Q_SUB)], axis=0)


def _sb_prompt(q, kT, vT, bias, cs_mat):
    b, t, _ = q.shape
    blk = SB_BLOCK * SB_Q_SUB
    assert t % blk == 0, "sequence length must be a multiple of the query rows per step"
    npair = SB_HEADS // 2
    kv = pl.BlockSpec((None, PAIR_W, t), lambda bi, p, i: (bi, p, 0))
    qo = pl.BlockSpec((None, blk, PAIR_W), lambda bi, p, i: (bi, i, p))
    return pl.pallas_call(
        _sb_prompt_kernel,
        grid=(b, npair, t // blk),
        in_specs=[
            pl.BlockSpec(memory_space=pltpu.SMEM),
            qo, kv, kv,
            pl.BlockSpec((SB_BLOCK, SB_BLOCK), lambda bi, p, i: (0, 0)),
        ],
        out_specs=qo,
        out_shape=jax.ShapeDtypeStruct((b, t, SB_W), F32),
        compiler_params=_params("parallel", "parallel", "arbitrary"),
        name="sb_prompt",
    )(bias, q, kT, vT, cs_mat)


def _sb_sample_pieces(k_refs, v_refs, q_ref, bias_ref, sfx_ref, pages_ref, o_ref, acc_ref, carry_ref,
                      first_step, last_step):
    npg = len(k_refs)
    heads = range(SB_HEADS)
    nrow = SB_HEADS * npg
    pad = -nrow % LANES
    st = {}

    def init():
        @pl.when(first_step)
        def _():
            acc_ref[...] = jnp.zeros_like(acc_ref)
            carry_ref[...] = jnp.zeros_like(carry_ref)

    def score_row(h, u):
        st["z", h, u] = jnp.sum(k_refs[u][h] * q_ref[h], axis=0, keepdims=True) + bias_ref[h]

    def weights():
        zpad = [jnp.zeros((pad, LANES), F32)] if pad else []
        z = jnp.concatenate([st.pop(("z", h, u)) for h in heads for u in range(npg)] + zpad, axis=0)
        sp, _ = _softplus_pair(z)
        sums = _dot_exact_rhs(sp, sfx_ref[...])
        suffix, tot = sums[:, :LANES], sums[:, LANES:]
        later = _dot_exact_lhs(pages_ref[...], tot)
        carry = jnp.concatenate(
            [jnp.broadcast_to(carry_ref[h], (npg, LANES)) for h in heads] + zpad, axis=0)
        st["a"] = jnp.exp(z - suffix - later - carry)
        st["done"] = later + tot

    def value_row(h, u):
        r = h * npg + u
        acc = acc_ref[h] if u == 0 else st.pop(("acc", h))
        acc = acc + v_refs[u][h] * st["a"][r:r + 1, :]
        if u < npg - 1:
            st["acc", h] = acc
        else:
            acc_ref[h] = acc
            carry_ref[h] = carry_ref[h] + st["done"][r:r + 1, :]

    def finish():
        @pl.when(last_step)
        def _():
            o_ref[...] = jnp.sum(acc_ref[...], axis=2)

    pairs = [(h, u) for h in heads for u in range(npg)]
    return ([init] + [functools.partial(score_row, h, u) for h, u in pairs] + [weights]
            + [functools.partial(value_row, h, u) for h, u in pairs] + [finish])


def _sb_sample_kernel(pt_ref, k_hbm, v_hbm, q_ref, bias_ref, sfx_ref, pages_ref, o_ref,
                      kbuf, vbuf, sem, acc_ref, carry_ref, *, npg, depth):
    n_seq, n_steps = pl.num_programs(0), pl.num_programs(1)
    j = pl.program_id(1)
    g = pl.program_id(0) * n_steps + j
    total = n_seq * n_steps
    last_page = n_steps * npg - 1

    def page_copies(step, u, page):
        slot = step % depth
        return (pltpu.make_async_copy(k_hbm.at[page], kbuf.at[slot, u], sem.at[slot, 0]),
                pltpu.make_async_copy(v_hbm.at[page], vbuf.at[slot, u], sem.at[slot, 1]))

    def start_step(step):
        seq, sj = step // n_steps, step % n_steps
        for u in range(npg):
            for cp in page_copies(step, u, pt_ref[seq, last_page - (sj * npg + u)]):
                cp.start()

    @pl.when(g == 0)
    def _():
        for s in range(depth - 1):
            @pl.when(s < total)
            def _():
                start_step(jnp.int32(s))

    ahead = g + (depth - 1)

    @pl.when(ahead < total)
    def _():
        start_step(ahead)

    for u in range(npg):
        for cp in page_copies(g, u, 0):
            cp.wait()

    slot = g % depth
    k_refs = [kbuf.at[slot, u] for u in range(npg)]
    v_refs = [vbuf.at[slot, u] for u in range(npg)]
    for piece in _sb_sample_pieces(k_refs, v_refs, q_ref, bias_ref, sfx_ref, pages_ref, o_ref,
                                   acc_ref, carry_ref, j == 0, j == n_steps - 1):
        piece()


def _sb_sample_consts(h, npg, pg):
    assert pg == LANES
    nrow = h * npg + (-(h * npg) % LANES)
    tok = jnp.arange(pg)
    sfx_mat = jnp.concatenate(
        [(tok[:, None] >= tok[None, :]).astype(BF16), jnp.ones((pg, pg), BF16)], axis=1)
    r = jnp.arange(nrow)
    pages_mat = ((r[:, None] // npg == r[None, :] // npg) & (r[None, :] < r[:, None])
                 & (r[:, None] < h * npg)).astype(BF16)
    return sfx_mat, pages_mat


def _sb_sample(page_table, cache_k, cache_v, q_bcast, bias_bcast):
    n, n_pages = page_table.shape
    npg = PAGES_PER_STEP
    depth = PAGE_RING
    h, d, pg = cache_k.shape[1:]
    assert n_pages % npg == 0

    sfx_mat, pages_mat = _sb_sample_consts(h, npg, pg)
    const2 = lambda bi, j, pt: (0, 0)
    grid_spec = pltpu.PrefetchScalarGridSpec(
        num_scalar_prefetch=1,
        grid=(n, n_pages // npg),
        in_specs=[
            pl.BlockSpec(memory_space=pl.ANY),
            pl.BlockSpec(memory_space=pl.ANY),
            pl.BlockSpec((None, h, d, pg), lambda bi, j, pt: (bi, 0, 0, 0)),
            pl.BlockSpec((h, 1, pg), lambda bi, j, pt: (0, 0, 0)),
            pl.BlockSpec(sfx_mat.shape, const2),
            pl.BlockSpec(pages_mat.shape, const2),
        ],
        out_specs=pl.BlockSpec((None, h, d), lambda bi, j, pt: (bi, 0, 0)),
        scratch_shapes=[
            pltpu.VMEM((depth, npg, h, d, pg), F32),
            pltpu.VMEM((depth, npg, h, d, pg), F32),
            pltpu.SemaphoreType.DMA((depth, 2)),
            pltpu.VMEM((h, d, pg), F32),
            pltpu.VMEM((h, 1, pg), F32),
        ],
    )
    return pl.pallas_call(
        functools.partial(_sb_sample_kernel, npg=npg, depth=depth),
        grid_spec=grid_spec,
        out_shape=jax.ShapeDtypeStruct((n, h, d), F32),
        compiler_params=_params("arbitrary", "arbitrary"),
        name="sb_sample",
    )(page_table, cache_k, cache_v, q_bcast, bias_bcast, sfx_mat, pages_mat)


def _head_ones(width):
    hid = jnp.arange(width) // HEAD_DIM
    return (hid[:, None] == hid[None, :]).astype(BF16)


def _rows_for(t, target):
    return target if t % target == 0 else t


def kernel(x_prompt, mem_prompt, x_sample, cache_sb_k, cache_sb_v, page_table, state_wkv, state_shift, cache_mem_k, cache_mem_v, norm_g, w_in, sb_bias, mu_shift, w0, w_lora_b, a0, a_lora_b, k_k, k_a, r_k, lnx_g, lnx_b, mem_norm_g, w_mem_k, w_mem_v, w_out, final_norm_g):
    depth = w_in.shape[0]
    assert depth == 1, "single-layer trunk"
    l = 0
    b_p, t_p, d = x_prompt.shape
    b_s = x_sample.shape[0]
    assert x_sample.shape[1] == 1
    n_mem = mem_prompt.shape[1]
    page = cache_sb_k.shape[2]

    w16 = w_in[l].astype(BF16)
    wkT = w_mem_k[l].T.astype(BF16)
    wvT = w_mem_v[l].T.astype(BF16)
    wo = w_out[l].astype(BF16)
    fg = final_norm_g.reshape(1, d)
    r_k_flat = r_k[l].reshape(1, RWKV_W)
    ones_all = _head_ones(RWKV_W)
    blk = SB_BLOCK
    ar = jnp.arange(blk)
    cs_mat = (ar[:, None] > ar[None, :]).astype(BF16)

    mkT, mvT = _mem_kv(mem_prompt, mem_norm_g[l:l + 1], wkT, wvT)
    p_rw, q_sb, xq, gate, kT, vT = _in_proj(
        x_prompt, norm_g[l:l + 1], w16, _rows_for(t_p, 1024), True)
    vecs = (mu_shift[l:l + 1], w0[l:l + 1], a0[l:l + 1], k_k[l:l + 1], k_a[l:l + 1],
            r_k_flat, lnx_g[l:l + 1], lnx_b[l:l + 1])
    o_sb = _sb_prompt(q_sb, kT, vT, sb_bias[l], cs_mat)

    xs2 = x_sample.reshape(1, b_s, d)
    p_rw_s, q_s, xq_s, gate_s, sbk_s, sbv_s = _in_proj(xs2, norm_g[l:l + 1], w16, b_s, False)
    ck = cache_sb_k[l].transpose(0, 2, 3, 1)
    cv = cache_sb_v[l].transpose(0, 2, 3, 1)
    q_b = jnp.broadcast_to(
        (q_s[0] * ATTN_SCALE).reshape(b_s, SB_HEADS, HEAD_DIM, 1), (b_s, SB_HEADS, HEAD_DIM, page))
    bias_b = jnp.broadcast_to(sb_bias[l].reshape(SB_HEADS, 1, 1), (SB_HEADS, 1, page))

    prev0 = jnp.zeros((b_p, 1, RWKV_COLS), F32)
    h0 = jnp.zeros((b_p, RWKV_HEADS, HEAD_DIM, HEAD_DIM), F32)
    o_rw, h_fin = _rwkv_prompt(p_rw, prev0, h0, vecs, w_lora_b[l], a_lora_b[l], ones_all)
    o_sb_s = _sb_sample(page_table, ck, cv, q_b, bias_b).reshape(1, b_s, SB_W)
    y_prompt = _out_proj(o_rw, o_sb, xq, gate, x_prompt, wo, fg, _rows_for(t_p, 512), (mkT, mvT))

    def tokens_major(xT, heads):
        return xT.reshape(1, xT.shape[0], heads, HEAD_DIM, xT.shape[2]).transpose(0, 1, 4, 2, 3)

    new_sb_k_p = tokens_major(kT, SB_HEADS)
    new_sb_v_p = tokens_major(vT, SB_HEADS)
    new_wkv_p = jnp.swapaxes(h_fin, -1, -2)[None]
    new_shift_p = p_rw[:, -1][None]
    new_mem_k_p = tokens_major(mkT, X_HEADS)
    new_mem_v_p = tokens_major(mvT, X_HEADS)

    p_rw_s = p_rw_s[0]
    svecs = (mu_shift[l:l + 1], w0[l:l + 1], a0[l:l + 1], k_k[l:l + 1], k_a[l:l + 1], r_k_flat)
    r_s, w_s, k2_s, v_s, kk_s, bv_s, bonus_s = _rwkv_sample_prep(
        p_rw_s, state_shift[l], svecs, w_lora_b[l], a_lora_b[l], ones_all)
    as_row = lambda a_: a_.reshape(b_s, RWKV_HEADS, 1, HEAD_DIM)
    o_rw_s, new_wkv_s = _rwkv_sample_step(
        state_wkv[l],
        [as_row(t_) for t_ in (r_s, w_s, k2_s, kk_s, bv_s, v_s, bonus_s)],
        lnx_g[l].reshape(RWKV_HEADS, 1, HEAD_DIM), lnx_b[l].reshape(RWKV_HEADS, 1, HEAD_DIM),
        SAMPLE_GROUP if b_s % SAMPLE_GROUP == 0 else 1)
    o_rw_s = o_rw_s.reshape(1, b_s, RWKV_W)

    mk_s = cache_mem_k[l].transpose(0, 2, 3, 1).reshape(b_s, X_W, n_mem)
    mv_s = cache_mem_v[l].transpose(0, 2, 3, 1).reshape(b_s, X_W, n_mem)
    xq_rows = jnp.broadcast_to(xq_s[0][:, None, :], (b_s, 8, X_W))
    o_x_s = _xattn(xq_rows, mk_s, mv_s, 8, SAMPLE_GROUP if b_s % SAMPLE_GROUP == 0 else 1)[:, 0][None]
    y_sample = _out_proj(o_rw_s, o_sb_s, o_x_s, gate_s, xs2, wo, fg, b_s).reshape(b_s, 1, d)

    new_sb_k_s = sbk_s.reshape(1, b_s, 1, SB_HEADS, HEAD_DIM)
    new_sb_v_s = sbv_s.reshape(1, b_s, 1, SB_HEADS, HEAD_DIM)
    new_shift_s = p_rw_s[None]

    return (y_prompt, y_sample, new_sb_k_p, new_sb_v_p, new_wkv_p, new_shift_p,
            new_mem_k_p, new_mem_v_p, new_sb_k_s, new_sb_v_s, new_wkv_s[None], new_shift_s)
```

```python
import functools
import math

import jax
import jax.numpy as jnp
from jax import lax
from jax.experimental import pallas as pl
from jax.experimental.pallas import tpu as pltpu

F32 = jnp.float32
BF16 = jnp.bfloat16

HEAD_DIM = 64
RWKV_HEADS = 6
SB_HEADS = 6
X_HEADS = 4
RWKV_W = RWKV_HEADS * HEAD_DIM
SB_W = SB_HEADS * HEAD_DIM
X_W = X_HEADS * HEAD_DIM
LORA = 64
RWKV_COLS = 3 * RWKV_W + 2 * LORA
NORM_EPS = 1e-6
GN_EPS = 64e-5
DECAY_SCALE = math.exp(-0.5)
ATTN_SCALE = HEAD_DIM ** -0.5
LOG2E = math.log2(math.e)

LANES = 128
PAIR_W = 2 * HEAD_DIM
RWKV_CHUNK = 128
RWKV_SUB = 2
INV_BASE = 8
SB_BLOCK = 256
SB_Q_SUB = 4
PAGES_PER_STEP = 16
PAGE_RING = 3
SAMPLE_GROUP = 8
VMEM_LIMIT = 48 * 1024 * 1024

NN = (((1,), (0,)), ((), ()))
NT = (((1,), (1,)), ((), ()))
TN = (((0,), (0,)), ((), ()))


def _dot(a, b, dims=NN):
    return lax.dot_general(a, b, dims, preferred_element_type=F32)


def _split2(x):
    hi = x.astype(BF16)
    lo = (x - hi.astype(F32)).astype(BF16)
    return hi, lo


def _dot_f32(a, b, dims=NN):
    (ka,), (kb,) = dims[0]
    ah, al = _split2(a)
    bh, bl = _split2(b)
    return _dot(jnp.concatenate([ah, ah, al], axis=ka), jnp.concatenate([bh, bl, bh], axis=kb), dims)


def _dot_rhs16(a, b, dims=NN):
    (ka,), (kb,) = dims[0]
    ah, al = _split2(a)
    bh = b.astype(BF16)
    return _dot(jnp.concatenate([ah, al], axis=ka), jnp.concatenate([bh, bh], axis=kb), dims)


def _dot_bf16(a, b, dims=NN):
    return _dot(a.astype(BF16), b.astype(BF16), dims)


def _dot_exact_rhs(a, b_bf16, dims=NN):
    hi, lo = _split2(a)
    return _dot(hi, b_bf16, dims) + _dot(lo, b_bf16, dims)


def _dot_exact_lhs(a_bf16, b, dims=NN):
    hi, lo = _split2(b)
    return _dot(a_bf16, hi, dims) + _dot(a_bf16, lo, dims)


def _sigmoid(x):
    return 1.0 / (1.0 + jnp.exp(-x))


def _softplus_pair(z):
    sp = jnp.maximum(z, 0.0) + jnp.log(1.0 + jnp.exp2(jnp.abs(z) * -LOG2E))
    return sp, z - sp


def _rmsnorm_rows(x, g):
    ms = jnp.mean(x * x, axis=-1, keepdims=True)
    return x * lax.rsqrt(ms + NORM_EPS) * g


def _params(*sem):
    return pltpu.CompilerParams(dimension_semantics=sem, vmem_limit_bytes=VMEM_LIMIT)


KV_COL0 = RWKV_COLS + SB_W
KV_COL1 = KV_COL0 + 2 * SB_W


def _in_proj_kernel(x_ref, g_ref, w_ref, prw_ref, q_ref, xq_ref, gate_ref, k_ref, v_ref,
                    *, kv_feature_major):
    h = _rmsnorm_rows(x_ref[...], g_ref[...]).astype(BF16)
    pa = _dot(h, w_ref[:, :KV_COL0])
    prw_ref[...] = pa[:, :RWKV_COLS]
    q_ref[...] = pa[:, RWKV_COLS:]
    kv = _dot(h, w_ref[:, KV_COL0:KV_COL1])
    if kv_feature_major:
        k_ref[...] = kv[:, :SB_W].T
        v_ref[...] = kv[:, SB_W:].T
    else:
        k_ref[...] = kv[:, :SB_W]
        v_ref[...] = kv[:, SB_W:]
    pb = _dot(h, w_ref[:, KV_COL1:])
    xq_ref[...] = pb[:, :X_W]
    gate_ref[...] = pb[:, X_W:]


def _in_proj(x, norm_g, w16, rows, kv_feature_major):
    b, t, d = x.shape
    n_in = w16.shape[1]
    n_gate = n_in - KV_COL1 - X_W
    row = lambda bi, i: (bi, i, 0)
    col = lambda bi, i: (bi, 0, i)
    const = lambda bi, i: (0, 0)
    if kv_feature_major:
        kv_shape = jax.ShapeDtypeStruct((b, SB_W, t), F32)
        kv_spec = pl.BlockSpec((None, SB_W, rows), col)
    else:
        kv_shape = jax.ShapeDtypeStruct((b, t, SB_W), F32)
        kv_spec = pl.BlockSpec((None, rows, SB_W), row)
    out_shape = (
        jax.ShapeDtypeStruct((b, t, RWKV_COLS), F32),
        jax.ShapeDtypeStruct((b, t, SB_W), F32),
        jax.ShapeDtypeStruct((b, t, X_W), F32),
        jax.ShapeDtypeStruct((b, t, n_gate), F32),
        kv_shape, kv_shape,
    )
    return pl.pallas_call(
        functools.partial(_in_proj_kernel, kv_feature_major=kv_feature_major),
        grid=(b, t // rows),
        in_specs=[
            pl.BlockSpec((None, rows, d), row),
            pl.BlockSpec((1, d), const),
            pl.BlockSpec((d, n_in), const, pipeline_mode=pl.Buffered(1)),
        ],
        out_specs=(
            pl.BlockSpec((None, rows, RWKV_COLS), row),
            pl.BlockSpec((None, rows, SB_W), row),
            pl.BlockSpec((None, rows, X_W), row),
            pl.BlockSpec((None, rows, n_gate), row),
            kv_spec, kv_spec,
        ),
        out_shape=out_shape,
        compiler_params=_params("parallel", "parallel"),
        name="in_proj",
    )(x, norm_g, w16)


def _mem_kv_kernel(mem_ref, g_ref, wkT_ref, wvT_ref, kT_ref, vT_ref):
    h = _rmsnorm_rows(mem_ref[...], g_ref[...]).astype(BF16)
    kT_ref[...] = _dot(wkT_ref[...], h, NT)
    vT_ref[...] = _dot(wvT_ref[...], h, NT)


def _mem_kv(mem, g, wkT, wvT):
    b, m, d = mem.shape
    const = lambda bi: (0, 0)
    blk = lambda bi: (bi, 0, 0)
    return pl.pallas_call(
        _mem_kv_kernel,
        grid=(b,),
        in_specs=[
            pl.BlockSpec((None, m, d), blk),
            pl.BlockSpec((1, d), const),
            pl.BlockSpec((X_W, d), const),
            pl.BlockSpec((X_W, d), const),
        ],
        out_specs=(pl.BlockSpec((None, X_W, m), blk), pl.BlockSpec((None, X_W, m), blk)),
        out_shape=(jax.ShapeDtypeStruct((b, X_W, m), F32),) * 2,
        compiler_params=_params("parallel"),
        name="mem_kv",
    )(mem, g, wkT, wvT)


def _xattn_rows(xq, kT_ref, vT_ref):
    heads = range(X_HEADS)
    sl = [slice(h * HEAD_DIM, (h + 1) * HEAD_DIM) for h in heads]
    s = [_dot((xq[:, sl[h]] * ATTN_SCALE).astype(BF16), kT_ref[sl[h], :].astype(BF16)) for h in heads]
    e = [jnp.exp(s[h] - jnp.max(s[h], axis=-1, keepdims=True)) for h in heads]
    l = [jnp.sum(e[h], axis=-1, keepdims=True) for h in heads]
    o = [_dot(e[h].astype(BF16), vT_ref[sl[h], :].astype(BF16), NT) / l[h] for h in heads]
    return jnp.concatenate(o, axis=1)


def _xattn_kernel(xq_ref, kT_ref, vT_ref, o_ref):
    for g in range(xq_ref.shape[0]):
        o_ref[g] = _xattn_rows(xq_ref[g], kT_ref.at[g], vT_ref.at[g])


def _xattn(xq, kT, vT, rows, group):
    b, t, _ = xq.shape
    m = kT.shape[-1]
    row = lambda bi, i: (bi, i, 0)
    full = lambda bi, i: (bi, 0, 0)
    return pl.pallas_call(
        _xattn_kernel,
        grid=(b // group, t // rows),
        in_specs=[
            pl.BlockSpec((group, rows, X_W), row),
            pl.BlockSpec((group, X_W, m), full),
            pl.BlockSpec((group, X_W, m), full),
        ],
        out_specs=pl.BlockSpec((group, rows, X_W), row),
        out_shape=jax.ShapeDtypeStruct((b, t, X_W), F32),
        compiler_params=_params("parallel", "parallel"),
        name="xattn",
    )(xq, kT, vT)


def _out_proj_kernel(orw_ref, osb_ref, x3_ref, gate_ref, x_ref, w_ref, fg_ref, *rest, fused_xattn):
    if fused_xattn:
        kT_ref, vT_ref, y_ref = rest
        o_x = _xattn_rows(x3_ref[...], kT_ref, vT_ref)
    else:
        (y_ref,) = rest
        o_x = x3_ref[...]
    g = gate_ref[...]
    sg = g * _sigmoid(g)
    a0, a1 = RWKV_W, RWKV_W + SB_W
    acc = _dot((orw_ref[...] * sg[:, :a0]).astype(BF16), w_ref[:a0, :])
    acc += _dot((osb_ref[...] * sg[:, a0:a1]).astype(BF16), w_ref[a0:a1, :])
    acc += _dot((o_x * sg[:, a1:]).astype(BF16), w_ref[a1:, :])
    y_ref[...] = _rmsnorm_rows(x_ref[...] + acc, fg_ref[...])


def _out_proj(o_rw, o_sb, x3, gate, x, w_out, fg, rows, mem_kv=None):
    b, t, d = x.shape
    dm = w_out.shape[0]
    row = lambda bi, i: (bi, i, 0)
    const = lambda bi, i: (0, 0)
    in_specs = [
        pl.BlockSpec((None, rows, RWKV_W), row),
        pl.BlockSpec((None, rows, SB_W), row),
        pl.BlockSpec((None, rows, X_W), row),
        pl.BlockSpec((None, rows, dm), row),
        pl.BlockSpec((None, rows, d), row),
        pl.BlockSpec((dm, d), const),
        pl.BlockSpec((1, d), const),
    ]
    operands = [o_rw, o_sb, x3, gate, x, w_out, fg]
    if mem_kv is not None:
        m = mem_kv[0].shape[-1]
        in_specs += [pl.BlockSpec((None, X_W, m), lambda bi, i: (bi, 0, 0))] * 2
        operands += list(mem_kv)
    return pl.pallas_call(
        functools.partial(_out_proj_kernel, fused_xattn=mem_kv is not None),
        grid=(b, t // rows),
        in_specs=in_specs,
        out_specs=pl.BlockSpec((None, rows, d), row),
        out_shape=jax.ShapeDtypeStruct((b, t, d), F32),
        compiler_params=_params("parallel", "parallel"),
        name="out_proj",
    )(*operands)


def _rwkv_token_math(r, k, v, xw, xa, w0, a0, k_k, k_a, r_k, wlb, alb, head_ones):
    log_w = -DECAY_SCALE * _sigmoid(w0 + _dot(jnp.tanh(xw).astype(BF16), wlb.astype(BF16)))
    a = _sigmoid(a0 + _dot(xa.astype(BF16), alb.astype(BF16)))
    kk = k * k_k
    ss = _dot_exact_rhs(kk * kk, head_ones)
    kk = kk * lax.rsqrt(jnp.maximum(ss, 1e-12))
    k2 = k * (1.0 + (a - 1.0) * k_a)
    bonus = _dot_exact_rhs(r * k2 * r_k, head_ones) * v
    return log_w, a, kk, k2, bonus


def _group_norm(o, g, b, head_ones):
    inv = 1.0 / HEAD_DIM
    mean = _dot_exact_rhs(o, head_ones) * inv
    d = o - mean
    var = _dot_exact_rhs(d * d, head_ones) * inv
    return d * lax.rsqrt(var + GN_EPS) * g + b


def _shift_mix(cur, last_row, mu, first_row_mask):
    prev = jnp.where(first_row_mask, last_row, pltpu.roll(cur, 1, axis=0))
    return cur + (prev - cur) * mu


def _rwkv_prompt_kernel(
        p_ref, prev_ref, first_ref, mu_ref,
        w0_ref, a0_ref, kk_ref, ka_ref, rk_ref, lg_ref, lb_ref, wlb_ref, alb_ref,
        h0_ref, tri_ref, ones_ref,
        o_ref, hfin_ref, h_scr, *, n_sub):
    i = pl.program_id(1)
    c = RWKV_CHUNK
    rows = c * n_sub
    w3 = 3 * RWKV_W

    @pl.when(i == 0)
    def _():
        h_scr[...] = h0_ref[...]

    first = lax.broadcasted_iota(jnp.int32, (rows, RWKV_COLS), 0) == 0
    last = jnp.where(i == 0, first_ref[...], prev_ref[7:8, :])
    xs = _shift_mix(p_ref[...], last, mu_ref[...], first)
    r, k, v = xs[:, :RWKV_W], xs[:, RWKV_W:2 * RWKV_W], xs[:, 2 * RWKV_W:w3]
    head_ones = ones_ref[...]
    log_w, a, kk, k2, bonus = _rwkv_token_math(
        r, k, v, xs[:, w3:w3 + LORA], xs[:, w3 + LORA:], w0_ref[...], a0_ref[...], kk_ref[...],
        ka_ref[...], rk_ref[...], wlb_ref[...], alb_ref[...], head_ones)

    subs = range(n_sub)
    lam = _dot_exact_lhs(tri_ref[...], log_w)
    lam_ends = [lam[(sc + 1) * c - 1:(sc + 1) * c, :] for sc in subs]
    lam_c = jnp.concatenate([jnp.broadcast_to(le, (c, RWKV_W)) for le in lam_ends], axis=0)
    w_in = jnp.exp(lam)
    w_ex = jnp.exp(lam - log_w)
    w_inv = jnp.exp(-lam)
    w_end = jnp.exp(lam_c - lam)
    w_c = [jnp.exp(le) for le in lam_ends]
    bvec = kk * a
    at = -kk * w_ex
    bt = bvec * w_inv
    kt = k2 * w_inv
    rt = r * w_in
    bh = bvec * w_end
    kh = k2 * w_end

    colid = lax.broadcasted_iota(jnp.int32, (c, c), 1)
    rowid = lax.broadcasted_iota(jnp.int32, (c, c), 0)
    strict = colid < rowid
    incl = colid <= rowid
    eye64 = (lax.broadcasted_iota(jnp.int32, (HEAD_DIM, HEAD_DIM), 0)
             == lax.broadcasted_iota(jnp.int32, (HEAD_DIM, HEAD_DIM), 1))

    heads = range(RWKV_HEADS)
    chains = [(sc, hh) for sc in subs for hh in heads]
    cut = lambda x: {(sc, hh): x[sc * c:(sc + 1) * c, hh * HEAD_DIM:(hh + 1) * HEAD_DIM]
                     for sc, hh in chains}
    at_h, bt_h, kt_h, rt_h, v_h, bh_h, kh_h = map(cut, (at, bt, kt, rt, v, bh, kh))
    each = lambda fn: {ch: fn(ch) for ch in chains}
    scores = each(lambda ch: _dot_f32(
        jnp.concatenate([at_h[ch], rt_h[ch]], axis=0),
        jnp.concatenate([bt_h[ch], kt_h[ch]], axis=0), NT))
    n = each(lambda ch: jnp.where(strict, scores[ch][:c, :c], 0.0))
    aak = each(lambda ch: jnp.where(strict, scores[ch][:c, c:], 0.0))
    mrb = each(lambda ch: jnp.where(incl, scores[ch][c:, :c], 0.0))
    mrk = each(lambda ch: jnp.where(incl, scores[ch][c:, c:], 0.0))
    rb, cb = rowid // INV_BASE, colid // INV_BASE
    dg = each(lambda ch: jnp.where(rb == cb, n[ch], 0.0))
    tm = each(lambda ch: jnp.where(colid == rowid, 1.0, dg[ch]))
    span = 1
    while 2 * span < INV_BASE:
        dg = each(lambda ch: _dot_bf16(dg[ch], dg[ch]))
        tm = each(lambda ch: tm[ch] + _dot_bf16(tm[ch], dg[ch]))
        span *= 2
    m = INV_BASE
    while m < c:
        off = (rowid // (2 * m) == colid // (2 * m)) & (rowid // m != colid // m)
        tn = each(lambda ch: _dot_bf16(tm[ch], jnp.where(off, n[ch], 0.0)))
        tm = each(lambda ch: tm[ch] + _dot_bf16(tn[ch], tm[ch]))
        m *= 2
    akv = each(lambda ch: _dot_rhs16(aak[ch], v_h[ch]))
    mkv = each(lambda ch: _dot_rhs16(mrk[ch], v_h[ch]))
    khv = each(lambda ch: _dot_rhs16(kh_h[ch], v_h[ch], TN))
    pq = each(lambda ch: _dot_rhs16(tm[ch], jnp.concatenate([at_h[ch], akv[ch]], axis=1)))
    mpq = each(lambda ch: _dot_rhs16(mrb[ch], pq[ch]))
    gj = each(lambda ch: _dot_rhs16(bh_h[ch], pq[ch], TN))
    p2 = each(lambda ch: rt_h[ch] + mpq[ch][:, :HEAD_DIM])
    q2 = each(lambda ch: mpq[ch][:, HEAD_DIM:] + mkv[ch])
    g = each(lambda ch: gj[ch][:, :HEAD_DIM] + jnp.where(
        eye64, w_c[ch[0]][:, ch[1] * HEAD_DIM:(ch[1] + 1) * HEAD_DIM], 0.0))
    jm = each(lambda ch: gj[ch][:, HEAD_DIM:] + khv[ch])

    state = [h_scr[hh] for hh in heads]
    outs = []
    for sc in subs:
        outs.append(jnp.concatenate(
            [_dot_f32(p2[(sc, hh)], state[hh]) + q2[(sc, hh)] for hh in heads], axis=1))
        state = [_dot_f32(g[(sc, hh)], state[hh]) + jm[(sc, hh)] for hh in heads]
    for hh in heads:
        h_scr[hh] = state[hh]

    o = jnp.concatenate(outs, axis=0)
    o_ref[...] = _group_norm(o, lg_ref[...], lb_ref[...], head_ones) + bonus

    @pl.when(i == pl.num_programs(1) - 1)
    def _():
        hfin_ref[...] = h_scr[...]


def _rwkv_prompt_specs(b, t, n_sub, index):
    c = RWKV_CHUNK * n_sub
    assert t % c == 0, "sequence length must be a multiple of the tokens per step"
    sub = 8
    const = index(lambda bi, i: (0, 0))
    hvec = pl.BlockSpec((1, RWKV_W), const)
    lora = pl.BlockSpec((LORA, RWKV_W), const)
    state = pl.BlockSpec((None, RWKV_HEADS, HEAD_DIM, HEAD_DIM), index(lambda bi, i: (bi, 0, 0, 0)))
    in_specs = [
        pl.BlockSpec((None, c, RWKV_COLS), index(lambda bi, i: (bi, i, 0))),
        pl.BlockSpec((None, sub, RWKV_COLS),
                     index(lambda bi, i: (bi, jnp.maximum(i * (c // sub) - 1, 0), 0))),
        pl.BlockSpec((None, 1, RWKV_COLS), index(lambda bi, i: (bi, 0, 0))),
        pl.BlockSpec((1, RWKV_COLS), const),
        hvec, hvec, hvec, hvec, hvec, hvec, hvec, lora, lora,
        state,
        pl.BlockSpec((c, c), const),
        pl.BlockSpec((RWKV_W, RWKV_W), const),
    ]
    out_specs = [pl.BlockSpec((None, c, RWKV_W), index(lambda bi, i: (bi, i, 0))), state]
    out_shape = [jax.ShapeDtypeStruct((b, t, RWKV_W), F32),
                 jax.ShapeDtypeStruct((b, RWKV_HEADS, HEAD_DIM, HEAD_DIM), F32)]
    scratch = [pltpu.VMEM((RWKV_HEADS, HEAD_DIM, HEAD_DIM), F32)]
    return in_specs, out_specs, out_shape, scratch


def _chunk_tri(n_sub):
    c = RWKV_CHUNK
    tok = jnp.arange(c * n_sub)
    return ((tok[None, :] <= tok[:, None]) & (tok[None, :] // c == tok[:, None] // c)).astype(BF16)


def _rwkv_prompt(p_rw, prev0, h0, vecs, wlb, alb, ones):
    b, t, _ = p_rw.shape
    mu, w0, a0, k_k, k_a, r_k, lnx_g, lnx_b = vecs
    in_specs, out_specs, out_shape, scratch = _rwkv_prompt_specs(b, t, RWKV_SUB, lambda fn: fn)
    return pl.pallas_call(
        functools.partial(_rwkv_prompt_kernel, n_sub=RWKV_SUB),
        grid=(b, t // (RWKV_CHUNK * RWKV_SUB)),
        in_specs=in_specs,
        out_specs=tuple(out_specs),
        out_shape=tuple(out_shape),
        scratch_shapes=scratch,
        compiler_params=_params("parallel", "arbitrary"),
        name="rwkv_prompt",
    )(p_rw, p_rw, prev0, mu, w0, a0, k_k, k_a, r_k, lnx_g, lnx_b, wlb, alb, h0,
      _chunk_tri(RWKV_SUB), ones)


def _rwkv_sample_prep_kernel(p_ref, prev_ref, mu_ref, w0_ref, a0_ref, kk_ref, ka_ref, rk_ref,
                             wlb_ref, alb_ref, ones_ref,
                             r_ref, w_ref, k2_ref, v_ref, kkn_ref, b_ref, bonus_ref):
    pf = p_ref[...]
    xs = pf + (prev_ref[...] - pf) * mu_ref[...]
    w3 = 3 * RWKV_W
    r, k, v = xs[:, :RWKV_W], xs[:, RWKV_W:2 * RWKV_W], xs[:, 2 * RWKV_W:w3]
    log_w, a, kk, k2, bonus = _rwkv_token_math(
        r, k, v, xs[:, w3:w3 + LORA], xs[:, w3 + LORA:], w0_ref[...], a0_ref[...], kk_ref[...],
        ka_ref[...], rk_ref[...], wlb_ref[...], alb_ref[...], ones_ref[...])
    r_ref[...] = r
    w_ref[...] = jnp.exp(log_w)
    k2_ref[...] = k2
    v_ref[...] = v
    kkn_ref[...] = kk
    b_ref[...] = kk * a
    bonus_ref[...] = bonus


def _rwkv_sample_prep(p, prev, vecs, wlb, alb, ones):
    n = p.shape[0]
    mu, w0, a0, k_k, k_a, r_k = vecs
    return pl.pallas_call(
        _rwkv_sample_prep_kernel,
        out_shape=(jax.ShapeDtypeStruct((n, RWKV_W), F32),) * 7,
        name="rwkv_sample_prep",
    )(p, prev, mu, w0, a0, k_k, k_a, r_k, wlb, alb, ones)


def _rwkv_sample_step_kernel(s_ref, r_ref, w_ref, k_ref, kk_ref, b_ref, v_ref, bonus_ref,
                             g_ref, beta_ref, o_ref, snew_ref):
    eye = (lax.broadcasted_iota(jnp.int32, (HEAD_DIM, HEAD_DIM), 0)
           == lax.broadcasted_iota(jnp.int32, (HEAD_DIM, HEAD_DIM), 1))
    to_col = lambda row: jnp.sum(jnp.where(eye, row, 0.0), axis=3, keepdims=True)
    to_row = lambda col: jnp.sum(jnp.where(eye, col, 0.0), axis=2, keepdims=True)
    s = s_ref[...]
    s_kk = jnp.sum(s * kk_ref[...], axis=3, keepdims=True)
    s = s * w_ref[...] - s_kk * b_ref[...] + to_col(v_ref[...]) * k_ref[...]
    snew_ref[...] = s
    o = to_row(jnp.sum(s * r_ref[...], axis=3, keepdims=True))
    mean = jnp.mean(o, axis=3, keepdims=True)
    d = o - mean
    var = jnp.mean(d * d, axis=3, keepdims=True)
    o_ref[...] = d * lax.rsqrt(var + GN_EPS) * g_ref[...] + beta_ref[...] + bonus_ref[...]


def _rwkv_sample_step(s0, rows, g_row, beta_row, group):
    n = s0.shape[0]
    h, d = RWKV_HEADS, HEAD_DIM
    st = pl.BlockSpec((group, h, d, d), lambda bi: (bi, 0, 0, 0))
    rowspec = pl.BlockSpec((group, h, 1, d), lambda bi: (bi, 0, 0, 0))
    cconst = pl.BlockSpec((h, 1, d), lambda bi: (0, 0, 0))
    return pl.pallas_call(
        _rwkv_sample_step_kernel,
        grid=(n // group,),
        in_specs=[st] + [rowspec] * 7 + [cconst] * 2,
        out_specs=(rowspec, st),
        out_shape=(jax.ShapeDtypeStruct((n, h, 1, d), F32),
                   jax.ShapeDtypeStruct((n, h, d, d), F32)),
        compiler_params=_params("parallel"),
        name="rwkv_sample_step",
    )(s0, *rows, g_row, beta_row)


def _sb_prompt_kernel(bias_ref, q_ref, kT_ref, vT_ref, cs_ref, o_ref):
    pair = pl.program_id(1)
    qi = pl.program_id(2)
    blk = SB_BLOCK
    cs_mat = cs_ref[...]
    q_all = q_ref[...] * ATTN_SCALE
    valid = (lax.broadcasted_iota(jnp.int32, (blk, blk), 1)
             < lax.broadcasted_iota(jnp.int32, (blk, blk), 0))

    heads = range(2)
    rows = [slice(hh * HEAD_DIM, (hh + 1) * HEAD_DIM) for hh in heads]
    lane = lax.broadcasted_iota(jnp.int32, (blk, HEAD_DIM), 1)
    ones2 = jnp.where(lane < 2, 1.0, 0.0)
    krow = lax.broadcasted_iota(jnp.int32, (HEAD_DIM, blk), 0)
    bias_rows = []
    for hh in heads:
        b_hi, b_lo = _split2(jnp.full((HEAD_DIM, blk), bias_ref[2 * pair + hh], F32))
        bias_rows.append(
            jnp.where(krow == 0, b_hi.astype(F32), jnp.where(krow == 1, b_lo.astype(F32), 0.0))
            .astype(BF16))
    chains = [(hh, u) for u in range(SB_Q_SUB) for hh in heads]
    qs = {(hh, u): jnp.concatenate([q_all[u * blk:(u + 1) * blk, rows[hh]], ones2], axis=1).astype(BF16)
          for hh, u in chains}

    def run(blocks, items, state):
        starts = [pl.multiple_of(j * blk, blk) for j in blocks]
        carries, accs = dict(zip(chains, state[0])), dict(zip(chains, state[1]))
        k16, v16, tmp = {}, {}, {}

        def stage(s, it):
            bid, ch, diag = items[it]
            hh = ch[0]
            if s == 0:
                if (bid, hh) not in k16:
                    k16[bid, hh] = jnp.concatenate(
                        [kT_ref[rows[hh], pl.ds(starts[bid], blk)].astype(BF16), bias_rows[hh]], axis=0)
                tmp[it] = {"z": _dot(qs[ch], k16[bid, hh])}
            elif s == 1:
                t = tmp[it]
                sp, t["zs"] = _softplus_pair(t.pop("z"))
                if diag:
                    sp = jnp.where(valid, sp, 0.0)
                t["first"] = sp[:, 0:1]
                t["parts"] = sp.astype(BF16)
            elif s == 2:
                t = tmp[it]
                t["cs"] = _dot(t.pop("parts"), cs_mat)
            elif s == 3:
                t = tmp[it]
                cs = t.pop("cs")
                a = jnp.exp(t.pop("zs") - cs - carries[ch])
                if diag:
                    a = jnp.where(valid, a, 0.0)
                t["a"] = a.astype(BF16)
                carries[ch] = carries[ch] + (cs[:, 0:1] + t.pop("first"))
            else:
                if (bid, hh) not in v16:
                    v16[bid, hh] = vT_ref[rows[hh], pl.ds(starts[bid], blk)].astype(BF16)
                accs[ch] = accs[ch] + _dot(tmp.pop(it)["a"], v16[bid, hh], NT)

        n_stage = 5
        for wave in range(len(items) + n_stage - 1):
            for s in reversed(range(n_stage)):
                if 0 <= wave - s < len(items):
                    stage(s, wave - s)
        return (tuple(carries[ch] for ch in chains), tuple(accs[ch] for ch in chains))

    zeros_c = jnp.zeros((blk, 1), F32)
    zeros_o = jnp.zeros((blk, HEAD_DIM), F32)
    state = ((zeros_c,) * len(chains), (zeros_o,) * len(chains))
    top = SB_Q_SUB * qi
    order = list(reversed(range(SB_Q_SUB)))
    state = run([top + d for d in order],
                [(i, ch, ch[1] == d) for i, d in enumerate(order) for ch in chains if ch[1] >= d],
                state)

    def body(jj, st):
        first = top - 1 - jj * SB_Q_SUB
        return run([first - r for r in range(SB_Q_SUB)],
                   [(r, ch, False) for r in range(SB_Q_SUB) for ch in chains], st)

    state = lax.fori_loop(0, qi, body, state)
    accs = dict(zip(chains, state[1]))
    o_ref[...] = jnp.concatenate(
        [jnp.concatenate([accs[(hh, u)] for hh in heads], axis=1) for u in range(SB_Q_SUB)], axis=0)


def _sb_prompt(q, kT, vT, bias, cs_mat):
    b, t, _ = q.shape
    blk = SB_BLOCK * SB_Q_SUB
    assert t % blk == 0, "sequence length must be a multiple of the query rows per step"
    npair = SB_HEADS // 2
    kv = pl.BlockSpec((None, PAIR_W, t), lambda bi, p, i: (bi, p, 0))
    qo = pl.BlockSpec((None, blk, PAIR_W), lambda bi, p, i: (bi, i, p))
    return pl.pallas_call(
        _sb_prompt_kernel,
        grid=(b, npair, t // blk),
        in_specs=[
            pl.BlockSpec(memory_space=pltpu.SMEM),
            qo, kv, kv,
            pl.BlockSpec((SB_BLOCK, SB_BLOCK), lambda bi, p, i: (0, 0)),
        ],
        out_specs=qo,
        out_shape=jax.ShapeDtypeStruct((b, t, SB_W), F32),
        compiler_params=_params("parallel", "parallel", "arbitrary"),
        name="sb_prompt",
    )(bias, q, kT, vT, cs_mat)


def _sb_sample_pieces(k_refs, v_refs, q_ref, bias_ref, sfx_ref, pages_ref, o_ref, acc_ref, carry_ref,
                      first_step, last_step):
    npg = len(k_refs)
    heads = range(SB_HEADS)
    nrow = SB_HEADS * npg
    pad = -nrow % LANES
    st = {}

    def init():
        @pl.when(first_step)
        def _():
            acc_ref[...] = jnp.zeros_like(acc_ref)
            carry_ref[...] = jnp.zeros_like(carry_ref)

    def score_row(h, u):
        st["z", h, u] = jnp.sum(k_refs[u][h] * q_ref[h], axis=0, keepdims=True) + bias_ref[h]

    def weights():
        zpad = [jnp.zeros((pad, LANES), F32)] if pad else []
        z = jnp.concatenate([st.pop(("z", h, u)) for h in heads for u in range(npg)] + zpad, axis=0)
        sp, _ = _softplus_pair(z)
        sums = _dot_exact_rhs(sp, sfx_ref[...])
        suffix, tot = sums[:, :LANES], sums[:, LANES:]
        later = _dot_exact_lhs(pages_ref[...], tot)
        carry = jnp.concatenate(
            [jnp.broadcast_to(carry_ref[h], (npg, LANES)) for h in heads] + zpad, axis=0)
        st["a"] = jnp.exp(z - suffix - later - carry)
        st["done"] = later + tot

    def value_row(h, u):
        r = h * npg + u
        acc = acc_ref[h] if u == 0 else st.pop(("acc", h))
        acc = acc + v_refs[u][h] * st["a"][r:r + 1, :]
        if u < npg - 1:
            st["acc", h] = acc
        else:
            acc_ref[h] = acc
            carry_ref[h] = carry_ref[h] + st["done"][r:r + 1, :]

    def finish():
        @pl.when(last_step)
        def _():
            o_ref[...] = jnp.sum(acc_ref[...], axis=2)

    pairs = [(h, u) for h in heads for u in range(npg)]
    return ([init] + [functools.partial(score_row, h, u) for h, u in pairs] + [weights]
            + [functools.partial(value_row, h, u) for h, u in pairs] + [finish])


def _sb_sample_kernel(pt_ref, k_hbm, v_hbm, q_ref, bias_ref, sfx_ref, pages_ref, o_ref,
                      kbuf, vbuf, sem, acc_ref, carry_ref, *, npg, depth):
    n_seq, n_steps = pl.num_programs(0), pl.num_programs(1)
    j = pl.program_id(1)
    g = pl.program_id(0) * n_steps + j
    total = n_seq * n_steps
    last_page = n_steps * npg - 1

    def page_copies(step, u, page):
        slot = step % depth
        return (pltpu.make_async_copy(k_hbm.at[page], kbuf.at[slot, u], sem.at[slot, 0]),
                pltpu.make_async_copy(v_hbm.at[page], vbuf.at[slot, u], sem.at[slot, 1]))

    def start_step(step):
        seq, sj = step // n_steps, step % n_steps
        for u in range(npg):
            for cp in page_copies(step, u, pt_ref[seq, last_page - (sj * npg + u)]):
                cp.start()

    @pl.when(g == 0)
    def _():
        for s in range(depth - 1):
            @pl.when(s < total)
            def _():
                start_step(jnp.int32(s))

    ahead = g + (depth - 1)

    @pl.when(ahead < total)
    def _():
        start_step(ahead)

    for u in range(npg):
        for cp in page_copies(g, u, 0):
            cp.wait()

    slot = g % depth
    k_refs = [kbuf.at[slot, u] for u in range(npg)]
    v_refs = [vbuf.at[slot, u] for u in range(npg)]
    for piece in _sb_sample_pieces(k_refs, v_refs, q_ref, bias_ref, sfx_ref, pages_ref, o_ref,
                                   acc_ref, carry_ref, j == 0, j == n_steps - 1):
        piece()


def _sb_sample_consts(h, npg, pg):
    assert pg == LANES
    nrow = h * npg + (-(h * npg) % LANES)
    tok = jnp.arange(pg)
    sfx_mat = jnp.concatenate(
        [(tok[:, None] >= tok[None, :]).astype(BF16), jnp.ones((pg, pg), BF16)], axis=1)
    r = jnp.arange(nrow)
    pages_mat = ((r[:, None] // npg == r[None, :] // npg) & (r[None, :] < r[:, None])
                 & (r[:, None] < h * npg)).astype(BF16)
    return sfx_mat, pages_mat


def _sb_sample(page_table, cache_k, cache_v, q_bcast, bias_bcast):
    n, n_pages = page_table.shape
    npg = PAGES_PER_STEP
    depth = PAGE_RING
    h, d, pg = cache_k.shape[1:]
    assert n_pages % npg == 0

    sfx_mat, pages_mat = _sb_sample_consts(h, npg, pg)
    const2 = lambda bi, j, pt: (0, 0)
    grid_spec = pltpu.PrefetchScalarGridSpec(
        num_scalar_prefetch=1,
        grid=(n, n_pages // npg),
        in_specs=[
            pl.BlockSpec(memory_space=pl.ANY),
            pl.BlockSpec(memory_space=pl.ANY),
            pl.BlockSpec((None, h, d, pg), lambda bi, j, pt: (bi, 0, 0, 0)),
            pl.BlockSpec((h, 1, pg), lambda bi, j, pt: (0, 0, 0)),
            pl.BlockSpec(sfx_mat.shape, const2),
            pl.BlockSpec(pages_mat.shape, const2),
        ],
        out_specs=pl.BlockSpec((None, h, d), lambda bi, j, pt: (bi, 0, 0)),
        scratch_shapes=[
            pltpu.VMEM((depth, npg, h, d, pg), F32),
            pltpu.VMEM((depth, npg, h, d, pg), F32),
            pltpu.SemaphoreType.DMA((depth, 2)),
            pltpu.VMEM((h, d, pg), F32),
            pltpu.VMEM((h, 1, pg), F32),
        ],
    )
    return pl.pallas_call(
        functools.partial(_sb_sample_kernel, npg=npg, depth=depth),
        grid_spec=grid_spec,
        out_shape=jax.ShapeDtypeStruct((n, h, d), F32),
        compiler_params=_params("arbitrary", "arbitrary"),
        name="sb_sample",
    )(page_table, cache_k, cache_v, q_bcast, bias_bcast, sfx_mat, pages_mat)


def _head_ones(width):
    hid = jnp.arange(width) // HEAD_DIM
    return (hid[:, None] == hid[None, :]).astype(BF16)


def _rows_for(t, target):
    return target if t % target == 0 else t


def kernel(x_prompt, mem_prompt, x_sample, cache_sb_k, cache_sb_v, page_table, state_wkv, state_shift, cache_mem_k, cache_mem_v, norm_g, w_in, sb_bias, mu_shift, w0, w_lora_b, a0, a_lora_b, k_k, k_a, r_k, lnx_g, lnx_b, mem_norm_g, w_mem_k, w_mem_v, w_out, final_norm_g):
    depth = w_in.shape[0]
    assert depth == 1, "single-layer trunk"
    l = 0
    b_p, t_p, d = x_prompt.shape
    b_s = x_sample.shape[0]
    assert x_sample.shape[1] == 1
    n_mem = mem_prompt.shape[1]
    page = cache_sb_k.shape[2]

    w16 = w_in[l].astype(BF16)
    wkT = w_mem_k[l].T.astype(BF16)
    wvT = w_mem_v[l].T.astype(BF16)
    wo = w_out[l].astype(BF16)
    fg = final_norm_g.reshape(1, d)
    r_k_flat = r_k[l].reshape(1, RWKV_W)
    ones_all = _head_ones(RWKV_W)
    blk = SB_BLOCK
    ar = jnp.arange(blk)
    cs_mat = (ar[:, None] > ar[None, :]).astype(BF16)

    mkT, mvT = _mem_kv(mem_prompt, mem_norm_g[l:l + 1], wkT, wvT)
    p_rw, q_sb, xq, gate, kT, vT = _in_proj(
        x_prompt, norm_g[l:l + 1], w16, _rows_for(t_p, 1024), True)
    vecs = (mu_shift[l:l + 1], w0[l:l + 1], a0[l:l + 1], k_k[l:l + 1], k_a[l:l + 1],
            r_k_flat, lnx_g[l:l + 1], lnx_b[l:l + 1])
    o_sb = _sb_prompt(q_sb, kT, vT, sb_bias[l], cs_mat)

    xs2 = x_sample.reshape(1, b_s, d)
    p_rw_s, q_s, xq_s, gate_s, sbk_s, sbv_s = _in_proj(xs2, norm_g[l:l + 1], w16, b_s, False)
    ck = cache_sb_k[l].transpose(0, 2, 3, 1)
    cv = cache_sb_v[l].transpose(0, 2, 3, 1)
    q_b = jnp.broadcast_to(
        (q_s[0] * ATTN_SCALE).reshape(b_s, SB_HEADS, HEAD_DIM, 1), (b_s, SB_HEADS, HEAD_DIM, page))
    bias_b = jnp.broadcast_to(sb_bias[l].reshape(SB_HEADS, 1, 1), (SB_HEADS, 1, page))

    prev0 = jnp.zeros((b_p, 1, RWKV_COLS), F32)
    h0 = jnp.zeros((b_p, RWKV_HEADS, HEAD_DIM, HEAD_DIM), F32)
    o_rw, h_fin = _rwkv_prompt(p_rw, prev0, h0, vecs, w_lora_b[l], a_lora_b[l], ones_all)
    o_sb_s = _sb_sample(page_table, ck, cv, q_b, bias_b).reshape(1, b_s, SB_W)
    y_prompt = _out_proj(o_rw, o_sb, xq, gate, x_prompt, wo, fg, _rows_for(t_p, 512), (mkT, mvT))

    def tokens_major(xT, heads):
        return xT.reshape(1, xT.shape[0], heads, HEAD_DIM, xT.shape[2]).transpose(0, 1, 4, 2, 3)

    new_sb_k_p = tokens_major(kT, SB_HEADS)
    new_sb_v_p = tokens_major(vT, SB_HEADS)
    new_wkv_p = jnp.swapaxes(h_fin, -1, -2)[None]
    new_shift_p = p_rw[:, -1][None]
    new_mem_k_p = tokens_major(mkT, X_HEADS)
    new_mem_v_p = tokens_major(mvT, X_HEADS)

    p_rw_s = p_rw_s[0]
    svecs = (mu_shift[l:l + 1], w0[l:l + 1], a0[l:l + 1], k_k[l:l + 1], k_a[l:l + 1], r_k_flat)
    r_s, w_s, k2_s, v_s, kk_s, bv_s, bonus_s = _rwkv_sample_prep(
        p_rw_s, state_shift[l], svecs, w_lora_b[l], a_lora_b[l], ones_all)
    as_row = lambda a_: a_.reshape(b_s, RWKV_HEADS, 1, HEAD_DIM)
    o_rw_s, new_wkv_s = _rwkv_sample_step(
        state_wkv[l],
        [as_row(t_) for t_ in (r_s, w_s, k2_s, kk_s, bv_s, v_s, bonus_s)],
        lnx_g[l].reshape(RWKV_HEADS, 1, HEAD_DIM), lnx_b[l].reshape(RWKV_HEADS, 1, HEAD_DIM),
        SAMPLE_GROUP if b_s % SAMPLE_GROUP == 0 else 1)
    o_rw_s = o_rw_s.reshape(1, b_s, RWKV_W)

    mk_s = cache_mem_k[l].transpose(0, 2, 3, 1).reshape(b_s, X_W, n_mem)
    mv_s = cache_mem_v[l].transpose(0, 2, 3, 1).reshape(b_s, X_W, n_mem)
    xq_rows = jnp.broadcast_to(xq_s[0][:, None, :], (b_s, 8, X_W))
    o_x_s = _xattn(xq_rows, mk_s, mv_s, 8, SAMPLE_GROUP if b_s % SAMPLE_GROUP == 0 else 1)[:, 0][None]
    y_sample = _out_proj(o_rw_s, o_sb_s, o_x_s, gate_s, xs2, wo, fg, b_s).reshape(b_s, 1, d)

    new_sb_k_s = sbk_s.reshape(1, b_s, 1, SB_HEADS, HEAD_DIM)
    new_sb_v_s = sbv_s.reshape(1, b_s, 1, SB_HEADS, HEAD_DIM)
    new_shift_s = p_rw_s[None]

    return (y_prompt, y_sample, new_sb_k_p, new_sb_v_p, new_wkv_p, new_shift_p,
            new_mem_k_p, new_mem_v_p, new_sb_k_s, new_sb_v_s, new_wkv_s[None], new_shift_s)
```

```python
import functools
import math

import jax
import jax.numpy as jnp
from jax import lax
from jax.experimental import pallas as pl
from jax.experimental.pallas import tpu as pltpu

F32 = jnp.float32
BF16 = jnp.bfloat16

HEAD_DIM = 64
RWKV_HEADS = 6
SB_HEADS = 6
X_HEADS = 4
RWKV_W = RWKV_HEADS * HEAD_DIM
SB_W = SB_HEADS * HEAD_DIM
X_W = X_HEADS * HEAD_DIM
LORA = 64
RWKV_COLS = 3 * RWKV_W + 2 * LORA
NORM_EPS = 1e-6
GN_EPS = 64e-5
DECAY_SCALE = math.exp(-0.5)
ATTN_SCALE = HEAD_DIM ** -0.5
LOG2E = math.log2(math.e)

LANES = 128
PAIR_W = 2 * HEAD_DIM
RWKV_CHUNK = 128
RWKV_SUB = 2
INV_BASE = 8
SB_BLOCK = 256
SB_Q_SUB = 4
PAGES_PER_STEP = 16
PAGE_RING = 3
SAMPLE_GROUP = 8
VMEM_LIMIT = 48 * 1024 * 1024

NN = (((1,), (0,)), ((), ()))
NT = (((1,), (1,)), ((), ()))
TN = (((0,), (0,)), ((), ()))


def _dot(a, b, dims=NN):
    return lax.dot_general(a, b, dims, preferred_element_type=F32)


def _split2(x):
    hi = x.astype(BF16)
    lo = (x - hi.astype(F32)).astype(BF16)
    return hi, lo


def _dot_f32(a, b, dims=NN):
    (ka,), (kb,) = dims[0]
    ah, al = _split2(a)
    bh, bl = _split2(b)
    return _dot(jnp.concatenate([ah, ah, al], axis=ka), jnp.concatenate([bh, bl, bh], axis=kb), dims)


def _dot_rhs16(a, b, dims=NN):
    (ka,), (kb,) = dims[0]
    ah, al = _split2(a)
    bh = b.astype(BF16)
    return _dot(jnp.concatenate([ah, al], axis=ka), jnp.concatenate([bh, bh], axis=kb), dims)


def _dot_bf16(a, b, dims=NN):
    return _dot(a.astype(BF16), b.astype(BF16), dims)


def _dot_exact_rhs(a, b_bf16, dims=NN):
    hi, lo = _split2(a)
    return _dot(hi, b_bf16, dims) + _dot(lo, b_bf16, dims)


def _dot_exact_lhs(a_bf16, b, dims=NN):
    hi, lo = _split2(b)
    return _dot(a_bf16, hi, dims) + _dot(a_bf16, lo, dims)


def _sigmoid(x):
    return 1.0 / (1.0 + jnp.exp(-x))


def _softplus_pair(z):
    sp = jnp.maximum(z, 0.0) + jnp.log(1.0 + jnp.exp2(jnp.abs(z) * -LOG2E))
    return sp, z - sp


def _rmsnorm_rows(x, g):
    ms = jnp.mean(x * x, axis=-1, keepdims=True)
    return x * lax.rsqrt(ms + NORM_EPS) * g


def _params(*sem):
    return pltpu.CompilerParams(dimension_semantics=sem, vmem_limit_bytes=VMEM_LIMIT)


KV_COL0 = RWKV_COLS + SB_W
KV_COL1 = KV_COL0 + 2 * SB_W


def _in_proj_kernel(x_ref, g_ref, w_ref, prw_ref, q_ref, xq_ref, gate_ref, k_ref, v_ref,
                    *, kv_feature_major):
    h = _rmsnorm_rows(x_ref[...], g_ref[...]).astype(BF16)
    pa = _dot(h, w_ref[:, :KV_COL0])
    prw_ref[...] = pa[:, :RWKV_COLS]
    q_ref[...] = pa[:, RWKV_COLS:]
    kv = _dot(h, w_ref[:, KV_COL0:KV_COL1])
    if kv_feature_major:
        k_ref[...] = kv[:, :SB_W].T
        v_ref[...] = kv[:, SB_W:].T
    else:
        k_ref[...] = kv[:, :SB_W]
        v_ref[...] = kv[:, SB_W:]
    pb = _dot(h, w_ref[:, KV_COL1:])
    xq_ref[...] = pb[:, :X_W]
    gate_ref[...] = pb[:, X_W:]


def _in_proj(x, norm_g, w16, rows, kv_feature_major):
    b, t, d = x.shape
    n_in = w16.shape[1]
    n_gate = n_in - KV_COL1 - X_W
    row = lambda bi, i: (bi, i, 0)
    col = lambda bi, i: (bi, 0, i)
    const = lambda bi, i: (0, 0)
    if kv_feature_major:
        kv_shape = jax.ShapeDtypeStruct((b, SB_W, t), F32)
        kv_spec = pl.BlockSpec((None, SB_W, rows), col)
    else:
        kv_shape = jax.ShapeDtypeStruct((b, t, SB_W), F32)
        kv_spec = pl.BlockSpec((None, rows, SB_W), row)
    out_shape = (
        jax.ShapeDtypeStruct((b, t, RWKV_COLS), F32),
        jax.ShapeDtypeStruct((b, t, SB_W), F32),
        jax.ShapeDtypeStruct((b, t, X_W), F32),
        jax.ShapeDtypeStruct((b, t, n_gate), F32),
        kv_shape, kv_shape,
    )
    return pl.pallas_call(
        functools.partial(_in_proj_kernel, kv_feature_major=kv_feature_major),
        grid=(b, t // rows),
        in_specs=[
            pl.BlockSpec((None, rows, d), row),
            pl.BlockSpec((1, d), const),
            pl.BlockSpec((d, n_in), const, pipeline_mode=pl.Buffered(1)),
        ],
        out_specs=(
            pl.BlockSpec((None, rows, RWKV_COLS), row),
            pl.BlockSpec((None, rows, SB_W), row),
            pl.BlockSpec((None, rows, X_W), row),
            pl.BlockSpec((None, rows, n_gate), row),
            kv_spec, kv_spec,
        ),
        out_shape=out_shape,
        compiler_params=_params("parallel", "parallel"),
        name="in_proj",
    )(x, norm_g, w16)


def _mem_kv_kernel(mem_ref, g_ref, wkT_ref, wvT_ref, kT_ref, vT_ref):
    h = _rmsnorm_rows(mem_ref[...], g_ref[...]).astype(BF16)
    kT_ref[...] = _dot(wkT_ref[...], h, NT)
    vT_ref[...] = _dot(wvT_ref[...], h, NT)


def _mem_kv(mem, g, wkT, wvT):
    b, m, d = mem.shape
    const = lambda bi: (0, 0)
    blk = lambda bi: (bi, 0, 0)
    return pl.pallas_call(
        _mem_kv_kernel,
        grid=(b,),
        in_specs=[
            pl.BlockSpec((None, m, d), blk),
            pl.BlockSpec((1, d), const),
            pl.BlockSpec((X_W, d), const),
            pl.BlockSpec((X_W, d), const),
        ],
        out_specs=(pl.BlockSpec((None, X_W, m), blk), pl.BlockSpec((None, X_W, m), blk)),
        out_shape=(jax.ShapeDtypeStruct((b, X_W, m), F32),) * 2,
        compiler_params=_params("parallel"),
        name="mem_kv",
    )(mem, g, wkT, wvT)


def _xattn_rows(xq, kT_ref, vT_ref):
    heads = range(X_HEADS)
    sl = [slice(h * HEAD_DIM, (h + 1) * HEAD_DIM) for h in heads]
    s = [_dot((xq[:, sl[h]] * ATTN_SCALE).astype(BF16), kT_ref[sl[h], :].astype(BF16)) for h in heads]
    e = [jnp.exp(s[h] - jnp.max(s[h], axis=-1, keepdims=True)) for h in heads]
    l = [jnp.sum(e[h], axis=-1, keepdims=True) for h in heads]
    o = [_dot(e[h].astype(BF16), vT_ref[sl[h], :].astype(BF16), NT) / l[h] for h in heads]
    return jnp.concatenate(o, axis=1)


def _xattn_kernel(xq_ref, kT_ref, vT_ref, o_ref):
    for g in range(xq_ref.shape[0]):
        o_ref[g] = _xattn_rows(xq_ref[g], kT_ref.at[g], vT_ref.at[g])


def _xattn(xq, kT, vT, rows, group):
    b, t, _ = xq.shape
    m = kT.shape[-1]
    row = lambda bi, i: (bi, i, 0)
    full = lambda bi, i: (bi, 0, 0)
    return pl.pallas_call(
        _xattn_kernel,
        grid=(b // group, t // rows),
        in_specs=[
            pl.BlockSpec((group, rows, X_W), row),
            pl.BlockSpec((group, X_W, m), full),
            pl.BlockSpec((group, X_W, m), full),
        ],
        out_specs=pl.BlockSpec((group, rows, X_W), row),
        out_shape=jax.ShapeDtypeStruct((b, t, X_W), F32),
        compiler_params=_params("parallel", "parallel"),
        name="xattn",
    )(xq, kT, vT)


def _out_proj_kernel(orw_ref, osb_ref, x3_ref, gate_ref, x_ref, w_ref, fg_ref, *rest, fused_xattn):
    if fused_xattn:
        kT_ref, vT_ref, y_ref = rest
        o_x = _xattn_rows(x3_ref[...], kT_ref, vT_ref)
    else:
        (y_ref,) = rest
        o_x = x3_ref[...]
    g = gate_ref[...]
    sg = g * _sigmoid(g)
    a0, a1 = RWKV_W, RWKV_W + SB_W
    acc = _dot((orw_ref[...] * sg[:, :a0]).astype(BF16), w_ref[:a0, :])
    acc += _dot((osb_ref[...] * sg[:, a0:a1]).astype(BF16), w_ref[a0:a1, :])
    acc += _dot((o_x * sg[:, a1:]).astype(BF16), w_ref[a1:, :])
    y_ref[...] = _rmsnorm_rows(x_ref[...] + acc, fg_ref[...])


def _out_proj(o_rw, o_sb, x3, gate, x, w_out, fg, rows, mem_kv=None):
    b, t, d = x.shape
    dm = w_out.shape[0]
    row = lambda bi, i: (bi, i, 0)
    const = lambda bi, i: (0, 0)
    in_specs = [
        pl.BlockSpec((None, rows, RWKV_W), row),
        pl.BlockSpec((None, rows, SB_W), row),
        pl.BlockSpec((None, rows, X_W), row),
        pl.BlockSpec((None, rows, dm), row),
        pl.BlockSpec((None, rows, d), row),
        pl.BlockSpec((dm, d), const),
        pl.BlockSpec((1, d), const),
    ]
    operands = [o_rw, o_sb, x3, gate, x, w_out, fg]
    if mem_kv is not None:
        m = mem_kv[0].shape[-1]
        in_specs += [pl.BlockSpec((None, X_W, m), lambda bi, i: (bi, 0, 0))] * 2
        operands += list(mem_kv)
    return pl.pallas_call(
        functools.partial(_out_proj_kernel, fused_xattn=mem_kv is not None),
        grid=(b, t // rows),
        in_specs=in_specs,
        out_specs=pl.BlockSpec((None, rows, d), row),
        out_shape=jax.ShapeDtypeStruct((b, t, d), F32),
        compiler_params=_params("parallel", "parallel"),
        name="out_proj",
    )(*operands)


def _rwkv_token_math(r, k, v, xw, xa, w0, a0, k_k, k_a, r_k, wlb, alb, head_ones):
    log_w = -DECAY_SCALE * _sigmoid(w0 + _dot(jnp.tanh(xw).astype(BF16), wlb.astype(BF16)))
    a = _sigmoid(a0 + _dot(xa.astype(BF16), alb.astype(BF16)))
    kk = k * k_k
    ss = _dot_exact_rhs(kk * kk, head_ones)
    kk = kk * lax.rsqrt(jnp.maximum(ss, 1e-12))
    k2 = k * (1.0 + (a - 1.0) * k_a)
    bonus = _dot_exact_rhs(r * k2 * r_k, head_ones) * v
    return log_w, a, kk, k2, bonus


def _group_norm(o, g, b, head_ones):
    inv = 1.0 / HEAD_DIM
    mean = _dot_exact_rhs(o, head_ones) * inv
    d = o - mean
    var = _dot_exact_rhs(d * d, head_ones) * inv
    return d * lax.rsqrt(var + GN_EPS) * g + b


def _shift_mix(cur, last_row, mu, first_row_mask):
    prev = jnp.where(first_row_mask, last_row, pltpu.roll(cur, 1, axis=0))
    return cur + (prev - cur) * mu


def _rwkv_prompt_kernel(
        p_ref, prev_ref, first_ref, mu_ref,
        w0_ref, a0_ref, kk_ref, ka_ref, rk_ref, lg_ref, lb_ref, wlb_ref, alb_ref,
        h0_ref, tri_ref, ones_ref,
        o_ref, hfin_ref, h_scr, *, n_sub):
    i = pl.program_id(1)
    c = RWKV_CHUNK
    rows = c * n_sub
    w3 = 3 * RWKV_W

    @pl.when(i == 0)
    def _():
        h_scr[...] = h0_ref[...]

    first = lax.broadcasted_iota(jnp.int32, (rows, RWKV_COLS), 0) == 0
    last = jnp.where(i == 0, first_ref[...], prev_ref[7:8, :])
    xs = _shift_mix(p_ref[...], last, mu_ref[...], first)
    r, k, v = xs[:, :RWKV_W], xs[:, RWKV_W:2 * RWKV_W], xs[:, 2 * RWKV_W:w3]
    head_ones = ones_ref[...]
    log_w, a, kk, k2, bonus = _rwkv_token_math(
        r, k, v, xs[:, w3:w3 + LORA], xs[:, w3 + LORA:], w0_ref[...], a0_ref[...], kk_ref[...],
        ka_ref[...], rk_ref[...], wlb_ref[...], alb_ref[...], head_ones)

    subs = range(n_sub)
    lam = _dot_exact_lhs(tri_ref[...], log_w)
    lam_ends = [lam[(sc + 1) * c - 1:(sc + 1) * c, :] for sc in subs]
    lam_c = jnp.concatenate([jnp.broadcast_to(le, (c, RWKV_W)) for le in lam_ends], axis=0)
    w_in = jnp.exp(lam)
    w_ex = jnp.exp(lam - log_w)
    w_inv = jnp.exp(-lam)
    w_end = jnp.exp(lam_c - lam)
    w_c = [jnp.exp(le) for le in lam_ends]
    bvec = kk * a
    at = -kk * w_ex
    bt = bvec * w_inv
    kt = k2 * w_inv
    rt = r * w_in
    bh = bvec * w_end
    kh = k2 * w_end

    colid = lax.broadcasted_iota(jnp.int32, (c, c), 1)
    rowid = lax.broadcasted_iota(jnp.int32, (c, c), 0)
    strict = colid < rowid
    incl = colid <= rowid
    eye64 = (lax.broadcasted_iota(jnp.int32, (HEAD_DIM, HEAD_DIM), 0)
             == lax.broadcasted_iota(jnp.int32, (HEAD_DIM, HEAD_DIM), 1))

    heads = range(RWKV_HEADS)
    chains = [(sc, hh) for sc in subs for hh in heads]
    cut = lambda x: {(sc, hh): x[sc * c:(sc + 1) * c, hh * HEAD_DIM:(hh + 1) * HEAD_DIM]
                     for sc, hh in chains}
    at_h, bt_h, kt_h, rt_h, v_h, bh_h, kh_h = map(cut, (at, bt, kt, rt, v, bh, kh))
    each = lambda fn: {ch: fn(ch) for ch in chains}
    scores = each(lambda ch: _dot_f32(
        jnp.concatenate([at_h[ch], rt_h[ch]], axis=0),
        jnp.concatenate([bt_h[ch], kt_h[ch]], axis=0), NT))
    n = each(lambda ch: jnp.where(strict, scores[ch][:c, :c], 0.0))
    aak = each(lambda ch: jnp.where(strict, scores[ch][:c, c:], 0.0))
    mrb = each(lambda ch: jnp.where(incl, scores[ch][c:, :c], 0.0))
    mrk = each(lambda ch: jnp.where(incl, scores[ch][c:, c:], 0.0))
    rb, cb = rowid // INV_BASE, colid // INV_BASE
    dg = each(lambda ch: jnp.where(rb == cb, n[ch], 0.0))
    tm = each(lambda ch: jnp.where(colid == rowid, 1.0, dg[ch]))
    span = 1
    while 2 * span < INV_BASE:
        dg = each(lambda ch: _dot_bf16(dg[ch], dg[ch]))
        tm = each(lambda ch: tm[ch] + _dot_bf16(tm[ch], dg[ch]))
        span *= 2
    m = INV_BASE
    while m < c:
        off = (rowid // (2 * m) == colid // (2 * m)) & (rowid // m != colid // m)
        tn = each(lambda ch: _dot_bf16(tm[ch], jnp.where(off, n[ch], 0.0)))
        tm = each(lambda ch: tm[ch] + _dot_bf16(tn[ch], tm[ch]))
        m *= 2
    akv = each(lambda ch: _dot_rhs16(aak[ch], v_h[ch]))
    mkv = each(lambda ch: _dot_rhs16(mrk[ch], v_h[ch]))
    khv = each(lambda ch: _dot_rhs16(kh_h[ch], v_h[ch], TN))
    pq = each(lambda ch: _dot_rhs16(tm[ch], jnp.concatenate([at_h[ch], akv[ch]], axis=1)))
    mpq = each(lambda ch: _dot_rhs16(mrb[ch], pq[ch]))
    gj = each(lambda ch: _dot_rhs16(bh_h[ch], pq[ch], TN))
    p2 = each(lambda ch: rt_h[ch] + mpq[ch][:, :HEAD_DIM])
    q2 = each(lambda ch: mpq[ch][:, HEAD_DIM:] + mkv[ch])
    g = each(lambda ch: gj[ch][:, :HEAD_DIM] + jnp.where(
        eye64, w_c[ch[0]][:, ch[1] * HEAD_DIM:(ch[1] + 1) * HEAD_DIM], 0.0))
    jm = each(lambda ch: gj[ch][:, HEAD_DIM:] + khv[ch])

    state = [h_scr[hh] for hh in heads]
    outs = []
    for sc in subs:
        outs.append(jnp.concatenate(
            [_dot_f32(p2[(sc, hh)], state[hh]) + q2[(sc, hh)] for hh in heads], axis=1))
        state = [_dot_f32(g[(sc, hh)], state[hh]) + jm[(sc, hh)] for hh in heads]
    for hh in heads:
        h_scr[hh] = state[hh]

    o = jnp.concatenate(outs, axis=0)
    o_ref[...] = _group_norm(o, lg_ref[...], lb_ref[...], head_ones) + bonus

    @pl.when(i == pl.num_programs(1) - 1)
    def _():
        hfin_ref[...] = h_scr[...]


def _rwkv_prompt_specs(b, t, n_sub, index):
    c = RWKV_CHUNK * n_sub
    assert t % c == 0, "sequence length must be a multiple of the tokens per step"
    sub = 8
    const = index(lambda bi, i: (0, 0))
    hvec = pl.BlockSpec((1, RWKV_W), const)
    lora = pl.BlockSpec((LORA, RWKV_W), const)
    state = pl.BlockSpec((None, RWKV_HEADS, HEAD_DIM, HEAD_DIM), index(lambda bi, i: (bi, 0, 0, 0)))
    in_specs = [
        pl.BlockSpec((None, c, RWKV_COLS), index(lambda bi, i: (bi, i, 0))),
        pl.BlockSpec((None, sub, RWKV_COLS),
                     index(lambda bi, i: (bi, jnp.maximum(i * (c // sub) - 1, 0), 0))),
        pl.BlockSpec((None, 1, RWKV_COLS), index(lambda bi, i: (bi, 0, 0))),
        pl.BlockSpec((1, RWKV_COLS), const),
        hvec, hvec, hvec, hvec, hvec, hvec, hvec, lora, lora,
        state,
        pl.BlockSpec((c, c), const),
        pl.BlockSpec((RWKV_W, RWKV_W), const),
    ]
    out_specs = [pl.BlockSpec((None, c, RWKV_W), index(lambda bi, i: (bi, i, 0))), state]
    out_shape = [jax.ShapeDtypeStruct((b, t, RWKV_W), F32),
                 jax.ShapeDtypeStruct((b, RWKV_HEADS, HEAD_DIM, HEAD_DIM), F32)]
    scratch = [pltpu.VMEM((RWKV_HEADS, HEAD_DIM, HEAD_DIM), F32)]
    return in_specs, out_specs, out_shape, scratch


def _chunk_tri(n_sub):
    c = RWKV_CHUNK
    tok = jnp.arange(c * n_sub)
    return ((tok[None, :] <= tok[:, None]) & (tok[None, :] // c == tok[:, None] // c)).astype(BF16)


def _rwkv_prompt(p_rw, prev0, h0, vecs, wlb, alb, ones):
    b, t, _ = p_rw.shape
    mu, w0, a0, k_k, k_a, r_k, lnx_g, lnx_b = vecs
    in_specs, out_specs, out_shape, scratch = _rwkv_prompt_specs(b, t, RWKV_SUB, lambda fn: fn)
    return pl.pallas_call(
        functools.partial(_rwkv_prompt_kernel, n_sub=RWKV_SUB),
        grid=(b, t // (RWKV_CHUNK * RWKV_SUB)),
        in_specs=in_specs,
        out_specs=tuple(out_specs),
        out_shape=tuple(out_shape),
        scratch_shapes=scratch,
        compiler_params=_params("parallel", "arbitrary"),
        name="rwkv_prompt",
    )(p_rw, p_rw, prev0, mu, w0, a0, k_k, k_a, r_k, lnx_g, lnx_b, wlb, alb, h0,
      _chunk_tri(RWKV_SUB), ones)


def _rwkv_sample_prep_kernel(p_ref, prev_ref, mu_ref, w0_ref, a0_ref, kk_ref, ka_ref, rk_ref,
                             wlb_ref, alb_ref, ones_ref,
                             r_ref, w_ref, k2_ref, v_ref, kkn_ref, b_ref, bonus_ref):
    pf = p_ref[...]
    xs = pf + (prev_ref[...] - pf) * mu_ref[...]
    w3 = 3 * RWKV_W
    r, k, v = xs[:, :RWKV_W], xs[:, RWKV_W:2 * RWKV_W], xs[:, 2 * RWKV_W:w3]
    log_w, a, kk, k2, bonus = _rwkv_token_math(
        r, k, v, xs[:, w3:w3 + LORA], xs[:, w3 + LORA:], w0_ref[...], a0_ref[...], kk_ref[...],
        ka_ref[...], rk_ref[...], wlb_ref[...], alb_ref[...], ones_ref[...])
    r_ref[...] = r
    w_ref[...] = jnp.exp(log_w)
    k2_ref[...] = k2
    v_ref[...] = v
    kkn_ref[...] = kk
    b_ref[...] = kk * a
    bonus_ref[...] = bonus


def _rwkv_sample_prep(p, prev, vecs, wlb, alb, ones):
    n = p.shape[0]
    mu, w0, a0, k_k, k_a, r_k = vecs
    return pl.pallas_call(
        _rwkv_sample_prep_kernel,
        out_shape=(jax.ShapeDtypeStruct((n, RWKV_W), F32),) * 7,
        name="rwkv_sample_prep",
    )(p, prev, mu, w0, a0, k_k, k_a, r_k, wlb, alb, ones)


def _rwkv_sample_step_kernel(s_ref, r_ref, w_ref, k_ref, kk_ref, b_ref, v_ref, bonus_ref,
                             g_ref, beta_ref, o_ref, snew_ref):
    eye = (lax.broadcasted_iota(jnp.int32, (HEAD_DIM, HEAD_DIM), 0)
           == lax.broadcasted_iota(jnp.int32, (HEAD_DIM, HEAD_DIM), 1))
    to_col = lambda row: jnp.sum(jnp.where(eye, row, 0.0), axis=3, keepdims=True)
    to_row = lambda col: jnp.sum(jnp.where(eye, col, 0.0), axis=2, keepdims=True)
    s = s_ref[...]
    s_kk = jnp.sum(s * kk_ref[...], axis=3, keepdims=True)
    s = s * w_ref[...] - s_kk * b_ref[...] + to_col(v_ref[...]) * k_ref[...]
    snew_ref[...] = s
    o = to_row(jnp.sum(s * r_ref[...], axis=3, keepdims=True))
    mean = jnp.mean(o, axis=3, keepdims=True)
    d = o - mean
    var = jnp.mean(d * d, axis=3, keepdims=True)
    o_ref[...] = d * lax.rsqrt(var + GN_EPS) * g_ref[...] + beta_ref[...] + bonus_ref[...]


def _rwkv_sample_step(s0, rows, g_row, beta_row, group):
    n = s0.shape[0]
    h, d = RWKV_HEADS, HEAD_DIM
    st = pl.BlockSpec((group, h, d, d), lambda bi: (bi, 0, 0, 0))
    rowspec = pl.BlockSpec((group, h, 1, d), lambda bi: (bi, 0, 0, 0))
    cconst = pl.BlockSpec((h, 1, d), lambda bi: (0, 0, 0))
    return pl.pallas_call(
        _rwkv_sample_step_kernel,
        grid=(n // group,),
        in_specs=[st] + [rowspec] * 7 + [cconst] * 2,
        out_specs=(rowspec, st),
        out_shape=(jax.ShapeDtypeStruct((n, h, 1, d), F32),
                   jax.ShapeDtypeStruct((n, h, d, d), F32)),
        compiler_params=_params("parallel"),
        name="rwkv_sample_step",
    )(s0, *rows, g_row, beta_row)


def _sb_prompt_kernel(bias_ref, q_ref, kT_ref, vT_ref, cs_ref, o_ref):
    pair = pl.program_id(1)
    qi = pl.program_id(2)
    blk = SB_BLOCK
    cs_mat = cs_ref[...]
    q_all = q_ref[...] * ATTN_SCALE
    valid = (lax.broadcasted_iota(jnp.int32, (blk, blk), 1)
             < lax.broadcasted_iota(jnp.int32, (blk, blk), 0))

    heads = range(2)
    rows = [slice(hh * HEAD_DIM, (hh + 1) * HEAD_DIM) for hh in heads]
    lane = lax.broadcasted_iota(jnp.int32, (blk, HEAD_DIM), 1)
    ones2 = jnp.where(lane < 2, 1.0, 0.0)
    krow = lax.broadcasted_iota(jnp.int32, (HEAD_DIM, blk), 0)
    bias_rows = []
    for hh in heads:
        b_hi, b_lo = _split2(jnp.full((HEAD_DIM, blk), bias_ref[2 * pair + hh], F32))
        bias_rows.append(
            jnp.where(krow == 0, b_hi.astype(F32), jnp.where(krow == 1, b_lo.astype(F32), 0.0))
            .astype(BF16))
    chains = [(hh, u) for u in range(SB_Q_SUB) for hh in heads]
    qs = {(hh, u): jnp.concatenate([q_all[u * blk:(u + 1) * blk, rows[hh]], ones2], axis=1).astype(BF16)
          for hh, u in chains}

    def run(blocks, items, state):
        starts = [pl.multiple_of(j * blk, blk) for j in blocks]
        carries, accs = dict(zip(chains, state[0])), dict(zip(chains, state[1]))
        k16, v16, tmp = {}, {}, {}

        def stage(s, it):
            bid, ch, diag = items[it]
            hh = ch[0]
            if s == 0:
                if (bid, hh) not in k16:
                    k16[bid, hh] = jnp.concatenate(
                        [kT_ref[rows[hh], pl.ds(starts[bid], blk)].astype(BF16), bias_rows[hh]], axis=0)
                tmp[it] = {"z": _dot(qs[ch], k16[bid, hh])}
            elif s == 1:
                t = tmp[it]
                sp, t["zs"] = _softplus_pair(t.pop("z"))
                if diag:
                    sp = jnp.where(valid, sp, 0.0)
                t["first"] = sp[:, 0:1]
                t["parts"] = sp.astype(BF16)
            elif s == 2:
                t = tmp[it]
                t["cs"] = _dot(t.pop("parts"), cs_mat)
            elif s == 3:
                t = tmp[it]
                cs = t.pop("cs")
                a = jnp.exp(t.pop("zs") - cs - carries[ch])
                if diag:
                    a = jnp.where(valid, a, 0.0)
                t["a"] = a.astype(BF16)
                carries[ch] = carries[ch] + (cs[:, 0:1] + t.pop("first"))
            else:
                if (bid, hh) not in v16:
                    v16[bid, hh] = vT_ref[rows[hh], pl.ds(starts[bid], blk)].astype(BF16)
                accs[ch] = accs[ch] + _dot(tmp.pop(it)["a"], v16[bid, hh], NT)

        n_stage = 5
        for wave in range(len(items) + n_stage - 1):
            for s in reversed(range(n_stage)):
                if 0 <= wave - s < len(items):
                    stage(s, wave - s)
        return (tuple(carries[ch] for ch in chains), tuple(accs[ch] for ch in chains))

    zeros_c = jnp.zeros((blk, 1), F32)
    zeros_o = jnp.zeros((blk, HEAD_DIM), F32)
    state = ((zeros_c,) * len(chains), (zeros_o,) * len(chains))
    top = SB_Q_SUB * qi
    order = list(reversed(range(SB_Q_SUB)))
    state = run([top + d for d in order],
                [(i, ch, ch[1] == d) for i, d in enumerate(order) for ch in chains if ch[1] >= d],
                state)

    def body(jj, st):
        first = top - 1 - jj * SB_Q_SUB
        return run([first - r for r in range(SB_Q_SUB)],
                   [(r, ch, False) for r in range(SB_Q_SUB) for ch in chains], st)

    state = lax.fori_loop(0, qi, body, state)
    accs = dict(zip(chains, state[1]))
    o_ref[...] = jnp.concatenate(
        [jnp.concatenate([accs[(hh, u)] for hh in heads], axis=1) for u in range(SB_Q_SUB)], axis=0)


def _sb_prompt(q, kT, vT, bias, cs_mat):
    b, t, _ = q.shape
    blk = SB_BLOCK * SB_Q_SUB
    assert t % blk == 0, "sequence length must be a multiple of the query rows per step"
    npair = SB_HEADS // 2
    kv = pl.BlockSpec((None, PAIR_W, t), lambda bi, p, i: (bi, p, 0))
    qo = pl.BlockSpec((None, blk, PAIR_W), lambda bi, p, i: (bi, i, p))
    return pl.pallas_call(
        _sb_prompt_kernel,
        grid=(b, npair, t // blk),
        in_specs=[
            pl.BlockSpec(memory_space=pltpu.SMEM),
            qo, kv, kv,
            pl.BlockSpec((SB_BLOCK, SB_BLOCK), lambda bi, p, i: (0, 0)),
        ],
        out_specs=qo,
        out_shape=jax.ShapeDtypeStruct((b, t, SB_W), F32),
        compiler_params=_params("parallel", "parallel", "arbitrary"),
        name="sb_prompt",
    )(bias, q, kT, vT, cs_mat)


def _sb_sample_pieces(k_refs, v_refs, q_ref, bias_ref, sfx_ref, pages_ref, o_ref, acc_ref, carry_ref,
                      first_step, last_step):
    npg = len(k_refs)
    heads = range(SB_HEADS)
    nrow = SB_HEADS * npg
    pad = -nrow % LANES
    st = {}

    def init():
        @pl.when(first_step)
        def _():
            acc_ref[...] = jnp.zeros_like(acc_ref)
            carry_ref[...] = jnp.zeros_like(carry_ref)

    def score_row(h, u):
        st["z", h, u] = jnp.sum(k_refs[u][h] * q_ref[h], axis=0, keepdims=True) + bias_ref[h]

    def weights():
        zpad = [jnp.zeros((pad, LANES), F32)] if pad else []
        z = jnp.concatenate([st.pop(("z", h, u)) for h in heads for u in range(npg)] + zpad, axis=0)
        sp, _ = _softplus_pair(z)
        sums = _dot_exact_rhs(sp, sfx_ref[...])
        suffix, tot = sums[:, :LANES], sums[:, LANES:]
        later = _dot_exact_lhs(pages_ref[...], tot)
        carry = jnp.concatenate(
            [jnp.broadcast_to(carry_ref[h], (npg, LANES)) for h in heads] + zpad, axis=0)
        st["a"] = jnp.exp(z - suffix - later - carry)
        st["done"] = later + tot

    def value_row(h, u):
        r = h * npg + u
        acc = acc_ref[h] if u == 0 else st.pop(("acc", h))
        acc = acc + v_refs[u][h] * st["a"][r:r + 1, :]
        if u < npg - 1:
            st["acc", h] = acc
        else:
            acc_ref[h] = acc
            carry_ref[h] = carry_ref[h] + st["done"][r:r + 1, :]

    def finish():
        @pl.when(last_step)
        def _():
            o_ref[...] = jnp.sum(acc_ref[...], axis=2)

    pairs = [(h, u) for h in heads for u in range(npg)]
    return ([init] + [functools.partial(score_row, h, u) for h, u in pairs] + [weights]
            + [functools.partial(value_row, h, u) for h, u in pairs] + [finish])


def _sb_sample_kernel(pt_ref, k_hbm, v_hbm, q_ref, bias_ref, sfx_ref, pages_ref, o_ref,
                      kbuf, vbuf, sem, acc_ref, carry_ref, *, npg, depth):
    n_seq, n_steps = pl.num_programs(0), pl.num_programs(1)
    j = pl.program_id(1)
    g = pl.program_id(0) * n_steps + j
    total = n_seq * n_steps
    last_page = n_steps * npg - 1

    def page_copies(step, u, page):
        slot = step % depth
        return (pltpu.make_async_copy(k_hbm.at[page], kbuf.at[slot, u], sem.at[slot, 0]),
                pltpu.make_async_copy(v_hbm.at[page], vbuf.at[slot, u], sem.at[slot, 1]))

    def start_step(step):
        seq, sj = step // n_steps, step % n_steps
        for u in range(npg):
            for thread, cp in enumerate(page_copies(step, u, pt_ref[seq, last_page - (sj * npg + u)])):
                cp.start(priority=thread)

    @pl.when(g == 0)
    def _():
        for s in range(depth - 1):
            @pl.when(s < total)
            def _():
                start_step(jnp.int32(s))

    ahead = g + (depth - 1)

    @pl.when(ahead < total)
    def _():
        start_step(ahead)

    for u in range(npg):
        for cp in page_copies(g, u, 0):
            cp.wait()

    slot = g % depth
    k_refs = [kbuf.at[slot, u] for u in range(npg)]
    v_refs = [vbuf.at[slot, u] for u in range(npg)]
    for piece in _sb_sample_pieces(k_refs, v_refs, q_ref, bias_ref, sfx_ref, pages_ref, o_ref,
                                   acc_ref, carry_ref, j == 0, j == n_steps - 1):
        piece()


def _sb_sample_consts(h, npg, pg):
    assert pg == LANES
    nrow = h * npg + (-(h * npg) % LANES)
    tok = jnp.arange(pg)
    sfx_mat = jnp.concatenate(
        [(tok[:, None] >= tok[None, :]).astype(BF16), jnp.ones((pg, pg), BF16)], axis=1)
    r = jnp.arange(nrow)
    pages_mat = ((r[:, None] // npg == r[None, :] // npg) & (r[None, :] < r[:, None])
                 & (r[:, None] < h * npg)).astype(BF16)
    return sfx_mat, pages_mat


def _sb_sample(page_table, cache_k, cache_v, q_bcast, bias_bcast):
    n, n_pages = page_table.shape
    npg = PAGES_PER_STEP
    depth = PAGE_RING
    h, d, pg = cache_k.shape[1:]
    assert n_pages % npg == 0

    sfx_mat, pages_mat = _sb_sample_consts(h, npg, pg)
    const2 = lambda bi, j, pt: (0, 0)
    grid_spec = pltpu.PrefetchScalarGridSpec(
        num_scalar_prefetch=1,
        grid=(n, n_pages // npg),
        in_specs=[
            pl.BlockSpec(memory_space=pl.ANY),
            pl.BlockSpec(memory_space=pl.ANY),
            pl.BlockSpec((None, h, d, pg), lambda bi, j, pt: (bi, 0, 0, 0)),
            pl.BlockSpec((h, 1, pg), lambda bi, j, pt: (0, 0, 0)),
            pl.BlockSpec(sfx_mat.shape, const2),
            pl.BlockSpec(pages_mat.shape, const2),
        ],
        out_specs=pl.BlockSpec((None, h, d), lambda bi, j, pt: (bi, 0, 0)),
        scratch_shapes=[
            pltpu.VMEM((depth, npg, h, d, pg), F32),
            pltpu.VMEM((depth, npg, h, d, pg), F32),
            pltpu.SemaphoreType.DMA((depth, 2)),
            pltpu.VMEM((h, d, pg), F32),
            pltpu.VMEM((h, 1, pg), F32),
        ],
    )
    return pl.pallas_call(
        functools.partial(_sb_sample_kernel, npg=npg, depth=depth),
        grid_spec=grid_spec,
        out_shape=jax.ShapeDtypeStruct((n, h, d), F32),
        compiler_params=_params("arbitrary", "arbitrary"),
        name="sb_sample",
    )(page_table, cache_k, cache_v, q_bcast, bias_bcast, sfx_mat, pages_mat)


def _head_ones(width):
    hid = jnp.arange(width) // HEAD_DIM
    return (hid[:, None] == hid[None, :]).astype(BF16)


def _rows_for(t, target):
    return target if t % target == 0 else t


def kernel(x_prompt, mem_prompt, x_sample, cache_sb_k, cache_sb_v, page_table, state_wkv, state_shift, cache_mem_k, cache_mem_v, norm_g, w_in, sb_bias, mu_shift, w0, w_lora_b, a0, a_lora_b, k_k, k_a, r_k, lnx_g, lnx_b, mem_norm_g, w_mem_k, w_mem_v, w_out, final_norm_g):
    depth = w_in.shape[0]
    assert depth == 1, "single-layer trunk"
    l = 0
    b_p, t_p, d = x_prompt.shape
    b_s = x_sample.shape[0]
    assert x_sample.shape[1] == 1
    n_mem = mem_prompt.shape[1]
    page = cache_sb_k.shape[2]

    w16 = w_in[l].astype(BF16)
    wkT = w_mem_k[l].T.astype(BF16)
    wvT = w_mem_v[l].T.astype(BF16)
    wo = w_out[l].astype(BF16)
    fg = final_norm_g.reshape(1, d)
    r_k_flat = r_k[l].reshape(1, RWKV_W)
    ones_all = _head_ones(RWKV_W)
    blk = SB_BLOCK
    ar = jnp.arange(blk)
    cs_mat = (ar[:, None] > ar[None, :]).astype(BF16)

    mkT, mvT = _mem_kv(mem_prompt, mem_norm_g[l:l + 1], wkT, wvT)
    p_rw, q_sb, xq, gate, kT, vT = _in_proj(
        x_prompt, norm_g[l:l + 1], w16, _rows_for(t_p, 1024), True)
    vecs = (mu_shift[l:l + 1], w0[l:l + 1], a0[l:l + 1], k_k[l:l + 1], k_a[l:l + 1],
            r_k_flat, lnx_g[l:l + 1], lnx_b[l:l + 1])
    o_sb = _sb_prompt(q_sb, kT, vT, sb_bias[l], cs_mat)

    xs2 = x_sample.reshape(1, b_s, d)
    p_rw_s, q_s, xq_s, gate_s, sbk_s, sbv_s = _in_proj(xs2, norm_g[l:l + 1], w16, b_s, False)
    ck = cache_sb_k[l].transpose(0, 2, 3, 1)
    cv = cache_sb_v[l].transpose(0, 2, 3, 1)
    q_b = jnp.broadcast_to(
        (q_s[0] * ATTN_SCALE).reshape(b_s, SB_HEADS, HEAD_DIM, 1), (b_s, SB_HEADS, HEAD_DIM, page))
    bias_b = jnp.broadcast_to(sb_bias[l].reshape(SB_HEADS, 1, 1), (SB_HEADS, 1, page))

    prev0 = jnp.zeros((b_p, 1, RWKV_COLS), F32)
    h0 = jnp.zeros((b_p, RWKV_HEADS, HEAD_DIM, HEAD_DIM), F32)
    o_rw, h_fin = _rwkv_prompt(p_rw, prev0, h0, vecs, w_lora_b[l], a_lora_b[l], ones_all)
    o_sb_s = _sb_sample(page_table, ck, cv, q_b, bias_b).reshape(1, b_s, SB_W)
    y_prompt = _out_proj(o_rw, o_sb, xq, gate, x_prompt, wo, fg, _rows_for(t_p, 512), (mkT, mvT))

    def tokens_major(xT, heads):
        return xT.reshape(1, xT.shape[0], heads, HEAD_DIM, xT.shape[2]).transpose(0, 1, 4, 2, 3)

    new_sb_k_p = tokens_major(kT, SB_HEADS)
    new_sb_v_p = tokens_major(vT, SB_HEADS)
    new_wkv_p = jnp.swapaxes(h_fin, -1, -2)[None]
    new_shift_p = p_rw[:, -1][None]
    new_mem_k_p = tokens_major(mkT, X_HEADS)
    new_mem_v_p = tokens_major(mvT, X_HEADS)

    p_rw_s = p_rw_s[0]
    svecs = (mu_shift[l:l + 1], w0[l:l + 1], a0[l:l + 1], k_k[l:l + 1], k_a[l:l + 1], r_k_flat)
    r_s, w_s, k2_s, v_s, kk_s, bv_s, bonus_s = _rwkv_sample_prep(
        p_rw_s, state_shift[l], svecs, w_lora_b[l], a_lora_b[l], ones_all)
    as_row = lambda a_: a_.reshape(b_s, RWKV_HEADS, 1, HEAD_DIM)
    o_rw_s, new_wkv_s = _rwkv_sample_step(
        state_wkv[l],
        [as_row(t_) for t_ in (r_s, w_s, k2_s, kk_s, bv_s, v_s, bonus_s)],
        lnx_g[l].reshape(RWKV_HEADS, 1, HEAD_DIM), lnx_b[l].reshape(RWKV_HEADS, 1, HEAD_DIM),
        SAMPLE_GROUP if b_s % SAMPLE_GROUP == 0 else 1)
    o_rw_s = o_rw_s.reshape(1, b_s, RWKV_W)

    mk_s = cache_mem_k[l].transpose(0, 2, 3, 1).reshape(b_s, X_W, n_mem)
    mv_s = cache_mem_v[l].transpose(0, 2, 3, 1).reshape(b_s, X_W, n_mem)
    xq_rows = jnp.broadcast_to(xq_s[0][:, None, :], (b_s, 8, X_W))
    o_x_s = _xattn(xq_rows, mk_s, mv_s, 8, SAMPLE_GROUP if b_s % SAMPLE_GROUP == 0 else 1)[:, 0][None]
    y_sample = _out_proj(o_rw_s, o_sb_s, o_x_s, gate_s, xs2, wo, fg, b_s).reshape(b_s, 1, d)

    new_sb_k_s = sbk_s.reshape(1, b_s, 1, SB_HEADS, HEAD_DIM)
    new_sb_v_s = sbv_s.reshape(1, b_s, 1, SB_HEADS, HEAD_DIM)
    new_shift_s = p_rw_s[None]

    return (y_prompt, y_sample, new_sb_k_p, new_sb_v_p, new_wkv_p, new_shift_p,
            new_mem_k_p, new_mem_v_p, new_sb_k_s, new_sb_v_s, new_wkv_s[None], new_shift_s)
```
